```python
import jax
import jax.numpy as jnp
from jax import lax
import numpy as np

D_MODEL = 4096
BATCH = 2
SEQ = 8192
DEPTH = 2

CTX_LEN = 256
GRID_W = 64

N_GROUPS = 4
GROUP_W = D_MODEL // N_GROUPS
D_MIX = N_GROUPS * GROUP_W
D_FF = 4 * D_MODEL
BLOCK = 128

SWA_HEAD_DIM = 128
SWA_HEADS = GROUP_W // SWA_HEAD_DIM
SWA_KV_HEADS = SWA_HEADS // 4
SWA_GROUP = SWA_HEADS // SWA_KV_HEADS
WINDOW = 128

LRU_WIDTH = GROUP_W
LRU_BLOCKS = 8
LRU_BLOCK_W = LRU_WIDTH // LRU_BLOCKS
LRU_C = 8.0
CONV_W = 4

MLA_HEADS = 8
MLA_NOPE = 128
MLA_ROPE = 64
MLA_V = GROUP_W // MLA_HEADS
MLA_Q_RANK = GROUP_W
MLA_KV_RANK = D_MODEL // 8

RWKV_HEAD = 64
RWKV_HEADS = GROUP_W // RWKV_HEAD
RWKV_DECAY_LORA = 64
RWKV_A_LORA = 64
RWKV_GATE_LORA = 160
RWKV_GN_EPS = 64e-5

ROPE_BASE = 10000.0
LN_EPS = 1e-5
RMS_EPS = 1e-6
ALPHA = (2 * DEPTH) ** 0.25
BETA = (8 * DEPTH) ** -0.25

SWA_Q = SWA_HEADS * SWA_HEAD_DIM
SWA_KV = SWA_KV_HEADS * SWA_HEAD_DIM
RWKV_SIZES = (GROUP_W, GROUP_W, GROUP_W,
              RWKV_DECAY_LORA, RWKV_DECAY_LORA,
              RWKV_A_LORA, RWKV_A_LORA,
              RWKV_GATE_LORA)
RWKV_COLS = sum(RWKV_SIZES)
RWKV_SPLITS = tuple(int(v) for v in np.cumsum(RWKV_SIZES)[:-1])
COL_SIZES = (SWA_Q, SWA_KV, SWA_KV,
             LRU_WIDTH, LRU_WIDTH,
             MLA_Q_RANK, MLA_KV_RANK, MLA_ROPE,
             RWKV_COLS)
D_IN = sum(COL_SIZES)
COL_SPLITS = tuple(int(v) for v in np.cumsum(COL_SIZES)[:-1])

kernel_name = 'hybrid_parallel_groups_diffusion_block'


def layer_norm(x, g=None, b=None):
    xf = x.astype(jnp.float32)
    mu = jnp.mean(xf, -1, keepdims=True)
    var = jnp.mean(jnp.square(xf - mu), -1, keepdims=True)
    y = (xf - mu) * lax.rsqrt(var + LN_EPS)
    if g is not None:
        y = y * g + b
    return y.astype(x.dtype)


def rms_norm(x, g):
    xf = x.astype(jnp.float32)
    y = xf * lax.rsqrt(jnp.mean(jnp.square(xf), -1, keepdims=True) + RMS_EPS) * g
    return y.astype(x.dtype)


def modulate(x, shift, scale):
    return layer_norm(x) * (1.0 + scale) + shift


def rope_1d(x, pos):
    half = x.shape[-1] // 2
    inv = ROPE_BASE ** (-jnp.arange(half, dtype=jnp.float32) / half)
    ang = pos.astype(jnp.float32)[:, None] * inv[None, :]
    cos = jnp.cos(ang)[:, None, :]
    sin = jnp.sin(ang)[:, None, :]
    xf = x.astype(jnp.float32)
    x1, x2 = xf[..., :half], xf[..., half:]
    return jnp.concatenate([x1 * cos - x2 * sin, x2 * cos + x1 * sin], -1).astype(x.dtype)


def rope_2d(x, row, col):
    d = x.shape[-1] // 2
    return jnp.concatenate([rope_1d(x[..., :d], row), rope_1d(x[..., d:], col)], -1)


def dwconv_centred(x, w, b):
    k_w, ch = w.shape
    left = k_w // 2
    y = lax.conv_general_dilated(
        x,
        w[:, None, :].astype(x.dtype),
        window_strides=(1,),
        padding=[(left, k_w - 1 - left)],
        dimension_numbers=('NWC', 'WIO', 'NWC'),
        feature_group_count=ch,
    )
    return y + b


def token_shift(x, mu_prev, mu_next):
    zero = jnp.zeros_like(x[:, :1])
    x_prev = jnp.concatenate([zero, x[:, :-1]], 1)
    x_next = jnp.concatenate([x[:, 1:], zero], 1)
    return x + mu_prev * (x_prev - x) + mu_next * (x_next - x)


def dense_attn(q, k, v):
    s = jnp.einsum('bqhd,bkhd->bhqk', q, k).astype(jnp.float32) * (q.shape[-1] ** -0.5)
    p = jax.nn.softmax(s, -1).astype(v.dtype)
    return jnp.einsum('bhqk,bkhd->bqhd', p, v)


def band_blocks(t, nb):
    bsz, _, h, d = t.shape
    tp = jnp.pad(t, ((0, 0), (BLOCK, BLOCK), (0, 0), (0, 0))).reshape(bsz, nb + 2, BLOCK, h, d)
    return jnp.concatenate([tp[:, :-2], tp[:, 1:-1], tp[:, 2:]], axis=2)


def band_mask(nb, t_len):
    qi = jnp.arange(BLOCK)[:, None]
    kj = jnp.arange(3 * BLOCK)[None, :] - BLOCK
    band = jnp.abs(kj - qi) <= WINDOW
    kabs = jnp.arange(nb)[:, None] * BLOCK + kj
    valid = (kabs >= 0) & (kabs < t_len)
    return band[None] & valid[:, None, :]


def swa_mixer(q, k, v, qc, kc, vc, sink, row, col, with_ctx):
    bsz, t_len, _ = q.shape
    n_ctx = qc.shape[1]
    nb = t_len // BLOCK
    scale = SWA_HEAD_DIM ** -0.5

    q = rope_2d(q.reshape(bsz, t_len, SWA_HEADS, SWA_HEAD_DIM), row, col)
    k = rope_2d(k.reshape(bsz, t_len, SWA_KV_HEADS, SWA_HEAD_DIM), row, col)
    v = v.reshape(bsz, t_len, SWA_KV_HEADS, SWA_HEAD_DIM)
    kc = kc.reshape(bsz, n_ctx, SWA_KV_HEADS, SWA_HEAD_DIM)
    vc = vc.reshape(bsz, n_ctx, SWA_KV_HEADS, SWA_HEAD_DIM)

    qb = q.reshape(bsz, nb, BLOCK, SWA_KV_HEADS, SWA_GROUP, SWA_HEAD_DIM)
    kb = band_blocks(k, nb)
    vb = band_blocks(v, nb)

    s_loc = jnp.einsum('bnqhgd,bnkhd->bnhgqk', qb, kb).astype(jnp.float32) * scale
    s_loc = jnp.where(band_mask(nb, t_len)[None, :, None, None], s_loc, -jnp.inf)
    s_ctx = jnp.einsum('bnqhgd,bkhd->bnhgqk', qb, kc).astype(jnp.float32) * scale

    sink_hg = sink.astype(jnp.float32).reshape(SWA_KV_HEADS, SWA_GROUP)[:, :, None, None]
    sink_col = jnp.broadcast_to(sink_hg, s_loc.shape[:-1] + (1,))
    p = jax.nn.softmax(jnp.concatenate([s_loc, s_ctx, sink_col], -1), -1).astype(v.dtype)

    nk = 3 * BLOCK
    o = (jnp.einsum('bnhgqk,bnkhd->bnqhgd', p[..., :nk], vb)
         + jnp.einsum('bnhgqk,bkhd->bnqhgd', p[..., nk:nk + n_ctx], vc))
    o = o.reshape(bsz, t_len, SWA_HEADS * SWA_HEAD_DIM)

    o_c = None
    if with_ctx:
        qcg = qc.reshape(bsz, n_ctx, SWA_KV_HEADS, SWA_GROUP, SWA_HEAD_DIM)
        s = jnp.einsum('bqhgd,bkhd->bhgqk', qcg, kc).astype(jnp.float32) * scale
        s = jnp.concatenate([s, jnp.broadcast_to(sink_hg, s.shape[:-1] + (1,))], -1)
        pc = jax.nn.softmax(s, -1)[..., :n_ctx].astype(vc.dtype)
        o_c = jnp.einsum('bhgqk,bkhd->bqhgd', pc, vc).reshape(bsz, n_ctx, SWA_HEADS * SWA_HEAD_DIM)
    return o, o_c


def block_diag(x, w):
    bsz, t_len, ch = x.shape
    xr = x.reshape(bsz, t_len, LRU_BLOCKS, LRU_BLOCK_W)
    return jnp.einsum('btnc,ncd->btnd', xr, w).reshape(bsz, t_len, ch)


def rglru_coeffs(u, wa, ba, wi, bi, lam):
    r = jax.nn.sigmoid(block_diag(u, wa) + ba)
    i = jax.nn.sigmoid(block_diag(u, wi) + bi)
    log_a = -LRU_C * r * jax.nn.softplus(-lam)
    return jnp.exp(log_a), jnp.sqrt(-jnp.expm1(2.0 * log_a)) * (i * u)


def lin_combine(left, right):
    a_l, b_l = left
    a_r, b_r = right
    return a_l * a_r, a_r * b_l + b_r


def lru_scan(a, b, h0, reverse):
    first = -1 if reverse else 0
    b = b.at[:, first].add(a[:, first] * h0)
    _, h = lax.associative_scan(lin_combine, (a, b), reverse=reverse, axis=1)
    return h, h[:, first if reverse else -1]


def rglru_mixer(xb, yb, xbc, ybc, conv_w, conv_b, wa, ba, wi, bi, lam, with_ctx):
    u = dwconv_centred(xb, conv_w, conv_b).astype(jnp.float32)
    uc = dwconv_centred(xbc, conv_w, conv_b).astype(jnp.float32)

    h_sum = 0.0
    hc_sum = 0.0
    for d in range(2):
        rev = d == 1
        ac, bc = rglru_coeffs(uc, wa[d], ba[d], wi[d], bi[d], lam[d])
        hc, hc_last = lru_scan(ac, bc, jnp.zeros_like(uc[:, 0]), rev)
        al, bl = rglru_coeffs(u, wa[d], ba[d], wi[d], bi[d], lam[d])
        hl, _ = lru_scan(al, bl, hc_last, rev)
        h_sum = h_sum + hl
        hc_sum = hc_sum + hc

    o = (h_sum * jax.nn.gelu(yb.astype(jnp.float32))).astype(xb.dtype)
    o_c = (hc_sum * jax.nn.gelu(ybc.astype(jnp.float32))).astype(xb.dtype) if with_ctx else None
    return o, o_c


def mla_project(qa, kva, kr, q_norm, kv_norm, w_qb, w_kvb, row, col, rotate):
    bsz, t_len, _ = qa.shape
    q = (rms_norm(qa, q_norm) @ w_qb).reshape(bsz, t_len, MLA_HEADS, MLA_NOPE + MLA_ROPE)
    kv = (rms_norm(kva, kv_norm) @ w_kvb).reshape(bsz, t_len, MLA_HEADS, MLA_NOPE + MLA_V)
    q_nope, q_rope = q[..., :MLA_NOPE], q[..., MLA_NOPE:]
    k_nope, v = kv[..., :MLA_NOPE], kv[..., MLA_NOPE:]
    k_rope = kr[:, :, None, :]
    if rotate:
        q_rope = rope_2d(q_rope, row, col)
        k_rope = rope_2d(k_rope, row, col)
    k_rope = jnp.broadcast_to(k_rope, (bsz, t_len, MLA_HEADS, MLA_ROPE))
    return (jnp.concatenate([q_nope, q_rope], -1),
            jnp.concatenate([k_nope, k_rope], -1),
            v)


def mla_mixer(qa, kva, kr, qac, kvac, krc, q_norm, kv_norm, w_qb, w_kvb, row, col, with_ctx):
    bsz, t_len, _ = qa.shape
    n_ctx = qac.shape[1]
    nb = t_len // BLOCK

    q, k, v = mla_project(qa, kva, kr, q_norm, kv_norm, w_qb, w_kvb, row, col, True)
    qc, kc, vc = mla_project(qac, kvac, krc, q_norm, kv_norm, w_qb, w_kvb, row, col, False)

    k_all = jnp.concatenate([kc, k], 1)
    v_all = jnp.concatenate([vc, v], 1)
    q_blocks = jnp.moveaxis(q.reshape(bsz, nb, BLOCK, MLA_HEADS, MLA_NOPE + MLA_ROPE), 1, 0)
    o = lax.map(lambda qb: dense_attn(qb, k_all, v_all), q_blocks)
    o = jnp.moveaxis(o, 0, 1).reshape(bsz, t_len, MLA_HEADS * MLA_V)

    o_c = dense_attn(qc, kc, vc).reshape(bsz, n_ctx, MLA_HEADS * MLA_V) if with_ctx else None
    return o, o_c


def rwkv_features(pd, mu_prev, mu_next, w0, w2, a0, a2, g2, k_k, k_a, r_k):
    u = token_shift(pd.astype(jnp.float32), mu_prev, mu_next)
    r, k, v, dw_f, dw_b, da_f, da_b, dg = jnp.split(u, RWKV_SPLITS, axis=-1)
    bsz, t_len, _ = r.shape

    def heads(t):
        return t.reshape(bsz, t_len, RWKV_HEADS, RWKV_HEAD)

    kk = heads(k * k_k)
    kk = kk / jnp.maximum(jnp.sqrt(jnp.sum(jnp.square(kk), -1, keepdims=True)), 1e-12)
    g = jax.nn.sigmoid(dg) @ g2

    dirs = []
    for d, (dw, da) in enumerate(((dw_f, da_f), (dw_b, da_b))):
        w_log = -jax.nn.softplus(-(w0[d] + jnp.tanh(dw) @ w2[d])) - 0.5
        decay = jnp.exp(-jnp.exp(w_log))
        a = jax.nn.sigmoid(a0[d] + da @ a2[d])
        k_d = heads(k * (1.0 + (a - 1.0) * k_a))
        bonus = jnp.sum(heads(r) * k_d * r_k, -1, keepdims=True) * heads(v)
        dirs.append((heads(decay), k_d, kk * heads(a), bonus))
    return heads(r), heads(v), kk, g, dirs


def wkv7_scan(r, decay, k, v, kk, b, s0, reverse):
    def step(s, inp):
        r_t, w_t, k_t, v_t, kk_t, b_t = inp
        s_kk = jnp.einsum('bhvk,bhk->bhv', s, kk_t)
        s = (s * w_t[:, :, None, :]
             - s_kk[..., None] * b_t[:, :, None, :]
             + v_t[..., None] * k_t[:, :, None, :])
        return s, jnp.einsum('bhvk,bhk->bhv', s, r_t)

    xs = tuple(jnp.moveaxis(t, 1, 0) for t in (r, decay, k, v, kk, b))
    s_last, ys = lax.scan(step, s0, xs, reverse=reverse)
    return jnp.moveaxis(ys, 0, 1), s_last


def rwkv_output(y, bonus, g, ln_g, ln_b):
    bsz, t_len, h, n = y.shape
    mu = jnp.mean(y, -1, keepdims=True)
    var = jnp.mean(jnp.square(y - mu), -1, keepdims=True)
    yn = ((y - mu) * lax.rsqrt(var + RWKV_GN_EPS)).reshape(bsz, t_len, h * n) * ln_g + ln_b
    return (yn + bonus.reshape(bsz, t_len, h * n)) * g


def rwkv_mixer(pd, pdc, mu_prev, mu_next, w0, w2, a0, a2, g2, k_k, k_a, r_k, ln_g, ln_b, with_ctx):
    r, v, kk, g, dirs = rwkv_features(pd, mu_prev, mu_next, w0, w2, a0, a2, g2, k_k, k_a, r_k)
    rc, vc, kkc, gc, dirs_c = rwkv_features(pdc, mu_prev, mu_next, w0, w2, a0, a2, g2, k_k, k_a, r_k)
    s0 = jnp.zeros((pd.shape[0], RWKV_HEADS, RWKV_HEAD, RWKV_HEAD), jnp.float32)

    y = 0.0
    yc = 0.0
    bonus = 0.0
    bonus_c = 0.0
    for d in range(2):
        rev = d == 1
        dec_c, k_c, b_c, bo_c = dirs_c[d]
        yc_d, s_ctx = wkv7_scan(rc, dec_c, k_c, vc, kkc, b_c, s0, rev)
        dec, k_d, b_d, bo = dirs[d]
        y_d, _ = wkv7_scan(r, dec, k_d, v, kk, b_d, s_ctx, rev)
        y = y + y_d
        bonus = bonus + bo
        yc = yc + yc_d
        bonus_c = bonus_c + bo_c

    o = rwkv_output(y, bonus, g, ln_g, ln_b).astype(pd.dtype)
    o_c = rwkv_output(yc, bonus_c, gc, ln_g, ln_b).astype(pd.dtype) if with_ctx else None
    return o, o_c


def sq_relu_mlp(h, w1, w2):
    return jnp.square(jax.nn.relu(h @ w1)) @ w2


def setup_inputs(seed: int = 0) -> dict:
    key = jax.random.key(seed)
    ks = iter(jax.random.split(key, 48))

    def nrm(shape, std):
        return std * jax.random.normal(next(ks), shape, jnp.float32)

    def uni(shape, lo, hi):
        return jax.random.uniform(next(ks), shape, jnp.float32, lo, hi)

    L, D = DEPTH, D_MODEL
    lam_a = uni((L, 2, LRU_WIDTH), 0.9, 0.999) ** (1.0 / LRU_C)
    return {
        'x': nrm((BATCH, SEQ, D), 1.0),
        'c': nrm((BATCH, D), 1.0),
        'ctx': nrm((BATCH, CTX_LEN, D), 1.0),
        'c_ctx': nrm((D,), 1.0),
        'w_mod': nrm((L, D, 6 * D), 0.5 * D ** -0.5),
        'b_mod': nrm((L, 6 * D), 0.01),
        'w_in': nrm((L, D, D_IN), D ** -0.5),
        'w_out': nrm((L, D_MIX, D), BETA * D_MIX ** -0.5),
        'ln1_g': 1.0 + nrm((L, D), 0.02),
        'ln1_b': nrm((L, D), 0.02),
        'w_ff1': nrm((L, D, D_FF), D ** -0.5),
        'w_ff2': nrm((L, D_FF, D), BETA * D_FF ** -0.5),
        'ln2_g': 1.0 + nrm((L, D), 0.02),
        'ln2_b': nrm((L, D), 0.02),
        'swa_sink': nrm((L, SWA_HEADS), 0.5),
        'lru_conv_w': nrm((L, CONV_W, LRU_WIDTH), CONV_W ** -0.5),
        'lru_conv_b': nrm((L, LRU_WIDTH), 0.01),
        'lru_wa': nrm((L, 2, LRU_BLOCKS, LRU_BLOCK_W, LRU_BLOCK_W), LRU_BLOCK_W ** -0.5),
        'lru_ba': nrm((L, 2, LRU_WIDTH), 0.1),
        'lru_wi': nrm((L, 2, LRU_BLOCKS, LRU_BLOCK_W, LRU_BLOCK_W), LRU_BLOCK_W ** -0.5),
        'lru_bi': nrm((L, 2, LRU_WIDTH), 0.1),
        'lru_lam': jnp.log(lam_a) - jnp.log1p(-lam_a),
        'mla_q_norm': 1.0 + nrm((L, MLA_Q_RANK), 0.02),
        'mla_kv_norm': 1.0 + nrm((L, MLA_KV_RANK), 0.02),
        'mla_w_qb': nrm((L, MLA_Q_RANK, MLA_HEADS * (MLA_NOPE + MLA_ROPE)), MLA_Q_RANK ** -0.5),
        'mla_w_kvb': nrm((L, MLA_KV_RANK, MLA_HEADS * (MLA_NOPE + MLA_V)), MLA_KV_RANK ** -0.5),
        'rwkv_mu_prev': uni((L, RWKV_COLS), 0.0, 0.5),
        'rwkv_mu_next': uni((L, RWKV_COLS), 0.0, 0.5),
        'rwkv_w0': uni((L, 2, GROUP_W), -6.0, -1.0),
        'rwkv_w2': nrm((L, 2, RWKV_DECAY_LORA, GROUP_W), 0.1),
        'rwkv_a0': nrm((L, 2, GROUP_W), 0.1),
        'rwkv_a2': nrm((L, 2, RWKV_A_LORA, GROUP_W), 0.1),
        'rwkv_g2': nrm((L, RWKV_GATE_LORA, GROUP_W), RWKV_GATE_LORA ** -0.5),
        'rwkv_k_k': 0.85 + nrm((L, GROUP_W), 0.02),
        'rwkv_k_a': 1.0 + nrm((L, GROUP_W), 0.02),
        'rwkv_r_k': nrm((L, RWKV_HEADS, RWKV_HEAD), 0.1),
        'rwkv_ln_g': 1.0 + nrm((L, GROUP_W), 0.02),
        'rwkv_ln_b': nrm((L, GROUP_W), 0.02),
    }


def reference(x, c, ctx, c_ctx, w_mod, b_mod, w_in, w_out, ln1_g, ln1_b, w_ff1, w_ff2, ln2_g, ln2_b,
              swa_sink, lru_conv_w, lru_conv_b, lru_wa, lru_ba, lru_wi, lru_bi, lru_lam,
              mla_q_norm, mla_kv_norm, mla_w_qb, mla_w_kvb,
              rwkv_mu_prev, rwkv_mu_next, rwkv_w0, rwkv_w2, rwkv_a0, rwkv_a2, rwkv_g2,
              rwkv_k_k, rwkv_k_a, rwkv_r_k, rwkv_ln_g, rwkv_ln_b):
    n_lat = x.shape[1]
    rows = n_lat // GRID_W
    row = jnp.repeat(jnp.arange(rows, dtype=jnp.int32), GRID_W)
    col = jnp.arange(n_lat, dtype=jnp.int32) % GRID_W
    xc = ctx

    for l in range(DEPTH):
        with_ctx = l < DEPTH - 1

        mod = jax.nn.silu(c) @ w_mod[l] + b_mod[l]
        mod_c = jax.nn.silu(c_ctx) @ w_mod[l] + b_mod[l]
        sh1, sc1, ga1, sh2, sc2, ga2 = jnp.split(mod[:, None, :], 6, axis=-1)
        csh1, csc1, cga1, csh2, csc2, cga2 = jnp.split(mod_c, 6, axis=-1)

        p = jnp.split(modulate(x, sh1, sc1) @ w_in[l], COL_SPLITS, axis=-1)
        pc = jnp.split(modulate(xc, csh1, csc1) @ w_in[l], COL_SPLITS, axis=-1)

        o_swa, oc_swa = swa_mixer(p[0], p[1], p[2], pc[0], pc[1], pc[2],
                                  swa_sink[l], row, col, with_ctx)
        o_lru, oc_lru = rglru_mixer(p[3], p[4], pc[3], pc[4],
                                    lru_conv_w[l], lru_conv_b[l],
                                    lru_wa[l], lru_ba[l], lru_wi[l], lru_bi[l], lru_lam[l],
                                    with_ctx)
        o_mla, oc_mla = mla_mixer(p[5], p[6], p[7], pc[5], pc[6], pc[7],
                                  mla_q_norm[l], mla_kv_norm[l], mla_w_qb[l], mla_w_kvb[l],
                                  row, col, with_ctx)
        o_rwkv, oc_rwkv = rwkv_mixer(p[8], pc[8],
                                     rwkv_mu_prev[l], rwkv_mu_next[l],
                                     rwkv_w0[l], rwkv_w2[l], rwkv_a0[l], rwkv_a2[l], rwkv_g2[l],
                                     rwkv_k_k[l], rwkv_k_a[l], rwkv_r_k[l],
                                     rwkv_ln_g[l], rwkv_ln_b[l], with_ctx)

        mix = jnp.concatenate([o_swa, o_lru, o_mla, o_rwkv], -1) @ w_out[l]
        x = layer_norm(ALPHA * x + ga1 * mix, ln1_g[l], ln1_b[l])
        x = layer_norm(ALPHA * x + ga2 * sq_relu_mlp(modulate(x, sh2, sc2), w_ff1[l], w_ff2[l]),
                       ln2_g[l], ln2_b[l])

        if with_ctx:
            mix_c = jnp.concatenate([oc_swa, oc_lru, oc_mla, oc_rwkv], -1) @ w_out[l]
            xc = layer_norm(ALPHA * xc + cga1 * mix_c, ln1_g[l], ln1_b[l])
            xc = layer_norm(ALPHA * xc + cga2 * sq_relu_mlp(modulate(xc, csh2, csc2), w_ff1[l], w_ff2[l]),
                            ln2_g[l], ln2_b[l])
    return x
```

```python
import functools

import numpy as np
import jax
import jax.numpy as jnp
from jax import lax
from jax.experimental import pallas as pl
from jax.experimental.pallas import tpu as pltpu

F32 = jnp.float32
BF16 = jnp.bfloat16

D_MODEL = 4096
GRID_W = 64
GROUP_W = 1024
D_FF = 4 * D_MODEL
BLOCK = 128

SWA_HEAD_DIM = 128
SWA_HEADS = 8
SWA_KV_HEADS = 2
SWA_GROUP = 4

LRU_BLOCKS = 8
LRU_BLOCK_W = 128
LRU_C = 8.0

MLA_HEADS = 8
MLA_NOPE = 128
MLA_ROPE = 64
MLA_V = 128
MLA_HEAD_PAD = 256

RWKV_HEAD = 64
RWKV_HEADS = 16
RWKV_GATE_LORA = 160
RWKV_GN_EPS = 64e-5
WKV_CHUNK = 64
WKV_HEADS_PER_STEP = 4
WKV_INV_BLOCK = 16

ROPE_BASE = 10000.0
LN_EPS = 1e-5
RMS_EPS = 1e-6
DEPTH = 2
ALPHA = (2 * DEPTH) ** 0.25

OFF_AQ, OFF_BX, OFF_BG, OFF_CQ, OFF_DR, OFF_DK, OFF_DV = 0, 1024, 2048, 3072, 4096, 5120, 6144
OFF_CKV, OFF_AK, OFF_AV, OFF_DL, OFF_CR = 7168, 7680, 7936, 8192, 8704
N_PROJ = 9216
_PROJ_SEGMENTS = ((0, 1024), (1536, 2560), (2560, 3584), (3584, 4608), (5184, 6208), (6208, 7232),
                  (7232, 8256), (4608, 5120), (1024, 1280), (1280, 1536), (8256, 8672), 96,
                  (5120, 5184), 64, 384)

VMEM_LIMIT_V7X = 56 * 1024 * 1024


def _params(*sem):
    return pltpu.CompilerParams(dimension_semantics=sem, vmem_limit_bytes=VMEM_LIMIT_V7X)


def _split2(a):
    hi = a.astype(BF16)
    lo = (a - hi.astype(F32)).astype(BF16)
    return hi, lo


def _mm(a, b):
    return jnp.dot(a, b, preferred_element_type=F32)


def _mm_nt(a, b):
    return lax.dot_general(a, b, (((1,), (1,)), ((), ())), preferred_element_type=F32)


def _mm_tn(a, b):
    return lax.dot_general(a, b, (((0,), (0,)), ((), ())), preferred_element_type=F32)


def _mm3(a, b, mm=_mm):
    ah, al = _split2(a)
    bh, bl = _split2(b)
    return mm(ah, bh) + (mm(al, bh) + mm(ah, bl))


def _mm3w(a, bh, bl):
    ah, al = _split2(a)
    return _mm(ah, bh) + (_mm(al, bh) + _mm(ah, bl))


def _mm1(a, b, mm=_mm):
    return mm(a.astype(BF16), b.astype(BF16))


def _seg_sum(x, bd):
    parts = []
    for j in range(x.shape[1] // 128):
        hi, lo = _split2(x[:, 128 * j:128 * (j + 1)])
        parts.append(_mm(hi, bd) + _mm(lo, bd))
    return jnp.concatenate(parts, axis=1)


def _layer_norm_rows(x):
    mu = jnp.mean(x, -1, keepdims=True)
    xc = x - mu
    var = jnp.mean(xc * xc, -1, keepdims=True)
    return xc * lax.rsqrt(var + LN_EPS)


def _softplus(z):
    return jnp.maximum(z, 0.0) + jnp.log1p(jnp.exp(-jnp.abs(z)))


def _gelu_tanh(x):
    return 0.5 * x * (1.0 + jnp.tanh(np.sqrt(2.0 / np.pi).astype(np.float32) * (x + 0.044715 * (x * x * x))))


def _rope(x, cos, sin_a, sin_b, half):
    return x * cos + pltpu.roll(x, 128 - half, 1) * sin_a + pltpu.roll(x, half, 1) * sin_b


def _shift_rows(x, row, tm, prev_rows, next_row):
    p6, p7 = prev_rows
    xm1 = jnp.where(row == 0, p7, pltpu.roll(x, 1, 0))
    xp1 = jnp.where(row == tm - 1, next_row, pltpu.roll(x, tm - 1, 0))
    xm2 = None
    if p6 is not None:
        xm2 = jnp.where(row == 0, p6, jnp.where(row == 1, p7, pltpu.roll(x, 2, 0)))
    return xm2, xm1, xp1


def _halo_specs(col, width, tm, m_rows):
    r8 = tm // 8
    last8 = m_rows // 8 - 1
    return [pl.BlockSpec((tm, width), lambda i: (i, col)),
            pl.BlockSpec((8, width), lambda i: (jnp.maximum(i * r8 - 1, 0), col)),
            pl.BlockSpec((8, width), lambda i: (jnp.minimum((i + 1) * r8, last8), col))]


def _full(shape):
    nd = len(shape)
    return pl.BlockSpec(shape, lambda *_: (0,) * nd)


def _mod_kernel(c_ref, w_ref, b_ref, o_ref):
    cc = c_ref[...]
    a = cc * jax.nn.sigmoid(cc)
    o_ref[...] = _mm3(a, w_ref[...]) + b_ref[...]


def _modulation(cc, w_mod, b_mod, tn=512):
    depth, d, n = w_mod.shape
    return pl.pallas_call(
        _mod_kernel,
        out_shape=jax.ShapeDtypeStruct((depth, 8, n), F32),
        grid=(depth, n // tn),
        in_specs=[pl.BlockSpec((8, d), lambda l, j: (0, 0)),
                  pl.BlockSpec((None, d, tn), lambda l, j: (l, 0, j)),
                  pl.BlockSpec((None, 1, tn), lambda l, j: (l, 0, j))],
        out_specs=pl.BlockSpec((None, 8, tn), lambda l, j: (l, 0, j)),
        compiler_params=_params("parallel", "parallel"),
        name="adaln_modulation",
    )(cc, w_mod, b_mod.reshape(depth, 1, n))


def _lnmm_kernel(x_ref, sh_ref, sc_ref, w_ref, o_ref, xn_ref, *, act, tm):
    @pl.when(pl.program_id(1) == 0)
    def _():
        scale = 1.0 + sc_ref[...]
        shift = sh_ref[...]
        rows = min(tm, 128)

        def body(r, carry):
            sl = pl.ds(pl.multiple_of(r * rows, rows), rows)
            xn_ref[sl, :] = (_layer_norm_rows(x_ref[sl, :]) * scale + shift).astype(BF16)
            return carry

        lax.fori_loop(0, tm // rows, body, 0)

    acc = _mm(xn_ref[...], w_ref[...])
    if act == "relu2":
        acc = jnp.maximum(acc, 0.0)
        acc = acc * acc
    o_ref[...] = acc.astype(o_ref.dtype)


def _ln_mod_matmul(x, shift, scale, w, *, act, out_dtype, tm=512, tn=1024):
    m, k = x.shape
    n = w.shape[1]
    nbm = shift.shape[0]
    tm = min(tm, m // nbm)
    seq_tiles = m // nbm // tm
    mod_spec = pl.BlockSpec((None, 1, k), lambda i, j: (i // seq_tiles, 0, 0))
    return pl.pallas_call(
        functools.partial(_lnmm_kernel, act=act, tm=tm),
        out_shape=jax.ShapeDtypeStruct((m, n), out_dtype),
        grid=(m // tm, n // tn),
        in_specs=[pl.BlockSpec((tm, k), lambda i, j: (i, 0)), mod_spec, mod_spec,
                  pl.BlockSpec((k, tn), lambda i, j: (0, j))],
        out_specs=pl.BlockSpec((tm, tn), lambda i, j: (i, j)),
        scratch_shapes=[pltpu.VMEM((tm, k), BF16)],
        compiler_params=_params("parallel", "arbitrary"),
        name="ln_mod_matmul_" + (act or "id"),
    )(x, shift, scale, w)


def _mmln_kernel(a_ref, w_ref, x_ref, ga_ref, g_ref, b_ref, o_ref, *, nk, tm):
    k = pl.program_id(1)

    @pl.when(k == 0)
    def _():
        o_ref[...] = jnp.zeros_like(o_ref)

    o_ref[...] += _mm(a_ref[...], w_ref[...])

    @pl.when(k == nk - 1)
    def _():
        gate = ga_ref[...]
        g = g_ref[...]
        b = b_ref[...]
        rows = min(tm, 128)

        def body(r, carry):
            sl = pl.ds(pl.multiple_of(r * rows, rows), rows)
            z = ALPHA * x_ref[sl, :] + gate * o_ref[sl, :]
            o_ref[sl, :] = _layer_norm_rows(z) * g + b
            return carry

        lax.fori_loop(0, tm // rows, body, 0)


def _matmul_res_ln(a, w, xres, gate, g, b, *, tm=512, tk=512):
    m, kdim = a.shape
    n = w.shape[1]
    nbm = gate.shape[0]
    tm = min(tm, m // nbm)
    seq_tiles = m // nbm // tm
    nk = kdim // tk
    return pl.pallas_call(
        functools.partial(_mmln_kernel, nk=nk, tm=tm),
        out_shape=jax.ShapeDtypeStruct((m, n), F32),
        grid=(m // tm, nk),
        in_specs=[pl.BlockSpec((tm, tk), lambda i, k: (i, k)),
                  pl.BlockSpec((tk, n), lambda i, k: (k, 0)),
                  pl.BlockSpec((tm, n), lambda i, k: (i, 0)),
                  pl.BlockSpec((None, 1, n), lambda i, k: (i // seq_tiles, 0, 0)),
                  _full((1, n)), _full((1, n))],
        out_specs=pl.BlockSpec((tm, n), lambda i, k: (i, 0)),
        compiler_params=_params("parallel", "arbitrary"),
        name="matmul_res_ln",
    )(a, w, xres, gate, g.reshape(1, n), b.reshape(1, n))


def _swa_prep_kernel(*refs, rotate):
    if rotate:
        q_ref, kv_ref, cos_ref, sa_ref, sb_ref, qo_ref, ko_ref, vo_ref = refs
        cos, sa, sb = cos_ref[...], sa_ref[...], sb_ref[...]
    else:
        q_ref, kv_ref, qo_ref, ko_ref, vo_ref = refs
    scale = np.float32(SWA_HEAD_DIM ** -0.5)
    for h in range(SWA_HEADS):
        sl = slice(128 * h, 128 * (h + 1))
        q = q_ref[:, sl]
        if rotate:
            q = _rope(q, cos, sa, sb, 32)
        qo_ref[:, sl] = (q * scale).astype(BF16)
    for h in range(SWA_KV_HEADS):
        sl = slice(128 * h, 128 * (h + 1))
        k = kv_ref[:, sl]
        if rotate:
            k = _rope(k, cos, sa, sb, 32)
        ko_ref[:, sl] = k.astype(BF16)
    vo_ref[...] = kv_ref[:, 256:512].astype(BF16)


def _swa_prep(p, t_len, tables, tm=256):
    m = p.shape[0]
    tm = min(tm, t_len)
    seq_tiles = t_len // tm
    rotate = tables is not None
    in_specs = [pl.BlockSpec((tm, 1024), lambda i: (i, OFF_AQ // 1024)),
                pl.BlockSpec((tm, 512), lambda i: (i, OFF_AK // 512))]
    args = [p, p]
    if rotate:
        in_specs += [pl.BlockSpec((tm, 128), lambda i: (i % seq_tiles, 0))] * 3
        args += list(tables)
    return pl.pallas_call(
        functools.partial(_swa_prep_kernel, rotate=rotate),
        out_shape=(jax.ShapeDtypeStruct((m, 1024), BF16), jax.ShapeDtypeStruct((m, 256), BF16),
                   jax.ShapeDtypeStruct((m, 256), BF16)),
        grid=(m // tm,),
        in_specs=in_specs,
        out_specs=(pl.BlockSpec((tm, 1024), lambda i: (i, 0)), pl.BlockSpec((tm, 256), lambda i: (i, 0)),
                   pl.BlockSpec((tm, 256), lambda i: (i, 0))),
        compiler_params=_params("parallel"),
        name="swa_prep_rope" if rotate else "swa_prep",
    )(*args)


def _sink_softmax_pv(s, sink, v):
    m = jnp.maximum(jnp.max(s, -1, keepdims=True), sink)
    p = jnp.exp(s - m)
    den = jnp.sum(p, -1, keepdims=True) + jnp.exp(sink - m)
    return _mm(p.astype(BF16), v) / den


def _swa_kernel(q_ref, kp_ref, ko_ref, kn_ref, vp_ref, vo_ref, vn_ref, kc_ref, vc_ref, sink_ref, o_ref, *, nb, n_ctx):
    n = pl.program_id(1)
    qi = lax.broadcasted_iota(jnp.int32, (BLOCK, BLOCK), 0)
    kj = lax.broadcasted_iota(jnp.int32, (BLOCK, BLOCK), 1)
    neg = np.float32(-np.inf)
    bias_prev = jnp.where(jnp.logical_and(kj >= qi, n > 0), 0.0, neg)
    bias_next = jnp.where(jnp.logical_and(kj <= qi, n < nb - 1), 0.0, neg)
    bias = jnp.concatenate([bias_prev, jnp.zeros((BLOCK, BLOCK), F32), bias_next,
                            jnp.zeros((BLOCK, n_ctx), F32)], axis=1)
    for hh in range(SWA_KV_HEADS):
        sl = slice(128 * hh, 128 * (hh + 1))
        k_all = jnp.concatenate([kp_ref[:, sl], ko_ref[:, sl], kn_ref[:, sl], kc_ref[:, sl]], axis=0)
        v_all = jnp.concatenate([vp_ref[:, sl], vo_ref[:, sl], vn_ref[:, sl], vc_ref[:, sl]], axis=0)
        for g in range(SWA_GROUP):
            h = hh * SWA_GROUP + g
            hs = slice(128 * h, 128 * (h + 1))
            s = _mm_nt(q_ref[:, hs], k_all) + bias
            o_ref[:, hs] = _sink_softmax_pv(s, sink_ref[0:1, h:h + 1], v_all).astype(BF16)


def _swa_attn(q, k, v, kc, vc, sink, bsz, t_len, n_ctx):
    nb = t_len // BLOCK

    def blk(width, off):
        def idx(b, n):
            return (b * nb + jnp.clip(n + off, 0, nb - 1), 0)
        return pl.BlockSpec((BLOCK, width), idx)

    ctx_spec = pl.BlockSpec((n_ctx, 256), lambda b, n: (b, 0))
    return pl.pallas_call(
        functools.partial(_swa_kernel, nb=nb, n_ctx=n_ctx),
        out_shape=jax.ShapeDtypeStruct((bsz * t_len, 1024), BF16),
        grid=(bsz, nb),
        in_specs=[blk(1024, 0), blk(256, -1), blk(256, 0), blk(256, 1), blk(256, -1), blk(256, 0), blk(256, 1),
                  ctx_spec, ctx_spec, _full((1, SWA_HEADS))],
        out_specs=blk(1024, 0),
        compiler_params=_params("parallel", "parallel"),
        name="swa_attention",
    )(q, k, k, k, v, v, v, kc, vc, sink.reshape(1, SWA_HEADS))


def _swa_ctx_kernel(q_ref, kc_ref, vc_ref, sink_ref, o_ref):
    for hh in range(SWA_KV_HEADS):
        sl = slice(128 * hh, 128 * (hh + 1))
        for g in range(SWA_GROUP):
            h = hh * SWA_GROUP + g
            hs = slice(128 * h, 128 * (h + 1))
            s = _mm_nt(q_ref[:, hs], kc_ref[:, sl])
            o_ref[:, hs] = _sink_softmax_pv(s, sink_ref[0:1, h:h + 1], vc_ref[:, sl]).astype(BF16)


def _swa_ctx_attn(qc, kc, vc, sink, bsz, n_ctx):
    return pl.pallas_call(
        _swa_ctx_kernel,
        out_shape=jax.ShapeDtypeStruct((bsz * n_ctx, 1024), BF16),
        grid=(bsz,),
        in_specs=[pl.BlockSpec((n_ctx, 1024), lambda b: (b, 0)), pl.BlockSpec((n_ctx, 256), lambda b: (b, 0)),
                  pl.BlockSpec((n_ctx, 256), lambda b: (b, 0)), _full((1, SWA_HEADS))],
        out_specs=pl.BlockSpec((n_ctx, 1024), lambda b: (b, 0)),
        compiler_params=_params("parallel"),
        name="swa_ctx_attention",
    )(qc, kc, vc, sink.reshape(1, SWA_HEADS))


def _lru_coef_kernel(x_ref, xp_ref, xn_ref, cw_ref, cb_ref, wh_ref, wl_ref, bg_ref, lam_ref,
                     af_ref, bf_ref, ab_ref, bb_ref, *, tm, seq_tiles):
    i = pl.program_id(0) % seq_tiles
    first = i == 0
    last = i == seq_tiles - 1
    x = x_ref[...]
    row = lax.broadcasted_iota(jnp.int32, x.shape, 0)
    p6 = jnp.where(first, 0.0, xp_ref[6:7, :])
    p7 = jnp.where(first, 0.0, xp_ref[7:8, :])
    n0 = jnp.where(last, 0.0, xn_ref[0:1, :])
    xm2, xm1, xp1 = _shift_rows(x, row, tm, (p6, p7), n0)
    u = (cw_ref[0:1, :] * xm2 + cw_ref[1:2, :] * xm1 + cw_ref[2:3, :] * x + cw_ref[3:4, :] * xp1) + cb_ref[...]
    sp = _softplus(-lam_ref[...])
    outs = ((af_ref, bf_ref), (ab_ref, bb_ref))
    for n in range(LRU_BLOCKS):
        sl = slice(128 * n, 128 * (n + 1))
        un = u[:, sl]
        z = _mm3w(un, wh_ref[n], wl_ref[n]) + bg_ref[n]
        for d in range(2):
            r = jax.nn.sigmoid(z[:, 256 * d:256 * d + 128])
            gi = jax.nn.sigmoid(z[:, 256 * d + 128:256 * d + 256])
            log_a = (-LRU_C) * r * sp[d:d + 1, sl]
            a_ref, b_ref = outs[d]
            a_ref[:, sl] = jnp.exp(log_a)
            b_ref[:, sl] = jnp.sqrt(1.0 - jnp.exp(2.0 * log_a)) * (gi * un)


def _lru_coef(p, t_len, conv_w, conv_b, wg_hi, wg_lo, bg, lam, tm=256):
    m = p.shape[0]
    tm = min(tm, t_len)
    seq_tiles = t_len // tm
    out = jax.ShapeDtypeStruct((m, 1024), F32)
    ospec = pl.BlockSpec((tm, 1024), lambda i: (i, 0))
    return pl.pallas_call(
        functools.partial(_lru_coef_kernel, tm=tm, seq_tiles=seq_tiles),
        out_shape=(out, out, out, out),
        grid=(m // tm,),
        in_specs=_halo_specs(OFF_BX // 1024, 1024, tm, m) + [
            _full((4, 1024)), _full((1, 1024)), _full((8, 128, 512)), _full((8, 128, 512)),
            _full((8, 1, 512)), _full((2, 1024))],
        out_specs=(ospec, ospec, ospec, ospec),
        compiler_params=_params("parallel"),
        name="lru_coefficients",
    )(p, p, p, conv_w, conv_b, wg_hi, wg_lo, bg, lam)


def _lru_scan_kernel(*refs, reverse, final, nc, tc):
    if final:
        a_ref, b_ref, h0_ref, hf_ref, gate_ref, o_ref, hl_ref, hs_ref, hrows_ref = refs
    else:
        a_ref, b_ref, h0_ref, o_ref, hl_ref, hs_ref = refs
        hrows_ref = o_ref
    c = pl.program_id(1)

    @pl.when(c == 0)
    def _():
        hs_ref[...] = h0_ref[...]

    def body(t, h):
        tt = (tc - 1 - t) if reverse else t
        h = a_ref[pl.ds(tt, 1), :] * h + b_ref[pl.ds(tt, 1), :]
        hrows_ref[pl.ds(tt, 1), :] = h
        return h

    h = lax.fori_loop(0, tc, body, hs_ref[...], unroll=8)
    hs_ref[...] = h

    @pl.when(c == (0 if reverse else nc - 1))
    def _():
        hl_ref[...] = hrows_ref[tc - 1:tc, :]

    if final:
        o_ref[...] = ((hf_ref[...] + hrows_ref[...]) * _gelu_tanh(gate_ref[...])).astype(BF16)


def _lru_scan(a, b, h0, bsz, t_len, *, reverse, h_other=None, p=None, tc=256):
    tc = min(tc, t_len)
    nc = t_len // tc
    final = h_other is not None

    def row_idx(bi, c):
        return bi * nc + ((nc - 1 - c) if reverse else c)

    rows = pl.BlockSpec((tc, 1024), lambda bi, c: (row_idx(bi, c), 0))
    state = pl.BlockSpec((None, 1, 1024), lambda bi, c: (bi, 0, 0))
    in_specs = [rows, rows, state]
    args = [a, b, h0]
    scratch = [pltpu.VMEM((1, 1024), F32)]
    if final:
        in_specs += [rows, pl.BlockSpec((tc, 1024), lambda bi, c: (row_idx(bi, c), OFF_BG // 1024))]
        args += [h_other, p]
        scratch.append(pltpu.VMEM((tc, 1024), F32))
    return pl.pallas_call(
        functools.partial(_lru_scan_kernel, reverse=reverse, final=final, nc=nc, tc=tc),
        out_shape=(jax.ShapeDtypeStruct((bsz * t_len, 1024), BF16 if final else F32),
                   jax.ShapeDtypeStruct((bsz, 1, 1024), F32)),
        grid=(bsz, nc),
        in_specs=in_specs,
        out_specs=(rows, state),
        scratch_shapes=scratch,
        compiler_params=_params("parallel", "arbitrary"),
        name="lru_scan_" + ("bwd_out" if final else ("bwd" if reverse else "fwd")),
    )(*args)


def _mla_proj_kernel(*refs, rotate):
    if rotate:
        (qa_ref, kva_ref, kr_ref, qn_ref, kvn_ref, wq_ref, wkv_ref, cos_ref, sa_ref, sb_ref,
         q_ref, k_ref, v_ref) = refs
        cos, sa, sb = cos_ref[...], sa_ref[...], sb_ref[...]
    else:
        qa_ref, kva_ref, kr_ref, qn_ref, kvn_ref, wq_ref, wkv_ref, q_ref, k_ref, v_ref = refs

    def rms(x, g):
        return (x * lax.rsqrt(jnp.mean(x * x, -1, keepdims=True) + RMS_EPS) * g).astype(BF16)

    scale = np.float32((MLA_NOPE + MLA_ROPE) ** -0.5)
    q = _mm(rms(qa_ref[...], qn_ref[...]), wq_ref[...]) * scale
    kv = _mm(rms(kva_ref[...], kvn_ref[...]), wkv_ref[...])
    kr = kr_ref[...]
    if rotate:
        kr = _rope(kr, cos, sa, sb, 16)
    kr = kr.astype(BF16)
    for h in range(MLA_HEADS):
        lo = MLA_HEAD_PAD * h
        q_ref[:, lo:lo + 128] = q[:, lo:lo + 128].astype(BF16)
        qr = q[:, lo + 128:lo + 256]
        if rotate:
            qr = _rope(qr, cos, sa, sb, 16)
        q_ref[:, lo + 128:lo + 256] = qr.astype(BF16)
        k_ref[:, lo:lo + 128] = kv[:, 128 * h:128 * (h + 1)].astype(BF16)
        k_ref[:, lo + 128:lo + 256] = kr
    v_ref[...] = kv[:, 1024:2048].astype(BF16)


def _mla_proj(p, t_len, q_norm, kv_norm, wq, wkv, tables, tm=256):
    m = p.shape[0]
    tm = min(tm, t_len)
    seq_tiles = t_len // tm
    rotate = tables is not None
    in_specs = [pl.BlockSpec((tm, 1024), lambda i: (i, OFF_CQ // 1024)),
                pl.BlockSpec((tm, 512), lambda i: (i, OFF_CKV // 512)),
                pl.BlockSpec((tm, 128), lambda i: (i, OFF_CR // 128)),
                _full((1, 1024)), _full((1, 512)), _full((1024, 2048)), _full((512, 2048))]
    args = [p, p, p, q_norm.reshape(1, -1), kv_norm.reshape(1, -1), wq, wkv]
    if rotate:
        in_specs += [pl.BlockSpec((tm, 128), lambda i: (i % seq_tiles, 0))] * 3
        args += list(tables)
    wide = pl.BlockSpec((tm, 2048), lambda i: (i, 0))
    return pl.pallas_call(
        functools.partial(_mla_proj_kernel, rotate=rotate),
        out_shape=(jax.ShapeDtypeStruct((m, 2048), BF16), jax.ShapeDtypeStruct((m, 2048), BF16),
                   jax.ShapeDtypeStruct((m, 1024), BF16)),
        grid=(m // tm,),
        in_specs=in_specs,
        out_specs=(wide, wide, pl.BlockSpec((tm, 1024), lambda i: (i, 0))),
        compiler_params=_params("parallel"),
        name="mla_project_rope" if rotate else "mla_project",
    )(*args)


def _mla_flash_kernel(*refs, with_lat, n_chunks, ck):
    if with_lat:
        q_ref, kc_ref, vc_ref, k_ref, v_ref, o_ref = refs
    else:
        q_ref, kc_ref, vc_ref, o_ref = refs
    q = q_ref[...]
    s = _mm_nt(q, kc_ref[...])
    m = jnp.max(s, -1, keepdims=True)
    p = jnp.exp(s - m)
    l = jnp.sum(p, -1, keepdims=True)
    acc = _mm(p.astype(BF16), vc_ref[...])
    if with_lat:
        def body(j, carry):
            m, l, acc = carry
            sl = pl.ds(pl.multiple_of(j * ck, ck), ck)
            s = _mm_nt(q, k_ref[sl, :])
            m_new = jnp.maximum(m, jnp.max(s, -1, keepdims=True))
            alpha = jnp.exp(m - m_new)
            p = jnp.exp(s - m_new)
            l = alpha * l + jnp.sum(p, -1, keepdims=True)
            acc = alpha * acc + _mm(p.astype(BF16), v_ref[sl, :])
            return m_new, l, acc

        m, l, acc = lax.fori_loop(0, n_chunks, body, (m, l, acc))
    o_ref[...] = (acc / l).astype(BF16)


def _mla_flash(q, kc, vc, bsz, tq_len, n_ctx, k=None, v=None, tq=512, ck=512):
    with_lat = k is not None
    tq = min(tq, tq_len)
    nq = tq_len // tq
    in_specs = [pl.BlockSpec((tq, MLA_HEAD_PAD), lambda b, h, i: (b * nq + i, h)),
                pl.BlockSpec((n_ctx, MLA_HEAD_PAD), lambda b, h, i: (b, h)),
                pl.BlockSpec((n_ctx, MLA_V), lambda b, h, i: (b, h))]
    args = [q, kc, vc]
    n_chunks = 0
    if with_lat:
        t_len = k.shape[0] // bsz
        ck = min(ck, t_len)
        n_chunks = t_len // ck
        in_specs += [pl.BlockSpec((t_len, MLA_HEAD_PAD), lambda b, h, i: (b, h)),
                     pl.BlockSpec((t_len, MLA_V), lambda b, h, i: (b, h))]
        args += [k, v]
    return pl.pallas_call(
        functools.partial(_mla_flash_kernel, with_lat=with_lat, n_chunks=n_chunks, ck=ck),
        out_shape=jax.ShapeDtypeStruct((bsz * tq_len, 1024), BF16),
        grid=(bsz, MLA_HEADS, nq),
        in_specs=in_specs,
        out_specs=pl.BlockSpec((tq, MLA_V), lambda b, h, i: (b * nq + i, h)),
        compiler_params=_params("parallel", "parallel", "parallel"),
        name="mla_flash" if with_lat else "mla_ctx_attention",
    )(*args)


def _rwkv_feat_kernel(xr_ref, xrp_ref, xrn_ref, xk_ref, xkp_ref, xkn_ref, xv_ref, xvp_ref, xvn_ref,
                      xl_ref, xlp_ref, xln_ref, mup_ref, mun_ref, w0_ref, a0_ref,
                      w2h_ref, w2l_ref, a2h_ref, a2l_ref, g2h_ref, g2l_ref, kk_w_ref, ka_ref, rk_ref, bd_ref,
                      r_o, v_o, kk_o, g_o, bonus_o, lwf_o, kf_o, bf_o, lwb_o, kb_o, bb_o, *, tm, seq_tiles):
    i = pl.program_id(0) % seq_tiles
    first = i == 0
    last = i == seq_tiles - 1

    def shifted(x_ref, xp_ref, xn_ref, lo, hi):
        x = x_ref[...]
        row = lax.broadcasted_iota(jnp.int32, x.shape, 0)
        p7 = jnp.where(first, 0.0, xp_ref[7:8, :])
        n0 = jnp.where(last, 0.0, xn_ref[0:1, :])
        _, xm1, xp1 = _shift_rows(x, row, tm, (None, p7), n0)
        return x + mup_ref[:, lo:hi] * (xm1 - x) + mun_ref[:, lo:hi] * (xp1 - x)

    r = shifted(xr_ref, xrp_ref, xrn_ref, 0, 1024)
    k = shifted(xk_ref, xkp_ref, xkn_ref, 1024, 2048)
    v = shifted(xv_ref, xvp_ref, xvn_ref, 2048, 3072)
    lo = shifted(xl_ref, xlp_ref, xln_ref, 3072, 3584)
    bd = bd_ref[...]

    kkr = k * kk_w_ref[...]
    kk = kkr / jnp.maximum(jnp.sqrt(_seg_sum(kkr * kkr, bd)), 1e-12)
    g = _mm3w(jax.nn.sigmoid(lo[:, 256:512]), g2h_ref[...], g2l_ref[...])
    wl = w0_ref[...] + _mm3w(jnp.tanh(lo[:, 0:128]), w2h_ref[...], w2l_ref[...])
    lw = -jnp.exp(-_softplus(-wl) - 0.5)
    a = jax.nn.sigmoid(a0_ref[...] + _mm3w(lo[:, 128:256], a2h_ref[...], a2l_ref[...]))
    r_o[...] = r
    v_o[...] = v
    kk_o[...] = kk
    g_o[...] = g
    bonus = None
    for d, (lw_o, k_o, b_o) in enumerate(((lwf_o, kf_o, bf_o), (lwb_o, kb_o, bb_o))):
        a_d = a[:, 1024 * d:1024 * (d + 1)]
        k_d = k * (1.0 + (a_d - 1.0) * ka_ref[...])
        bo = _seg_sum(r * k_d * rk_ref[...], bd) * v
        bonus = bo if bonus is None else bonus + bo
        lw_o[...] = lw[:, 1024 * d:1024 * (d + 1)]
        k_o[...] = k_d
        b_o[...] = kk * a_d
    bonus_o[...] = bonus


def _rwkv_feat(p, t_len, wts, tm=256):
    m = p.shape[0]
    tm = min(tm, t_len)
    seq_tiles = t_len // tm
    in_specs = (_halo_specs(OFF_DR // 1024, 1024, tm, m) + _halo_specs(OFF_DK // 1024, 1024, tm, m)
                + _halo_specs(OFF_DV // 1024, 1024, tm, m) + _halo_specs(OFF_DL // 512, 512, tm, m)
                + [_full(w.shape) for w in wts])
    out = jax.ShapeDtypeStruct((m, 1024), F32)
    ospec = pl.BlockSpec((tm, 1024), lambda i: (i, 0))
    return pl.pallas_call(
        functools.partial(_rwkv_feat_kernel, tm=tm, seq_tiles=seq_tiles),
        out_shape=(out,) * 11,
        grid=(m // tm,),
        in_specs=in_specs,
        out_specs=(ospec,) * 11,
        compiler_params=_params("parallel"),
        name="rwkv_features",
    )(*([p] * 12), *wts)


def _unit_tri_inverse(lmat, eye, blk):
    dg = jnp.where(blk, lmat, 0.0)
    off = lmat - dg
    d2 = _mm3(dg, dg)
    d4 = _mm3(d2, d2)
    d8 = _mm3(d4, d4)
    x = eye - dg
    x = x + _mm3(x, d2)
    x = x + _mm3(x, d4)
    t16 = x + _mm3(x, d8)
    n1 = _mm3(t16, off)
    n2 = _mm3(n1, n1)
    y1 = t16 - _mm3(n1, t16)
    return y1 + _mm3(n2, y1)


def _wkv_kernel(r_ref, v_ref, kk_ref, lw_ref, k_ref, b_ref, s0_ref, y_ref, sout_ref, s_ref, *, reverse, nc, hp):
    c = pl.program_id(2)
    cl = WKV_CHUNK

    @pl.when(c == 0)
    def _():
        s_ref[...] = s0_ref[...]

    ti = lax.broadcasted_iota(jnp.int32, (cl, cl), 0)
    si = lax.broadcasted_iota(jnp.int32, (cl, cl), 1)
    if reverse:
        incl, strict = si >= ti, si > ti
    else:
        incl, strict = si <= ti, si < ti
    blk = (ti // WKV_INV_BLOCK) == (si // WKV_INV_BLOCK)
    eye = jnp.where(ti == si, 1.0, 0.0).astype(F32)
    tri = jnp.where(incl, 1.0, 0.0).astype(BF16)

    lw = lw_ref[...]
    l1 = lw.astype(BF16)
    rem = lw - l1.astype(F32)
    l2 = rem.astype(BF16)
    l3 = (rem - l2.astype(F32)).astype(BF16)
    cum = _mm(tri, l1) + (_mm(tri, l2) + _mm(tri, l3))
    total = cum[0:1, :] if reverse else cum[cl - 1:cl, :]
    e_r = jnp.exp(cum)
    e_k = jnp.exp(-cum)
    e_x = jnp.exp(cum - lw)
    e_t = jnp.exp(total - cum)
    d_c = jnp.exp(total)
    r_t = r_ref[...] * e_r
    kk_t = kk_ref[...] * e_x
    k_raw = k_ref[...]
    b_raw = b_ref[...]
    k_t = k_raw * e_k
    b_t = b_raw * e_k
    k_d = k_raw * e_t
    b_d = b_raw * e_t
    v_all = v_ref[...]

    ys = []
    for h in range(hp):
        sl = slice(64 * h, 64 * (h + 1))
        rows = slice(64 * h, 64 * (h + 1))
        s0 = s_ref[rows, :]
        v = v_all[:, sl]
        left = jnp.concatenate([kk_t[:, sl], r_t[:, sl]], axis=0)
        right = jnp.concatenate([k_t[:, sl], b_t[:, sl]], axis=0)
        a = _mm1(left, right, _mm_nt)
        a_kk_k = jnp.where(strict, a[0:cl, 0:cl], 0.0)
        a_kk_b = jnp.where(strict, a[0:cl, cl:2 * cl], 0.0)
        a_r_k = jnp.where(incl, a[cl:2 * cl, 0:cl], 0.0)
        a_r_b = jnp.where(incl, a[cl:2 * cl, cl:2 * cl], 0.0)
        tinv = _unit_tri_inverse(a_kk_b, eye, blk)
        av = _mm1(jnp.concatenate([a_kk_k, a_r_k], axis=0), v)
        ls = _mm3(left, s0, _mm_nt)
        u = _mm3(tinv, ls[0:cl] + av[0:cl])
        ys.append(ls[cl:2 * cl] + av[cl:2 * cl] - _mm1(a_r_b, u))
        vu = jnp.concatenate([v, -u], axis=0)
        kb = jnp.concatenate([k_d[:, sl], b_d[:, sl]], axis=0)
        s_new = s0 * d_c[:, sl] + _mm3(vu, kb, _mm_tn)
        s_ref[rows, :] = s_new
    y_ref[...] = jnp.concatenate(ys, axis=1)

    @pl.when(c == nc - 1)
    def _():
        sout_ref[...] = s_ref[...]


def _wkv_scan(r, v, kk, lw, k, b, s0, bsz, t_len, *, reverse):
    hp = WKV_HEADS_PER_STEP
    nc = t_len // WKV_CHUNK
    width = hp * RWKV_HEAD

    def row_idx(bi, c):
        return bi * nc + ((nc - 1 - c) if reverse else c)

    rows = pl.BlockSpec((WKV_CHUNK, width), lambda bi, g, c: (row_idx(bi, c), g))
    state = pl.BlockSpec((None, width, RWKV_HEAD), lambda bi, g, c: (bi, g, 0))
    return pl.pallas_call(
        functools.partial(_wkv_kernel, reverse=reverse, nc=nc, hp=hp),
        out_shape=(jax.ShapeDtypeStruct((bsz * t_len, 1024), F32),
                   jax.ShapeDtypeStruct((bsz, RWKV_HEADS * RWKV_HEAD, RWKV_HEAD), F32)),
        grid=(bsz, RWKV_HEADS // hp, nc),
        in_specs=[rows] * 6 + [state],
        out_specs=(rows, state),
        scratch_shapes=[pltpu.VMEM((width, RWKV_HEAD), F32)],
        compiler_params=_params("parallel", "parallel", "arbitrary"),
        name="wkv7_chunked_" + ("bwd" if reverse else "fwd"),
    )(r, v, kk, lw, k, b, s0)


def _rwkv_out_kernel(yf_ref, yb_ref, bonus_ref, g_ref, lng_ref, lnb_ref, bd_ref, o_ref):
    bd = bd_ref[...]
    y = yf_ref[...] + yb_ref[...]
    inv_n = np.float32(1.0 / RWKV_HEAD)
    yc = y - _seg_sum(y, bd) * inv_n
    var = _seg_sum(yc * yc, bd) * inv_n
    yn = yc * lax.rsqrt(var + RWKV_GN_EPS) * lng_ref[...] + lnb_ref[...]
    o_ref[...] = ((yn + bonus_ref[...]) * g_ref[...]).astype(BF16)


def _rwkv_out(y_f, y_b, bonus, g, ln_g, ln_b, bd, t_len, tm=256):
    m = y_f.shape[0]
    tm = min(tm, t_len)
    rows = pl.BlockSpec((tm, 1024), lambda i: (i, 0))
    return pl.pallas_call(
        _rwkv_out_kernel,
        out_shape=jax.ShapeDtypeStruct((m, 1024), BF16),
        grid=(m // tm,),
        in_specs=[rows] * 4 + [_full((1, 1024)), _full((1, 1024)), _full((128, 128))],
        out_specs=rows,
        compiler_params=_params("parallel"),
        name="rwkv_groupnorm_gate",
    )(y_f, y_b, bonus, g, ln_g.reshape(1, -1), ln_b.reshape(1, -1), bd)


def _permute_w_in(w):
    parts = []
    for seg in _PROJ_SEGMENTS:
        if isinstance(seg, tuple):
            parts.append(w[:, seg[0]:seg[1]])
        else:
            parts.append(jnp.zeros((w.shape[0], seg), w.dtype))
    return jnp.concatenate(parts, axis=1).astype(BF16)


def _permute_w_qb(w):
    w = w.reshape(w.shape[0], MLA_HEADS, MLA_NOPE + MLA_ROPE)
    w = jnp.pad(w, ((0, 0), (0, 0), (0, MLA_HEAD_PAD - MLA_NOPE - MLA_ROPE)))
    return w.reshape(w.shape[0], MLA_HEADS * MLA_HEAD_PAD).astype(BF16)


def _permute_w_kvb(w):
    w = w.reshape(w.shape[0], MLA_HEADS, MLA_NOPE + MLA_V)
    return jnp.concatenate([w[:, :, :MLA_NOPE].reshape(w.shape[0], -1),
                            w[:, :, MLA_NOPE:].reshape(w.shape[0], -1)], axis=1).astype(BF16)


def _rope_tables(pos_row, pos_col, half):
    inv = ROPE_BASE ** (-jnp.arange(half, dtype=F32) / half)
    zeros = jnp.zeros((pos_row.shape[0], half), F32)
    cos, sin_a, sin_b = [], [], []
    for pos in (pos_row, pos_col):
        ang = pos.astype(F32)[:, None] * inv[None, :]
        c, s = jnp.cos(ang), jnp.sin(ang)
        cos += [c, c]
        sin_a += [-s, zeros]
        sin_b += [zeros, s]
    pad = jnp.zeros((pos_row.shape[0], 128 - 4 * half), F32)
    return tuple(jnp.concatenate(t + [pad], axis=1) for t in (cos, sin_a, sin_b))


def _block_diag_ones():
    i = np.arange(128)
    return jnp.asarray((i[:, None] // RWKV_HEAD) == (i[None, :] // RWKV_HEAD), BF16)


def _split_w(w):
    hi = w.astype(BF16)
    return hi, (w - hi.astype(F32)).astype(BF16)


def _lru_gate_weights(wa, ba, wi, bi):
    w = jnp.concatenate([wa[0], wi[0], wa[1], wi[1]], axis=-1)
    b = jnp.concatenate([t.reshape(LRU_BLOCKS, 1, LRU_BLOCK_W) for t in (ba[0], bi[0], ba[1], bi[1])], axis=-1)
    return _split_w(w) + (b,)


def _rwkv_weights(mu_prev, mu_next, w0, w2, a0, a2, g2, k_k, k_a, r_k, bd):
    def pad_mu(mu):
        return jnp.pad(mu, (0, 3584 - mu.shape[0])).reshape(1, 3584)

    def two_dir(w):
        z = jnp.zeros_like(w[0])
        return jnp.concatenate([jnp.concatenate([w[0], z], axis=1), jnp.concatenate([z, w[1]], axis=1)], axis=0)

    g2p = jnp.pad(g2, ((0, 256 - RWKV_GATE_LORA), (0, 0)))
    return (pad_mu(mu_prev), pad_mu(mu_next), w0.reshape(1, 2048), a0.reshape(1, 2048),
            *_split_w(two_dir(w2)), *_split_w(two_dir(a2)), *_split_w(g2p),
            k_k.reshape(1, 1024), k_a.reshape(1, 1024), r_k.reshape(1, 1024), bd)


def kernel(x, c, ctx, c_ctx, w_mod, b_mod, w_in, w_out, ln1_g, ln1_b, w_ff1, w_ff2, ln2_g, ln2_b, swa_sink, lru_conv_w, lru_conv_b, lru_wa, lru_ba, lru_wi, lru_bi, lru_lam, mla_q_norm, mla_kv_norm, mla_w_qb, mla_w_kvb, rwkv_mu_prev, rwkv_mu_next, rwkv_w0, rwkv_w2, rwkv_a0, rwkv_a2, rwkv_g2, rwkv_k_k, rwkv_k_a, rwkv_r_k, rwkv_ln_g, rwkv_ln_b):
    bsz, t_len, d = x.shape
    n_ctx = ctx.shape[1]
    depth = w_mod.shape[0]

    cc = jnp.zeros((8, d), F32).at[:bsz].set(c).at[bsz].set(c_ctx)
    mod = _modulation(cc, w_mod, b_mod)

    pos = jnp.arange(t_len, dtype=jnp.int32)
    row, col = pos // GRID_W, pos % GRID_W
    swa_tables = _rope_tables(row, col, SWA_HEAD_DIM // 4)
    mla_tables = _rope_tables(row, col, MLA_ROPE // 4)
    bd = _block_diag_ones()
    zeros_h = jnp.zeros((bsz, 1, GROUP_W), F32)
    zeros_s = jnp.zeros((bsz, RWKV_HEADS * RWKV_HEAD, RWKV_HEAD), F32)

    xl = x.reshape(bsz * t_len, d)
    xc = ctx.reshape(bsz * n_ctx, d)

    for l in range(depth):
        with_ctx = l < depth - 1
        chunks = [mod[l, :, k * d:(k + 1) * d] for k in range(6)]
        lat = [m[:bsz][:, None, :] for m in chunks]
        cxm = [m[bsz:bsz + 1][:, None, :] for m in chunks]

        w_in_p = _permute_w_in(w_in[l])
        p = _ln_mod_matmul(xl, lat[0], lat[1], w_in_p, act=None, out_dtype=F32)
        pc = _ln_mod_matmul(xc, cxm[0], cxm[1], w_in_p, act=None, out_dtype=F32)

        q_a, k_a, v_a = _swa_prep(p, t_len, swa_tables)
        qc_a, kc_a, vc_a = _swa_prep(pc, n_ctx, None)
        o_swa = _swa_attn(q_a, k_a, v_a, kc_a, vc_a, swa_sink[l], bsz, t_len, n_ctx)

        gate_w = _lru_gate_weights(lru_wa[l], lru_ba[l], lru_wi[l], lru_bi[l])
        lru_args = (lru_conv_w[l], lru_conv_b[l].reshape(1, -1), *gate_w, lru_lam[l])
        caf, cbf, cab, cbb = _lru_coef(pc, n_ctx, *lru_args)
        hc_f, hlast_f = _lru_scan(caf, cbf, zeros_h, bsz, n_ctx, reverse=False)
        oc_lru, hlast_b = _lru_scan(cab, cbb, zeros_h, bsz, n_ctx, reverse=True, h_other=hc_f, p=pc)
        laf, lbf, lab, lbb = _lru_coef(p, t_len, *lru_args)
        h_f, _ = _lru_scan(laf, lbf, hlast_f, bsz, t_len, reverse=False)
        o_lru, _ = _lru_scan(lab, lbb, hlast_b, bsz, t_len, reverse=True, h_other=h_f, p=p)

        wq = _permute_w_qb(mla_w_qb[l])
        wkv = _permute_w_kvb(mla_w_kvb[l])
        q_c, k_c, v_c = _mla_proj(p, t_len, mla_q_norm[l], mla_kv_norm[l], wq, wkv, mla_tables)
        qc_c, kc_c, vc_c = _mla_proj(pc, n_ctx, mla_q_norm[l], mla_kv_norm[l], wq, wkv, None)
        o_mla = _mla_flash(q_c, kc_c, vc_c, bsz, t_len, n_ctx, k=k_c, v=v_c)

        rw = _rwkv_weights(rwkv_mu_prev[l], rwkv_mu_next[l], rwkv_w0[l], rwkv_w2[l], rwkv_a0[l], rwkv_a2[l],
                           rwkv_g2[l], rwkv_k_k[l], rwkv_k_a[l], rwkv_r_k[l], bd)
        fc = _rwkv_feat(pc, n_ctx, rw)
        yc_f, s_f = _wkv_scan(fc[0], fc[1], fc[2], fc[5], fc[6], fc[7], zeros_s, bsz, n_ctx, reverse=False)
        yc_b, s_b = _wkv_scan(fc[0], fc[1], fc[2], fc[8], fc[9], fc[10], zeros_s, bsz, n_ctx, reverse=True)
        fl = _rwkv_feat(p, t_len, rw)
        y_f, _ = _wkv_scan(fl[0], fl[1], fl[2], fl[5], fl[6], fl[7], s_f, bsz, t_len, reverse=False)
        y_b, _ = _wkv_scan(fl[0], fl[1], fl[2], fl[8], fl[9], fl[10], s_b, bsz, t_len, reverse=True)
        o_rwkv = _rwkv_out(y_f, y_b, fl[4], fl[3], rwkv_ln_g[l], rwkv_ln_b[l], bd, t_len)

        w_out_b = w_out[l].astype(BF16)
        w_ff1_b = w_ff1[l].astype(BF16)
        w_ff2_b = w_ff2[l].astype(BF16)

        def tail(xin, mix, m):
            x1 = _matmul_res_ln(mix, w_out_b, xin, m[2], ln1_g[l], ln1_b[l])
            hid = _ln_mod_matmul(x1, m[3], m[4], w_ff1_b, act="relu2", out_dtype=BF16)
            return _matmul_res_ln(hid, w_ff2_b, x1, m[5], ln2_g[l], ln2_b[l])

        if with_ctx:
            oc_swa = _swa_ctx_attn(qc_a, kc_a, vc_a, swa_sink[l], bsz, n_ctx)
            oc_mla = _mla_flash(qc_c, kc_c, vc_c, bsz, n_ctx, n_ctx)
            oc_rwkv = _rwkv_out(yc_f, yc_b, fc[4], fc[3], rwkv_ln_g[l], rwkv_ln_b[l], bd, n_ctx)
            xc = tail(xc, jnp.concatenate([oc_swa, oc_lru, oc_mla, oc_rwkv], axis=-1), cxm)

        xl = tail(xl, jnp.concatenate([o_swa, o_lru, o_mla, o_rwkv], axis=-1), lat)

    return xl.reshape(bsz, t_len, d)
```

```python
import functools

import numpy as np
import jax
import jax.numpy as jnp
from jax import lax
from jax.experimental import pallas as pl
from jax.experimental.pallas import tpu as pltpu

F32 = jnp.float32
BF16 = jnp.bfloat16

D_MODEL = 4096
GRID_W = 64
GROUP_W = 1024
D_FF = 4 * D_MODEL
BLOCK = 128

SWA_HEAD_DIM = 128
SWA_HEADS = 8
SWA_KV_HEADS = 2
SWA_GROUP = 4

LRU_BLOCKS = 8
LRU_BLOCK_W = 128
LRU_C = 8.0

MLA_HEADS = 8
MLA_NOPE = 128
MLA_ROPE = 64
MLA_V = 128
MLA_HEAD_PAD = 256

RWKV_HEAD = 64
RWKV_HEADS = 16
RWKV_GATE_LORA = 160
RWKV_GN_EPS = 64e-5
WKV_CHUNK = 64
WKV_INV_BLOCK = 16

ROPE_BASE = 10000.0
LN_EPS = 1e-5
RMS_EPS = 1e-6
DEPTH = 2
ALPHA = (2 * DEPTH) ** 0.25

OFF_AQ, OFF_BX, OFF_BG, OFF_CQ, OFF_DR, OFF_DK, OFF_DV = 0, 1024, 2048, 3072, 4096, 5120, 6144
OFF_CKV, OFF_AK, OFF_AV, OFF_DL, OFF_CR = 7168, 7680, 7936, 8192, 8704
N_PROJ = 9216
_PROJ_SEGMENTS = ((0, 1024), (1536, 2560), (2560, 3584), (3584, 4608), (5184, 6208), (6208, 7232),
                  (7232, 8256), (4608, 5120), (1024, 1280), (1280, 1536), (8256, 8672), 96,
                  (5120, 5184), 64, 384)

VMEM_LIMIT_V7X = 56 * 1024 * 1024


def _params(*sem):
    return pltpu.CompilerParams(dimension_semantics=sem, vmem_limit_bytes=VMEM_LIMIT_V7X)


def _split2(a):
    hi = a.astype(BF16)
    lo = (a - hi.astype(F32)).astype(BF16)
    return hi, lo


def _mm(a, b):
    return jnp.dot(a, b, preferred_element_type=F32)


def _mm_nt(a, b):
    return lax.dot_general(a, b, (((1,), (1,)), ((), ())), preferred_element_type=F32)


def _mm_tn(a, b):
    return lax.dot_general(a, b, (((0,), (0,)), ((), ())), preferred_element_type=F32)


def _mm3(a, b, mm=_mm):
    ah, al = _split2(a)
    bh, bl = _split2(b)
    return mm(ah, bh) + (mm(al, bh) + mm(ah, bl))


def _mm3w(a, bh, bl):
    ah, al = _split2(a)
    return _mm(ah, bh) + (_mm(al, bh) + _mm(ah, bl))


def _mm1(a, b, mm=_mm):
    return mm(a.astype(BF16), b.astype(BF16))


def _seg_sum(x, bd):
    parts = []
    for j in range(x.shape[1] // 128):
        hi, lo = _split2(x[:, 128 * j:128 * (j + 1)])
        parts.append(_mm(hi, bd) + _mm(lo, bd))
    return jnp.concatenate(parts, axis=1)


def _layer_norm_rows(x):
    mu = jnp.mean(x, -1, keepdims=True)
    xc = x - mu
    var = jnp.mean(xc * xc, -1, keepdims=True)
    return xc * lax.rsqrt(var + LN_EPS)


def _softplus(z):
    return jnp.maximum(z, 0.0) + jnp.log1p(jnp.exp(-jnp.abs(z)))


def _gelu_tanh(x):
    return 0.5 * x * (1.0 + jnp.tanh(np.sqrt(2.0 / np.pi).astype(np.float32) * (x + 0.044715 * (x * x * x))))


def _rope(x, cos, sin_a, sin_b, half):
    return x * cos + pltpu.roll(x, 128 - half, 1) * sin_a + pltpu.roll(x, half, 1) * sin_b


def _shift_rows(x, row, tm, prev_rows, next_row):
    p6, p7 = prev_rows
    xm1 = jnp.where(row == 0, p7, pltpu.roll(x, 1, 0))
    xp1 = jnp.where(row == tm - 1, next_row, pltpu.roll(x, tm - 1, 0))
    xm2 = None
    if p6 is not None:
        xm2 = jnp.where(row == 0, p6, jnp.where(row == 1, p7, pltpu.roll(x, 2, 0)))
    return xm2, xm1, xp1


def _halo_specs(col, width, tm, m_rows):
    r8 = tm // 8
    last8 = m_rows // 8 - 1
    return [pl.BlockSpec((tm, width), lambda i: (i, col)),
            pl.BlockSpec((8, width), lambda i: (jnp.maximum(i * r8 - 1, 0), col)),
            pl.BlockSpec((8, width), lambda i: (jnp.minimum((i + 1) * r8, last8), col))]


def _full(shape):
    nd = len(shape)
    return pl.BlockSpec(shape, lambda *_: (0,) * nd)


def _mod_kernel(c_ref, w_ref, b_ref, o_ref):
    cc = c_ref[...]
    a = cc * jax.nn.sigmoid(cc)
    o_ref[...] = _mm3(a, w_ref[...]) + b_ref[...]


def _modulation(cc, w_mod, b_mod, tn=512):
    depth, d, n = w_mod.shape
    return pl.pallas_call(
        _mod_kernel,
        out_shape=jax.ShapeDtypeStruct((depth, 8, n), F32),
        grid=(depth, n // tn),
        in_specs=[pl.BlockSpec((8, d), lambda l, j: (0, 0)),
                  pl.BlockSpec((None, d, tn), lambda l, j: (l, 0, j)),
                  pl.BlockSpec((None, 1, tn), lambda l, j: (l, 0, j))],
        out_specs=pl.BlockSpec((None, 8, tn), lambda l, j: (l, 0, j)),
        compiler_params=_params("parallel", "parallel"),
        name="adaln_modulation",
    )(cc, w_mod, b_mod.reshape(depth, 1, n))


def _lnmm_kernel(x_ref, sh_ref, sc_ref, w_ref, o_ref, xn_ref, *, act, tm):
    @pl.when(pl.program_id(1) == 0)
    def _():
        scale = 1.0 + sc_ref[...]
        shift = sh_ref[...]
        rows = min(tm, 128)

        def body(r, carry):
            sl = pl.ds(pl.multiple_of(r * rows, rows), rows)
            xn_ref[sl, :] = (_layer_norm_rows(x_ref[sl, :]) * scale + shift).astype(BF16)
            return carry

        lax.fori_loop(0, tm // rows, body, 0)

    acc = _mm(xn_ref[...], w_ref[...])
    if act == "relu2":
        acc = jnp.maximum(acc, 0.0)
        acc = acc * acc
    o_ref[...] = acc.astype(o_ref.dtype)


def _ln_mod_matmul(x, shift, scale, w, *, act, out_dtype, tm=512, tn=1024):
    m, k = x.shape
    n = w.shape[1]
    nbm = shift.shape[0]
    tm = min(tm, m // nbm)
    seq_tiles = m // nbm // tm
    mod_spec = pl.BlockSpec((None, 1, k), lambda i, j: (i // seq_tiles, 0, 0))
    return pl.pallas_call(
        functools.partial(_lnmm_kernel, act=act, tm=tm),
        out_shape=jax.ShapeDtypeStruct((m, n), out_dtype),
        grid=(m // tm, n // tn),
        in_specs=[pl.BlockSpec((tm, k), lambda i, j: (i, 0)), mod_spec, mod_spec,
                  pl.BlockSpec((k, tn), lambda i, j: (0, j))],
        out_specs=pl.BlockSpec((tm, tn), lambda i, j: (i, j)),
        scratch_shapes=[pltpu.VMEM((tm, k), BF16)],
        compiler_params=_params("parallel", "arbitrary"),
        name="ln_mod_matmul_" + (act or "id"),
    )(x, shift, scale, w)


def _mmln_kernel(a_ref, w_ref, x_ref, ga_ref, g_ref, b_ref, o_ref, *, nk, tm):
    k = pl.program_id(1)

    @pl.when(k == 0)
    def _():
        o_ref[...] = jnp.zeros_like(o_ref)

    o_ref[...] += _mm(a_ref[...], w_ref[...])

    @pl.when(k == nk - 1)
    def _():
        gate = ga_ref[...]
        g = g_ref[...]
        b = b_ref[...]
        rows = min(tm, 128)

        def body(r, carry):
            sl = pl.ds(pl.multiple_of(r * rows, rows), rows)
            z = ALPHA * x_ref[sl, :] + gate * o_ref[sl, :]
            o_ref[sl, :] = _layer_norm_rows(z) * g + b
            return carry

        lax.fori_loop(0, tm // rows, body, 0)


def _matmul_res_ln(a, w, xres, gate, g, b, *, tm=512, tk=512):
    m, kdim = a.shape
    n = w.shape[1]
    nbm = gate.shape[0]
    tm = min(tm, m // nbm)
    seq_tiles = m // nbm // tm
    nk = kdim // tk
    return pl.pallas_call(
        functools.partial(_mmln_kernel, nk=nk, tm=tm),
        out_shape=jax.ShapeDtypeStruct((m, n), F32),
        grid=(m // tm, nk),
        in_specs=[pl.BlockSpec((tm, tk), lambda i, k: (i, k)),
                  pl.BlockSpec((tk, n), lambda i, k: (k, 0)),
                  pl.BlockSpec((tm, n), lambda i, k: (i, 0)),
                  pl.BlockSpec((None, 1, n), lambda i, k: (i // seq_tiles, 0, 0)),
                  _full((1, n)), _full((1, n))],
        out_specs=pl.BlockSpec((tm, n), lambda i, k: (i, 0)),
        compiler_params=_params("parallel", "arbitrary"),
        name="matmul_res_ln",
    )(a, w, xres, gate, g.reshape(1, n), b.reshape(1, n))


def _swa_prep_kernel(*refs, rotate):
    if rotate:
        q_ref, kv_ref, cos_ref, sa_ref, sb_ref, qo_ref, ko_ref, vo_ref = refs
        cos, sa, sb = cos_ref[...], sa_ref[...], sb_ref[...]
    else:
        q_ref, kv_ref, qo_ref, ko_ref, vo_ref = refs
    scale = np.float32(SWA_HEAD_DIM ** -0.5)
    for h in range(SWA_HEADS):
        sl = slice(128 * h, 128 * (h + 1))
        q = q_ref[:, sl]
        if rotate:
            q = _rope(q, cos, sa, sb, 32)
        qo_ref[:, sl] = (q * scale).astype(BF16)
    for h in range(SWA_KV_HEADS):
        sl = slice(128 * h, 128 * (h + 1))
        k = kv_ref[:, sl]
        if rotate:
            k = _rope(k, cos, sa, sb, 32)
        ko_ref[:, sl] = k.astype(BF16)
    vo_ref[...] = kv_ref[:, 256:512].astype(BF16)


def _swa_prep(p, t_len, tables, tm=256):
    m = p.shape[0]
    tm = min(tm, t_len)
    seq_tiles = t_len // tm
    rotate = tables is not None
    in_specs = [pl.BlockSpec((tm, 1024), lambda i: (i, OFF_AQ // 1024)),
                pl.BlockSpec((tm, 512), lambda i: (i, OFF_AK // 512))]
    args = [p, p]
    if rotate:
        in_specs += [pl.BlockSpec((tm, 128), lambda i: (i % seq_tiles, 0))] * 3
        args += list(tables)
    return pl.pallas_call(
        functools.partial(_swa_prep_kernel, rotate=rotate),
        out_shape=(jax.ShapeDtypeStruct((m, 1024), BF16), jax.ShapeDtypeStruct((m, 256), BF16),
                   jax.ShapeDtypeStruct((m, 256), BF16)),
        grid=(m // tm,),
        in_specs=in_specs,
        out_specs=(pl.BlockSpec((tm, 1024), lambda i: (i, 0)), pl.BlockSpec((tm, 256), lambda i: (i, 0)),
                   pl.BlockSpec((tm, 256), lambda i: (i, 0))),
        compiler_params=_params("parallel"),
        name="swa_prep_rope" if rotate else "swa_prep",
    )(*args)


def _sink_softmax_pv(s, sink, v):
    m = jnp.maximum(jnp.max(s, -1, keepdims=True), sink)
    p = jnp.exp(s - m)
    den = jnp.sum(p, -1, keepdims=True) + jnp.exp(sink - m)
    return _mm(p.astype(BF16), v) / den


def _swa_kernel(q_ref, kp_ref, ko_ref, kn_ref, vp_ref, vo_ref, vn_ref, kc_ref, vc_ref, sink_ref, o_ref, *, nb, n_ctx):
    n = pl.program_id(1)
    qi = lax.broadcasted_iota(jnp.int32, (BLOCK, BLOCK), 0)
    kj = lax.broadcasted_iota(jnp.int32, (BLOCK, BLOCK), 1)
    neg = np.float32(-np.inf)
    bias_prev = jnp.where(jnp.logical_and(kj >= qi, n > 0), 0.0, neg)
    bias_next = jnp.where(jnp.logical_and(kj <= qi, n < nb - 1), 0.0, neg)
    bias = jnp.concatenate([bias_prev, jnp.zeros((BLOCK, BLOCK), F32), bias_next,
                            jnp.zeros((BLOCK, n_ctx), F32)], axis=1)
    for hh in range(SWA_KV_HEADS):
        sl = slice(128 * hh, 128 * (hh + 1))
        k_all = jnp.concatenate([kp_ref[:, sl], ko_ref[:, sl], kn_ref[:, sl], kc_ref[:, sl]], axis=0)
        v_all = jnp.concatenate([vp_ref[:, sl], vo_ref[:, sl], vn_ref[:, sl], vc_ref[:, sl]], axis=0)
        for g in range(SWA_GROUP):
            h = hh * SWA_GROUP + g
            hs = slice(128 * h, 128 * (h + 1))
            s = _mm_nt(q_ref[:, hs], k_all) + bias
            o_ref[:, hs] = _sink_softmax_pv(s, sink_ref[0:1, h:h + 1], v_all).astype(BF16)


def _swa_attn(q, k, v, kc, vc, sink, bsz, t_len, n_ctx):
    nb = t_len // BLOCK

    def blk(width, off):
        def idx(b, n):
            return (b * nb + jnp.clip(n + off, 0, nb - 1), 0)
        return pl.BlockSpec((BLOCK, width), idx)

    ctx_spec = pl.BlockSpec((n_ctx, 256), lambda b, n: (b, 0))
    return pl.pallas_call(
        functools.partial(_swa_kernel, nb=nb, n_ctx=n_ctx),
        out_shape=jax.ShapeDtypeStruct((bsz * t_len, 1024), BF16),
        grid=(bsz, nb),
        in_specs=[blk(1024, 0), blk(256, -1), blk(256, 0), blk(256, 1), blk(256, -1), blk(256, 0), blk(256, 1),
                  ctx_spec, ctx_spec, _full((1, SWA_HEADS))],
        out_specs=blk(1024, 0),
        compiler_params=_params("parallel", "parallel"),
        name="swa_attention",
    )(q, k, k, k, v, v, v, kc, vc, sink.reshape(1, SWA_HEADS))


def _swa_ctx_kernel(q_ref, kc_ref, vc_ref, sink_ref, o_ref):
    for hh in range(SWA_KV_HEADS):
        sl = slice(128 * hh, 128 * (hh + 1))
        for g in range(SWA_GROUP):
            h = hh * SWA_GROUP + g
            hs = slice(128 * h, 128 * (h + 1))
            s = _mm_nt(q_ref[:, hs], kc_ref[:, sl])
            o_ref[:, hs] = _sink_softmax_pv(s, sink_ref[0:1, h:h + 1], vc_ref[:, sl]).astype(BF16)


def _swa_ctx_attn(qc, kc, vc, sink, bsz, n_ctx):
    return pl.pallas_call(
        _swa_ctx_kernel,
        out_shape=jax.ShapeDtypeStruct((bsz * n_ctx, 1024), BF16),
        grid=(bsz,),
        in_specs=[pl.BlockSpec((n_ctx, 1024), lambda b: (b, 0)), pl.BlockSpec((n_ctx, 256), lambda b: (b, 0)),
                  pl.BlockSpec((n_ctx, 256), lambda b: (b, 0)), _full((1, SWA_HEADS))],
        out_specs=pl.BlockSpec((n_ctx, 1024), lambda b: (b, 0)),
        compiler_params=_params("parallel"),
        name="swa_ctx_attention",
    )(qc, kc, vc, sink.reshape(1, SWA_HEADS))


def _lru_coef_kernel(x_ref, xp_ref, xn_ref, cw_ref, cb_ref, wh_ref, wl_ref, bg_ref, lam_ref,
                     af_ref, bf_ref, ab_ref, bb_ref, *, tm, seq_tiles):
    i = pl.program_id(0) % seq_tiles
    first = i == 0
    last = i == seq_tiles - 1
    x = x_ref[...]
    row = lax.broadcasted_iota(jnp.int32, x.shape, 0)
    p6 = jnp.where(first, 0.0, xp_ref[6:7, :])
    p7 = jnp.where(first, 0.0, xp_ref[7:8, :])
    n0 = jnp.where(last, 0.0, xn_ref[0:1, :])
    xm2, xm1, xp1 = _shift_rows(x, row, tm, (p6, p7), n0)
    u = (cw_ref[0:1, :] * xm2 + cw_ref[1:2, :] * xm1 + cw_ref[2:3, :] * x + cw_ref[3:4, :] * xp1) + cb_ref[...]
    sp = _softplus(-lam_ref[...])
    outs = ((af_ref, bf_ref), (ab_ref, bb_ref))
    for n in range(LRU_BLOCKS):
        sl = slice(128 * n, 128 * (n + 1))
        un = u[:, sl]
        z = _mm3w(un, wh_ref[n], wl_ref[n]) + bg_ref[n]
        for d in range(2):
            r = jax.nn.sigmoid(z[:, 256 * d:256 * d + 128])
            gi = jax.nn.sigmoid(z[:, 256 * d + 128:256 * d + 256])
            log_a = (-LRU_C) * r * sp[d:d + 1, sl]
            a_ref, b_ref = outs[d]
            a_ref[:, sl] = jnp.exp(log_a)
            b_ref[:, sl] = jnp.sqrt(1.0 - jnp.exp(2.0 * log_a)) * (gi * un)


def _lru_coef(p, t_len, conv_w, conv_b, wg_hi, wg_lo, bg, lam, tm=256):
    m = p.shape[0]
    tm = min(tm, t_len)
    seq_tiles = t_len // tm
    out = jax.ShapeDtypeStruct((m, 1024), F32)
    ospec = pl.BlockSpec((tm, 1024), lambda i: (i, 0))
    return pl.pallas_call(
        functools.partial(_lru_coef_kernel, tm=tm, seq_tiles=seq_tiles),
        out_shape=(out, out, out, out),
        grid=(m // tm,),
        in_specs=_halo_specs(OFF_BX // 1024, 1024, tm, m) + [
            _full((4, 1024)), _full((1, 1024)), _full((8, 128, 512)), _full((8, 128, 512)),
            _full((8, 1, 512)), _full((2, 1024))],
        out_specs=(ospec, ospec, ospec, ospec),
        compiler_params=_params("parallel"),
        name="lru_coefficients",
    )(p, p, p, conv_w, conv_b, wg_hi, wg_lo, bg, lam)


def _lru_scan_kernel(*refs, reverse, final, nc, tc):
    if final:
        a_ref, b_ref, h0_ref, hf_ref, gate_ref, o_ref, hl_ref, hs_ref, hrows_ref = refs
    else:
        a_ref, b_ref, h0_ref, o_ref, hl_ref, hs_ref = refs
        hrows_ref = o_ref
    c = pl.program_id(1)

    @pl.when(c == 0)
    def _():
        hs_ref[...] = h0_ref[...]

    def body(t, h):
        tt = (tc - 1 - t) if reverse else t
        h = a_ref[pl.ds(tt, 1), :] * h + b_ref[pl.ds(tt, 1), :]
        hrows_ref[pl.ds(tt, 1), :] = h
        return h

    h = lax.fori_loop(0, tc, body, hs_ref[...], unroll=8)
    hs_ref[...] = h

    @pl.when(c == (0 if reverse else nc - 1))
    def _():
        hl_ref[...] = hrows_ref[tc - 1:tc, :]

    if final:
        o_ref[...] = ((hf_ref[...] + hrows_ref[...]) * _gelu_tanh(gate_ref[...])).astype(BF16)


def _lru_scan(a, b, h0, bsz, t_len, *, reverse, h_other=None, p=None, tc=256):
    tc = min(tc, t_len)
    nc = t_len // tc
    final = h_other is not None

    def row_idx(bi, c):
        return bi * nc + ((nc - 1 - c) if reverse else c)

    rows = pl.BlockSpec((tc, 1024), lambda bi, c: (row_idx(bi, c), 0))
    state = pl.BlockSpec((None, 1, 1024), lambda bi, c: (bi, 0, 0))
    in_specs = [rows, rows, state]
    args = [a, b, h0]
    scratch = [pltpu.VMEM((1, 1024), F32)]
    if final:
        in_specs += [rows, pl.BlockSpec((tc, 1024), lambda bi, c: (row_idx(bi, c), OFF_BG // 1024))]
        args += [h_other, p]
        scratch.append(pltpu.VMEM((tc, 1024), F32))
    return pl.pallas_call(
        functools.partial(_lru_scan_kernel, reverse=reverse, final=final, nc=nc, tc=tc),
        out_shape=(jax.ShapeDtypeStruct((bsz * t_len, 1024), BF16 if final else F32),
                   jax.ShapeDtypeStruct((bsz, 1, 1024), F32)),
        grid=(bsz, nc),
        in_specs=in_specs,
        out_specs=(rows, state),
        scratch_shapes=scratch,
        compiler_params=_params("parallel", "arbitrary"),
        name="lru_scan_" + ("bwd_out" if final else ("bwd" if reverse else "fwd")),
    )(*args)


def _mla_proj_kernel(*refs, rotate):
    if rotate:
        (qa_ref, kva_ref, kr_ref, qn_ref, kvn_ref, wq_ref, wkv_ref, cos_ref, sa_ref, sb_ref,
         q_ref, k_ref, v_ref) = refs
        cos, sa, sb = cos_ref[...], sa_ref[...], sb_ref[...]
    else:
        qa_ref, kva_ref, kr_ref, qn_ref, kvn_ref, wq_ref, wkv_ref, q_ref, k_ref, v_ref = refs

    def rms(x, g):
        return (x * lax.rsqrt(jnp.mean(x * x, -1, keepdims=True) + RMS_EPS) * g).astype(BF16)

    scale = np.float32((MLA_NOPE + MLA_ROPE) ** -0.5 * np.log2(np.e))
    q = _mm(rms(qa_ref[...], qn_ref[...]), wq_ref[...]) * scale
    kv = _mm(rms(kva_ref[...], kvn_ref[...]), wkv_ref[...])
    kr = kr_ref[...]
    if rotate:
        kr = _rope(kr, cos, sa, sb, 16)
    kr = kr.astype(BF16)
    lane = lax.broadcasted_iota(jnp.int32, kr.shape, 1)
    ones_col = jnp.where(lane == 0, 1.0, 0.0).astype(BF16)
    for h in range(MLA_HEADS):
        lo = MLA_HEAD_PAD * h
        q_ref[:, lo:lo + 128] = q[:, lo:lo + 128].astype(BF16)
        qr = q[:, lo + 128:lo + 256]
        if rotate:
            qr = _rope(qr, cos, sa, sb, 16)
        q_ref[:, lo + 128:lo + 256] = qr.astype(BF16)
        k_ref[:, lo:lo + 128] = kv[:, 128 * h:128 * (h + 1)].astype(BF16)
        k_ref[:, lo + 128:lo + 256] = kr
        v_ref[:, lo:lo + 128] = kv[:, 1024 + 128 * h:1024 + 128 * (h + 1)].astype(BF16)
        v_ref[:, lo + 128:lo + 256] = ones_col


def _mla_proj(p, t_len, q_norm, kv_norm, wq, wkv, tables, tm=256):
    m = p.shape[0]
    tm = min(tm, t_len)
    seq_tiles = t_len // tm
    rotate = tables is not None
    in_specs = [pl.BlockSpec((tm, 1024), lambda i: (i, OFF_CQ // 1024)),
                pl.BlockSpec((tm, 512), lambda i: (i, OFF_CKV // 512)),
                pl.BlockSpec((tm, 128), lambda i: (i, OFF_CR // 128)),
                _full((1, 1024)), _full((1, 512)), _full((1024, 2048)), _full((512, 2048))]
    args = [p, p, p, q_norm.reshape(1, -1), kv_norm.reshape(1, -1), wq, wkv]
    if rotate:
        in_specs += [pl.BlockSpec((tm, 128), lambda i: (i % seq_tiles, 0))] * 3
        args += list(tables)
    wide = pl.BlockSpec((tm, 2048), lambda i: (i, 0))
    return pl.pallas_call(
        functools.partial(_mla_proj_kernel, rotate=rotate),
        out_shape=(jax.ShapeDtypeStruct((m, 2048), BF16),) * 3,
        grid=(m // tm,),
        in_specs=in_specs,
        out_specs=(wide, wide, wide),
        compiler_params=_params("parallel"),
        name="mla_project_rope" if rotate else "mla_project",
    )(*args)


def _mla_flash_kernel(*refs, with_lat, n_chunks, ck):
    if with_lat:
        q_ref, kc_ref, vc_ref, k_ref, v_ref, o_ref = refs
    else:
        q_ref, kc_ref, vc_ref, o_ref = refs
    q = q_ref[...]
    s = _mm_nt(q, kc_ref[...])
    m = jnp.max(s, -1, keepdims=True)
    acc = _mm(jnp.exp2((s - m).astype(BF16)), vc_ref[...])
    if with_lat:
        def body(j, carry):
            m, acc = carry
            sl = pl.ds(pl.multiple_of(j * ck, ck), ck)
            s = _mm_nt(q, k_ref[sl, :])
            m_new = jnp.maximum(m, jnp.max(s, -1, keepdims=True))
            p = jnp.exp2((s - m_new).astype(BF16))
            acc = jnp.exp2(m - m_new) * acc + _mm(p, v_ref[sl, :])
            return m_new, acc

        m, acc = lax.fori_loop(0, n_chunks, body, (m, acc))
    o_ref[...] = (acc[:, 0:MLA_V] / acc[:, MLA_V:MLA_V + 1]).astype(BF16)


def _mla_flash(q, kc, vc, bsz, tq_len, n_ctx, k=None, v=None, tq=512, ck=512):
    with_lat = k is not None
    tq = min(tq, tq_len)
    nq = tq_len // tq
    in_specs = [pl.BlockSpec((tq, MLA_HEAD_PAD), lambda b, h, i: (b * nq + i, h)),
                pl.BlockSpec((n_ctx, MLA_HEAD_PAD), lambda b, h, i: (b, h)),
                pl.BlockSpec((n_ctx, MLA_HEAD_PAD), lambda b, h, i: (b, h))]
    args = [q, kc, vc]
    n_chunks = 0
    if with_lat:
        t_len = k.shape[0] // bsz
        ck = min(ck, t_len)
        n_chunks = t_len // ck
        in_specs += [pl.BlockSpec((t_len, MLA_HEAD_PAD), lambda b, h, i: (b, h)),
                     pl.BlockSpec((t_len, MLA_HEAD_PAD), lambda b, h, i: (b, h))]
        args += [k, v]
    return pl.pallas_call(
        functools.partial(_mla_flash_kernel, with_lat=with_lat, n_chunks=n_chunks, ck=ck),
        out_shape=jax.ShapeDtypeStruct((bsz * tq_len, 1024), BF16),
        grid=(bsz, MLA_HEADS, nq),
        in_specs=in_specs,
        out_specs=pl.BlockSpec((tq, MLA_V), lambda b, h, i: (b * nq + i, h)),
        compiler_params=_params("parallel", "parallel", "parallel"),
        name="mla_flash" if with_lat else "mla_ctx_attention",
    )(*args)


def _rwkv_feat_kernel(xr_ref, xrp_ref, xrn_ref, xk_ref, xkp_ref, xkn_ref, xv_ref, xvp_ref, xvn_ref,
                      xl_ref, xlp_ref, xln_ref, mup_ref, mun_ref, w0_ref, a0_ref,
                      w2h_ref, w2l_ref, a2h_ref, a2l_ref, g2h_ref, g2l_ref, kk_w_ref, ka_ref, rk_ref, bd_ref,
                      r_o, v_o, kk_o, g_o, bonus_o, lwf_o, kf_o, bf_o, lwb_o, kb_o, bb_o, *, tm, seq_tiles):
    i = pl.program_id(0) % seq_tiles
    first = i == 0
    last = i == seq_tiles - 1

    def shifted(x_ref, xp_ref, xn_ref, lo, hi):
        x = x_ref[...]
        row = lax.broadcasted_iota(jnp.int32, x.shape, 0)
        p7 = jnp.where(first, 0.0, xp_ref[7:8, :])
        n0 = jnp.where(last, 0.0, xn_ref[0:1, :])
        _, xm1, xp1 = _shift_rows(x, row, tm, (None, p7), n0)
        return x + mup_ref[:, lo:hi] * (xm1 - x) + mun_ref[:, lo:hi] * (xp1 - x)

    r = shifted(xr_ref, xrp_ref, xrn_ref, 0, 1024)
    k = shifted(xk_ref, xkp_ref, xkn_ref, 1024, 2048)
    v = shifted(xv_ref, xvp_ref, xvn_ref, 2048, 3072)
    lo = shifted(xl_ref, xlp_ref, xln_ref, 3072, 3584)
    bd = bd_ref[...]

    kkr = k * kk_w_ref[...]
    kk = kkr / jnp.maximum(jnp.sqrt(_seg_sum(kkr * kkr, bd)), 1e-12)
    g = _mm3w(jax.nn.sigmoid(lo[:, 256:512]), g2h_ref[...], g2l_ref[...])
    wl = w0_ref[...] + _mm3w(jnp.tanh(lo[:, 0:128]), w2h_ref[...], w2l_ref[...])
    lw = -jnp.exp(-_softplus(-wl) - 0.5)
    a = jax.nn.sigmoid(a0_ref[...] + _mm3w(lo[:, 128:256], a2h_ref[...], a2l_ref[...]))
    r_o[...] = r
    v_o[...] = v
    kk_o[...] = kk
    g_o[...] = g
    bonus = None
    for d, (lw_o, k_o, b_o) in enumerate(((lwf_o, kf_o, bf_o), (lwb_o, kb_o, bb_o))):
        a_d = a[:, 1024 * d:1024 * (d + 1)]
        k_d = k * (1.0 + (a_d - 1.0) * ka_ref[...])
        bo = _seg_sum(r * k_d * rk_ref[...], bd) * v
        bonus = bo if bonus is None else bonus + bo
        lw_o[...] = lw[:, 1024 * d:1024 * (d + 1)]
        k_o[...] = k_d
        b_o[...] = kk * a_d
    bonus_o[...] = bonus


def _rwkv_feat(p, t_len, wts, tm=256):
    m = p.shape[0]
    tm = min(tm, t_len)
    seq_tiles = t_len // tm
    in_specs = (_halo_specs(OFF_DR // 1024, 1024, tm, m) + _halo_specs(OFF_DK // 1024, 1024, tm, m)
                + _halo_specs(OFF_DV // 1024, 1024, tm, m) + _halo_specs(OFF_DL // 512, 512, tm, m)
                + [_full(w.shape) for w in wts])
    out = jax.ShapeDtypeStruct((m, 1024), F32)
    ospec = pl.BlockSpec((tm, 1024), lambda i: (i, 0))
    return pl.pallas_call(
        functools.partial(_rwkv_feat_kernel, tm=tm, seq_tiles=seq_tiles),
        out_shape=(out,) * 11,
        grid=(m // tm,),
        in_specs=in_specs,
        out_specs=(ospec,) * 11,
        compiler_params=_params("parallel"),
        name="rwkv_features",
    )(*([p] * 12), *wts)


def _wkv_kernel(r_ref, v_ref, kk_ref, lw_ref, k_ref, b_ref, s0_ref, y_ref, sout_ref, s_ref, *, reverse, nc):
    c = pl.program_id(1)
    cl = WKV_CHUNK

    @pl.when(c == 0)
    def _():
        s_ref[...] = s0_ref[...]

    ti = lax.broadcasted_iota(jnp.int32, (cl, cl), 0)
    si = lax.broadcasted_iota(jnp.int32, (cl, cl), 1)
    if reverse:
        incl, strict = si >= ti, si > ti
    else:
        incl, strict = si <= ti, si < ti
    blk = (ti // WKV_INV_BLOCK) == (si // WKV_INV_BLOCK)
    tri = jnp.where(incl, 1.0, 0.0).astype(BF16)

    lw = lw_ref[...]
    l1 = lw.astype(BF16)
    rem = lw - l1.astype(F32)
    l2 = rem.astype(BF16)
    l3 = (rem - l2.astype(F32)).astype(BF16)
    cum = _mm(tri, l1) + (_mm(tri, l2) + _mm(tri, l3))
    total = cum[0:1, :] if reverse else cum[cl - 1:cl, :]
    e_k = jnp.exp(-cum)
    e_t = jnp.exp(total - cum)
    d_c = jnp.exp(total)
    r_t = (r_ref[...] * jnp.exp(cum)).astype(BF16)
    kk_t = (kk_ref[...] * jnp.exp(cum - lw)).astype(BF16)
    k_raw = k_ref[...]
    b_raw = b_ref[...]
    k_t = (k_raw * e_k).astype(BF16)
    b_t = (b_raw * e_k).astype(BF16)
    k_d = (k_raw * e_t).astype(BF16)
    b_d = (b_raw * e_t).astype(BF16)
    v_all = v_ref[...]
    v_bf = v_all.astype(BF16)

    heads = range(RWKV_HEADS)
    sls = [slice(RWKV_HEAD * h, RWKV_HEAD * (h + 1)) for h in heads]
    s_all = s_ref[...]
    s0 = [s_all[sl, :] for sl in sls]
    left = [jnp.concatenate([kk_t[:, sl], r_t[:, sl]], axis=0) for sl in sls]
    right = [jnp.concatenate([k_t[:, sl], b_t[:, sl]], axis=0) for sl in sls]
    a = [_mm_nt(left[h], right[h]) for h in heads]
    ls = [_mm_nt(left[h], s0[h].astype(BF16)) for h in heads]
    a_l = [jnp.where(strict, a[h][0:cl, cl:2 * cl], 0.0) for h in heads]
    a_v = [jnp.concatenate([jnp.where(strict, a[h][0:cl, 0:cl], 0.0),
                            jnp.where(incl, a[h][cl:2 * cl, 0:cl], 0.0)], axis=0).astype(BF16) for h in heads]
    a_rb = [jnp.where(incl, a[h][cl:2 * cl, cl:2 * cl], 0.0).astype(BF16) for h in heads]
    av = [_mm(a_v[h], v_bf[:, sls[h]]) for h in heads]
    dg = [jnp.where(blk, a_l[h], 0.0) for h in heads]
    off = [a_l[h] - dg[h] for h in heads]
    d2 = [_mm1(dg[h], dg[h]) for h in heads]
    d4 = [_mm1(d2[h], d2[h]) for h in heads]
    x = [d2[h] - dg[h] - _mm1(dg[h], d2[h]) for h in heads]
    d8 = [_mm1(d4[h], d4[h]) for h in heads]
    x = [x[h] + d4[h] + _mm1(x[h], d4[h]) for h in heads]
    t16 = [x[h] + d8[h] + _mm1(x[h], d8[h]) for h in heads]
    n1 = [off[h] + _mm1(t16[h], off[h]) for h in heads]
    n2 = [_mm1(n1[h], n1[h]) for h in heads]
    y1 = [t16[h] - n1[h] - _mm1(n1[h], t16[h]) for h in heads]
    t_m = [y1[h] + n2[h] + _mm1(n2[h], y1[h]) for h in heads]
    rhs = [ls[h][0:cl] + av[h][0:cl] for h in heads]
    u = [rhs[h] + _mm1(t_m[h], rhs[h]) for h in heads]
    vu = [jnp.concatenate([v_bf[:, sls[h]], (-u[h]).astype(BF16)], axis=0) for h in heads]
    kb = [jnp.concatenate([k_d[:, sl], b_d[:, sl]], axis=0) for sl in sls]
    states = [s0[h] * d_c[:, sls[h]] + _mm_tn(vu[h], kb[h]) for h in heads]
    ys = [ls[h][cl:2 * cl] + av[h][cl:2 * cl] - _mm(a_rb[h], u[h].astype(BF16)) for h in heads]
    y_ref[...] = jnp.concatenate(ys, axis=1)
    s_ref[...] = jnp.concatenate(states, axis=0)

    @pl.when(c == nc - 1)
    def _():
        sout_ref[...] = s_ref[...]


def _wkv_scan(r, v, kk, lw, k, b, s0, bsz, t_len, *, reverse):
    nc = t_len // WKV_CHUNK

    def row_idx(bi, c):
        return bi * nc + ((nc - 1 - c) if reverse else c)

    rows = pl.BlockSpec((WKV_CHUNK, GROUP_W), lambda bi, c: (row_idx(bi, c), 0))
    state = pl.BlockSpec((None, GROUP_W, RWKV_HEAD), lambda bi, c: (bi, 0, 0))
    return pl.pallas_call(
        functools.partial(_wkv_kernel, reverse=reverse, nc=nc),
        out_shape=(jax.ShapeDtypeStruct((bsz * t_len, GROUP_W), F32),
                   jax.ShapeDtypeStruct((bsz, GROUP_W, RWKV_HEAD), F32)),
        grid=(bsz, nc),
        in_specs=[rows] * 6 + [state],
        out_specs=(rows, state),
        scratch_shapes=[pltpu.VMEM((GROUP_W, RWKV_HEAD), F32)],
        compiler_params=_params("parallel", "arbitrary"),
        name="wkv7_chunked_" + ("bwd" if reverse else "fwd"),
    )(r, v, kk, lw, k, b, s0)


def _rwkv_out_kernel(yf_ref, yb_ref, bonus_ref, g_ref, lng_ref, lnb_ref, bd_ref, o_ref):
    bd = bd_ref[...]
    y = yf_ref[...] + yb_ref[...]
    inv_n = np.float32(1.0 / RWKV_HEAD)
    yc = y - _seg_sum(y, bd) * inv_n
    var = _seg_sum(yc * yc, bd) * inv_n
    yn = yc * lax.rsqrt(var + RWKV_GN_EPS) * lng_ref[...] + lnb_ref[...]
    o_ref[...] = ((yn + bonus_ref[...]) * g_ref[...]).astype(BF16)


def _rwkv_out(y_f, y_b, bonus, g, ln_g, ln_b, bd, t_len, tm=256):
    m = y_f.shape[0]
    tm = min(tm, t_len)
    rows = pl.BlockSpec((tm, 1024), lambda i: (i, 0))
    return pl.pallas_call(
        _rwkv_out_kernel,
        out_shape=jax.ShapeDtypeStruct((m, 1024), BF16),
        grid=(m // tm,),
        in_specs=[rows] * 4 + [_full((1, 1024)), _full((1, 1024)), _full((128, 128))],
        out_specs=rows,
        compiler_params=_params("parallel"),
        name="rwkv_groupnorm_gate",
    )(y_f, y_b, bonus, g, ln_g.reshape(1, -1), ln_b.reshape(1, -1), bd)


def _permute_w_in(w):
    parts = []
    for seg in _PROJ_SEGMENTS:
        if isinstance(seg, tuple):
            parts.append(w[:, seg[0]:seg[1]])
        else:
            parts.append(jnp.zeros((w.shape[0], seg), w.dtype))
    return jnp.concatenate(parts, axis=1).astype(BF16)


def _permute_w_qb(w):
    w = w.reshape(w.shape[0], MLA_HEADS, MLA_NOPE + MLA_ROPE)
    w = jnp.pad(w, ((0, 0), (0, 0), (0, MLA_HEAD_PAD - MLA_NOPE - MLA_ROPE)))
    return w.reshape(w.shape[0], MLA_HEADS * MLA_HEAD_PAD).astype(BF16)


def _permute_w_kvb(w):
    w = w.reshape(w.shape[0], MLA_HEADS, MLA_NOPE + MLA_V)
    return jnp.concatenate([w[:, :, :MLA_NOPE].reshape(w.shape[0], -1),
                            w[:, :, MLA_NOPE:].reshape(w.shape[0], -1)], axis=1).astype(BF16)


def _rope_tables(pos_row, pos_col, half):
    inv = ROPE_BASE ** (-jnp.arange(half, dtype=F32) / half)
    zeros = jnp.zeros((pos_row.shape[0], half), F32)
    cos, sin_a, sin_b = [], [], []
    for pos in (pos_row, pos_col):
        ang = pos.astype(F32)[:, None] * inv[None, :]
        c, s = jnp.cos(ang), jnp.sin(ang)
        cos += [c, c]
        sin_a += [-s, zeros]
        sin_b += [zeros, s]
    pad = jnp.zeros((pos_row.shape[0], 128 - 4 * half), F32)
    return tuple(jnp.concatenate(t + [pad], axis=1) for t in (cos, sin_a, sin_b))


def _block_diag_ones():
    i = np.arange(128)
    return jnp.asarray((i[:, None] // RWKV_HEAD) == (i[None, :] // RWKV_HEAD), BF16)


def _split_w(w):
    hi = w.astype(BF16)
    return hi, (w - hi.astype(F32)).astype(BF16)


def _lru_gate_weights(wa, ba, wi, bi):
    w = jnp.concatenate([wa[0], wi[0], wa[1], wi[1]], axis=-1)
    b = jnp.concatenate([t.reshape(LRU_BLOCKS, 1, LRU_BLOCK_W) for t in (ba[0], bi[0], ba[1], bi[1])], axis=-1)
    return _split_w(w) + (b,)


def _rwkv_weights(mu_prev, mu_next, w0, w2, a0, a2, g2, k_k, k_a, r_k, bd):
    def pad_mu(mu):
        return jnp.pad(mu, (0, 3584 - mu.shape[0])).reshape(1, 3584)

    def two_dir(w):
        z = jnp.zeros_like(w[0])
        return jnp.concatenate([jnp.concatenate([w[0], z], axis=1), jnp.concatenate([z, w[1]], axis=1)], axis=0)

    g2p = jnp.pad(g2, ((0, 256 - RWKV_GATE_LORA), (0, 0)))
    return (pad_mu(mu_prev), pad_mu(mu_next), w0.reshape(1, 2048), a0.reshape(1, 2048),
            *_split_w(two_dir(w2)), *_split_w(two_dir(a2)), *_split_w(g2p),
            k_k.reshape(1, 1024), k_a.reshape(1, 1024), r_k.reshape(1, 1024), bd)


def kernel(x, c, ctx, c_ctx, w_mod, b_mod, w_in, w_out, ln1_g, ln1_b, w_ff1, w_ff2, ln2_g, ln2_b, swa_sink, lru_conv_w, lru_conv_b, lru_wa, lru_ba, lru_wi, lru_bi, lru_lam, mla_q_norm, mla_kv_norm, mla_w_qb, mla_w_kvb, rwkv_mu_prev, rwkv_mu_next, rwkv_w0, rwkv_w2, rwkv_a0, rwkv_a2, rwkv_g2, rwkv_k_k, rwkv_k_a, rwkv_r_k, rwkv_ln_g, rwkv_ln_b):
    bsz, t_len, d = x.shape
    n_ctx = ctx.shape[1]
    depth = w_mod.shape[0]

    cc = jnp.zeros((8, d), F32).at[:bsz].set(c).at[bsz].set(c_ctx)
    mod = _modulation(cc, w_mod, b_mod)

    pos = jnp.arange(t_len, dtype=jnp.int32)
    row, col = pos // GRID_W, pos % GRID_W
    swa_tables = _rope_tables(row, col, SWA_HEAD_DIM // 4)
    mla_tables = _rope_tables(row, col, MLA_ROPE // 4)
    bd = _block_diag_ones()
    zeros_h = jnp.zeros((bsz, 1, GROUP_W), F32)
    zeros_s = jnp.zeros((bsz, RWKV_HEADS * RWKV_HEAD, RWKV_HEAD), F32)

    xl = x.reshape(bsz * t_len, d)
    xc = ctx.reshape(bsz * n_ctx, d)

    for l in range(depth):
        with_ctx = l < depth - 1
        chunks = [mod[l, :, k * d:(k + 1) * d] for k in range(6)]
        lat = [m[:bsz][:, None, :] for m in chunks]
        cxm = [m[bsz:bsz + 1][:, None, :] for m in chunks]

        w_in_p = _permute_w_in(w_in[l])
        p = _ln_mod_matmul(xl, lat[0], lat[1], w_in_p, act=None, out_dtype=F32)
        pc = _ln_mod_matmul(xc, cxm[0], cxm[1], w_in_p, act=None, out_dtype=F32)

        q_a, k_a, v_a = _swa_prep(p, t_len, swa_tables)
        qc_a, kc_a, vc_a = _swa_prep(pc, n_ctx, None)
        o_swa = _swa_attn(q_a, k_a, v_a, kc_a, vc_a, swa_sink[l], bsz, t_len, n_ctx)

        gate_w = _lru_gate_weights(lru_wa[l], lru_ba[l], lru_wi[l], lru_bi[l])
        lru_args = (lru_conv_w[l], lru_conv_b[l].reshape(1, -1), *gate_w, lru_lam[l])
        caf, cbf, cab, cbb = _lru_coef(pc, n_ctx, *lru_args)
        hc_f, hlast_f = _lru_scan(caf, cbf, zeros_h, bsz, n_ctx, reverse=False)
        oc_lru, hlast_b = _lru_scan(cab, cbb, zeros_h, bsz, n_ctx, reverse=True, h_other=hc_f, p=pc)
        laf, lbf, lab, lbb = _lru_coef(p, t_len, *lru_args)
        h_f, _ = _lru_scan(laf, lbf, hlast_f, bsz, t_len, reverse=False)
        o_lru, _ = _lru_scan(lab, lbb, hlast_b, bsz, t_len, reverse=True, h_other=h_f, p=p)

        wq = _permute_w_qb(mla_w_qb[l])
        wkv = _permute_w_kvb(mla_w_kvb[l])
        q_c, k_c, v_c = _mla_proj(p, t_len, mla_q_norm[l], mla_kv_norm[l], wq, wkv, mla_tables)
        qc_c, kc_c, vc_c = _mla_proj(pc, n_ctx, mla_q_norm[l], mla_kv_norm[l], wq, wkv, None)
        o_mla = _mla_flash(q_c, kc_c, vc_c, bsz, t_len, n_ctx, k=k_c, v=v_c)

        rw = _rwkv_weights(rwkv_mu_prev[l], rwkv_mu_next[l], rwkv_w0[l], rwkv_w2[l], rwkv_a0[l], rwkv_a2[l],
                           rwkv_g2[l], rwkv_k_k[l], rwkv_k_a[l], rwkv_r_k[l], bd)
        fc = _rwkv_feat(pc, n_ctx, rw)
        yc_f, s_f = _wkv_scan(fc[0], fc[1], fc[2], fc[5], fc[6], fc[7], zeros_s, bsz, n_ctx, reverse=False)
        yc_b, s_b = _wkv_scan(fc[0], fc[1], fc[2], fc[8], fc[9], fc[10], zeros_s, bsz, n_ctx, reverse=True)
        fl = _rwkv_feat(p, t_len, rw)
        y_f, _ = _wkv_scan(fl[0], fl[1], fl[2], fl[5], fl[6], fl[7], s_f, bsz, t_len, reverse=False)
        y_b, _ = _wkv_scan(fl[0], fl[1], fl[2], fl[8], fl[9], fl[10], s_b, bsz, t_len, reverse=True)
        o_rwkv = _rwkv_out(y_f, y_b, fl[4], fl[3], rwkv_ln_g[l], rwkv_ln_b[l], bd, t_len)

        w_out_b = w_out[l].astype(BF16)
        w_ff1_b = w_ff1[l].astype(BF16)
        w_ff2_b = w_ff2[l].astype(BF16)

        def tail(xin, mix, m):
            x1 = _matmul_res_ln(mix, w_out_b, xin, m[2], ln1_g[l], ln1_b[l])
            hid = _ln_mod_matmul(x1, m[3], m[4], w_ff1_b, act="relu2", out_dtype=BF16)
            return _matmul_res_ln(hid, w_ff2_b, x1, m[5], ln2_g[l], ln2_b[l])

        if with_ctx:
            oc_swa = _swa_ctx_attn(qc_a, kc_a, vc_a, swa_sink[l], bsz, n_ctx)
            oc_mla = _mla_flash(qc_c, kc_c, vc_c, bsz, n_ctx, n_ctx)
            oc_rwkv = _rwkv_out(yc_f, yc_b, fc[4], fc[3], rwkv_ln_g[l], rwkv_ln_b[l], bd, n_ctx)
            xc = tail(xc, jnp.concatenate([oc_swa, oc_lru, oc_mla, oc_rwkv], axis=-1), cxm)

        xl = tail(xl, jnp.concatenate([o_swa, o_lru, o_mla, o_rwkv], axis=-1), lat)

    return xl.reshape(bsz, t_len, d)
```

```python
import functools

import numpy as np
import jax
import jax.numpy as jnp
from jax import lax
from jax.experimental import pallas as pl
from jax.experimental.pallas import tpu as pltpu

F32 = jnp.float32
BF16 = jnp.bfloat16

D_MODEL = 4096
GRID_W = 64
GROUP_W = 1024
D_FF = 4 * D_MODEL
BLOCK = 128

SWA_HEAD_DIM = 128
SWA_HEADS = 8
SWA_KV_HEADS = 2
SWA_GROUP = 4

LRU_BLOCKS = 8
LRU_BLOCK_W = 128
LRU_C = 8.0

MLA_HEADS = 8
MLA_NOPE = 128
MLA_ROPE = 64
MLA_V = 128
MLA_HEAD_PAD = 256

RWKV_HEAD = 64
RWKV_HEADS = 16
RWKV_GATE_LORA = 160
RWKV_GN_EPS = 64e-5
WKV_CHUNK = 64
WKV_INV_BLOCK = 16

ROPE_BASE = 10000.0
LN_EPS = 1e-5
RMS_EPS = 1e-6
DEPTH = 2
ALPHA = (2 * DEPTH) ** 0.25

OFF_AQ, OFF_BX, OFF_BG, OFF_CQ, OFF_DR, OFF_DK, OFF_DV = 0, 1024, 2048, 3072, 4096, 5120, 6144
OFF_CKV, OFF_AK, OFF_AV, OFF_DL, OFF_CR = 7168, 7680, 7936, 8192, 8704
N_PROJ = 9216
_PROJ_SEGMENTS = ((0, 1024), (1536, 2560), (2560, 3584), (3584, 4608), (5184, 6208), (6208, 7232),
                  (7232, 8256), (4608, 5120), (1024, 1280), (1280, 1536), (8256, 8672), 96,
                  (5120, 5184), 64, 384)

VMEM_LIMIT_V7X = 56 * 1024 * 1024


def _params(*sem):
    return pltpu.CompilerParams(dimension_semantics=sem, vmem_limit_bytes=VMEM_LIMIT_V7X)


def _split2(a):
    hi = a.astype(BF16)
    lo = (a - hi.astype(F32)).astype(BF16)
    return hi, lo


def _mm(a, b):
    return jnp.dot(a, b, preferred_element_type=F32)


def _mm_nt(a, b):
    return lax.dot_general(a, b, (((1,), (1,)), ((), ())), preferred_element_type=F32)


def _mm_tn(a, b):
    return lax.dot_general(a, b, (((0,), (0,)), ((), ())), preferred_element_type=F32)


def _mm3(a, b, mm=_mm):
    ah, al = _split2(a)
    bh, bl = _split2(b)
    return mm(ah, bh) + (mm(al, bh) + mm(ah, bl))


def _mm3w(a, bh, bl):
    ah, al = _split2(a)
    return _mm(ah, bh) + (_mm(al, bh) + _mm(ah, bl))


def _mm1(a, b, mm=_mm):
    return mm(a.astype(BF16), b.astype(BF16))


def _seg_sum(x, bd):
    parts = []
    for j in range(x.shape[1] // 128):
        hi, lo = _split2(x[:, 128 * j:128 * (j + 1)])
        parts.append(_mm(hi, bd) + _mm(lo, bd))
    return jnp.concatenate(parts, axis=1)


def _layer_norm_rows(x):
    mu = jnp.mean(x, -1, keepdims=True)
    xc = x - mu
    var = jnp.mean(xc * xc, -1, keepdims=True)
    return xc * lax.rsqrt(var + LN_EPS)


def _softplus(z):
    return jnp.maximum(z, 0.0) + jnp.log1p(jnp.exp(-jnp.abs(z)))


def _gelu_tanh(x):
    return 0.5 * x * (1.0 + jnp.tanh(np.sqrt(2.0 / np.pi).astype(np.float32) * (x + 0.044715 * (x * x * x))))


def _rope(x, cos, sin_a, sin_b, half):
    return x * cos + pltpu.roll(x, 128 - half, 1) * sin_a + pltpu.roll(x, half, 1) * sin_b


def _shift_rows(x, row, tm, prev_rows, next_row):
    p6, p7 = prev_rows
    xm1 = jnp.where(row == 0, p7, pltpu.roll(x, 1, 0))
    xp1 = jnp.where(row == tm - 1, next_row, pltpu.roll(x, tm - 1, 0))
    xm2 = None
    if p6 is not None:
        xm2 = jnp.where(row == 0, p6, jnp.where(row == 1, p7, pltpu.roll(x, 2, 0)))
    return xm2, xm1, xp1


def _halo_specs(col, width, tm, m_rows):
    r8 = tm // 8
    last8 = m_rows // 8 - 1
    return [pl.BlockSpec((tm, width), lambda i: (i, col)),
            pl.BlockSpec((8, width), lambda i: (jnp.maximum(i * r8 - 1, 0), col)),
            pl.BlockSpec((8, width), lambda i: (jnp.minimum((i + 1) * r8, last8), col))]


def _full(shape):
    nd = len(shape)
    return pl.BlockSpec(shape, lambda *_: (0,) * nd)


def _mod_kernel(c_ref, w_ref, b_ref, o_ref):
    cc = c_ref[...]
    a = cc * jax.nn.sigmoid(cc)
    o_ref[...] = _mm3(a, w_ref[...]) + b_ref[...]


def _modulation(cc, w_mod, b_mod, tn=512):
    depth, d, n = w_mod.shape
    return pl.pallas_call(
        _mod_kernel,
        out_shape=jax.ShapeDtypeStruct((depth, 8, n), F32),
        grid=(depth, n // tn),
        in_specs=[pl.BlockSpec((8, d), lambda l, j: (0, 0)),
                  pl.BlockSpec((None, d, tn), lambda l, j: (l, 0, j)),
                  pl.BlockSpec((None, 1, tn), lambda l, j: (l, 0, j))],
        out_specs=pl.BlockSpec((None, 8, tn), lambda l, j: (l, 0, j)),
        compiler_params=_params("parallel", "parallel"),
        name="adaln_modulation",
    )(cc, w_mod, b_mod.reshape(depth, 1, n))


def _lnmm_kernel(x_ref, sh_ref, sc_ref, w_ref, o_ref, xn_ref, *, act, tm):
    @pl.when(pl.program_id(1) == 0)
    def _():
        scale = 1.0 + sc_ref[...]
        shift = sh_ref[...]
        rows = min(tm, 128)

        def body(r, carry):
            sl = pl.ds(pl.multiple_of(r * rows, rows), rows)
            xn_ref[sl, :] = (_layer_norm_rows(x_ref[sl, :]) * scale + shift).astype(BF16)
            return carry

        lax.fori_loop(0, tm // rows, body, 0)

    acc = _mm(xn_ref[...], w_ref[...])
    if act == "relu2":
        acc = jnp.maximum(acc, 0.0)
        acc = acc * acc
    o_ref[...] = acc.astype(o_ref.dtype)


def _ln_mod_matmul(x, shift, scale, w, *, act, out_dtype, tm=512, tn=1024):
    m, k = x.shape
    n = w.shape[1]
    nbm = shift.shape[0]
    tm = min(tm, m // nbm)
    seq_tiles = m // nbm // tm
    mod_spec = pl.BlockSpec((None, 1, k), lambda i, j: (i // seq_tiles, 0, 0))
    x_spec = pl.BlockSpec((tm, k), lambda i, j: (i, 0), pipeline_mode=pl.Buffered(1))
    return pl.pallas_call(
        functools.partial(_lnmm_kernel, act=act, tm=tm),
        out_shape=jax.ShapeDtypeStruct((m, n), out_dtype),
        grid=(m // tm, n // tn),
        in_specs=[x_spec, mod_spec, mod_spec,
                  pl.BlockSpec((k, tn), lambda i, j: (0, j))],
        out_specs=pl.BlockSpec((tm, tn), lambda i, j: (i, j)),
        scratch_shapes=[pltpu.VMEM((tm, k), BF16)],
        compiler_params=_params("parallel", "arbitrary"),
        name="ln_mod_matmul_" + (act or "id"),
    )(x, shift, scale, w)


def _mmln_kernel(a_ref, w_ref, x_ref, ga_ref, g_ref, b_ref, o_ref, *, nk, tm):
    k = pl.program_id(1)

    @pl.when(k == 0)
    def _():
        o_ref[...] = _mm(a_ref[...], w_ref[...])

    @pl.when(k > 0)
    def _():
        o_ref[...] += _mm(a_ref[...], w_ref[...])

    @pl.when(k == nk - 1)
    def _():
        gate = ga_ref[...]
        g = g_ref[...]
        b = b_ref[...]
        rows = min(tm, 128)

        def body(r, carry):
            sl = pl.ds(pl.multiple_of(r * rows, rows), rows)
            z = ALPHA * x_ref[sl, :] + gate * o_ref[sl, :]
            o_ref[sl, :] = _layer_norm_rows(z) * g + b
            return carry

        lax.fori_loop(0, tm // rows, body, 0)


def _matmul_res_ln(a, w, xres, gate, g, b, *, tm=512, tk=1024):
    m, kdim = a.shape
    n = w.shape[1]
    nbm = gate.shape[0]
    tm = min(tm, m // nbm)
    seq_tiles = m // nbm // tm
    nk = kdim // tk
    return pl.pallas_call(
        functools.partial(_mmln_kernel, nk=nk, tm=tm),
        out_shape=jax.ShapeDtypeStruct((m, n), F32),
        grid=(m // tm, nk),
        in_specs=[pl.BlockSpec((tm, tk), lambda i, k: (i, k)),
                  pl.BlockSpec((tk, n), lambda i, k: (k, 0)),
                  pl.BlockSpec((tm, n), lambda i, k: (i, 0), pipeline_mode=pl.Buffered(1)),
                  pl.BlockSpec((None, 1, n), lambda i, k: (i // seq_tiles, 0, 0)),
                  _full((1, n)), _full((1, n))],
        out_specs=pl.BlockSpec((tm, n), lambda i, k: (i, 0)),
        compiler_params=_params("parallel", "arbitrary"),
        name="matmul_res_ln",
    )(a, w, xres, gate, g.reshape(1, n), b.reshape(1, n))


def _swa_prep_kernel(*refs, rotate):
    if rotate:
        q_ref, kv_ref, cos_ref, sa_ref, sb_ref, qo_ref, ko_ref, vo_ref = refs
        cos, sa, sb = cos_ref[...], sa_ref[...], sb_ref[...]
    else:
        q_ref, kv_ref, qo_ref, ko_ref, vo_ref = refs
    scale = np.float32(SWA_HEAD_DIM ** -0.5)
    for h in range(SWA_HEADS):
        sl = slice(128 * h, 128 * (h + 1))
        q = q_ref[:, sl]
        if rotate:
            q = _rope(q, cos, sa, sb, 32)
        qo_ref[:, sl] = (q * scale).astype(BF16)
    for h in range(SWA_KV_HEADS):
        sl = slice(128 * h, 128 * (h + 1))
        k = kv_ref[:, sl]
        if rotate:
            k = _rope(k, cos, sa, sb, 32)
        ko_ref[:, sl] = k.astype(BF16)
    vo_ref[...] = kv_ref[:, 256:512].astype(BF16)


def _swa_prep(p, t_len, tables, tm=256):
    m = p.shape[0]
    tm = min(tm, t_len)
    seq_tiles = t_len // tm
    rotate = tables is not None
    in_specs = [pl.BlockSpec((tm, 1024), lambda i: (i, OFF_AQ // 1024)),
                pl.BlockSpec((tm, 512), lambda i: (i, OFF_AK // 512))]
    args = [p, p]
    if rotate:
        in_specs += [pl.BlockSpec((tm, 128), lambda i: (i % seq_tiles, 0))] * 3
        args += list(tables)
    return pl.pallas_call(
        functools.partial(_swa_prep_kernel, rotate=rotate),
        out_shape=(jax.ShapeDtypeStruct((m, 1024), BF16), jax.ShapeDtypeStruct((m, 256), BF16),
                   jax.ShapeDtypeStruct((m, 256), BF16)),
        grid=(m // tm,),
        in_specs=in_specs,
        out_specs=(pl.BlockSpec((tm, 1024), lambda i: (i, 0)), pl.BlockSpec((tm, 256), lambda i: (i, 0)),
                   pl.BlockSpec((tm, 256), lambda i: (i, 0))),
        compiler_params=_params("parallel"),
        name="swa_prep_rope" if rotate else "swa_prep",
    )(*args)


def _sink_softmax_pv(s, sink, v):
    m = jnp.maximum(jnp.max(s, -1, keepdims=True), sink)
    p = jnp.exp(s - m)
    den = jnp.sum(p, -1, keepdims=True) + jnp.exp(sink - m)
    return _mm(p.astype(BF16), v) / den


def _swa_kernel(q_ref, kp_ref, ko_ref, kn_ref, vp_ref, vo_ref, vn_ref, kc_ref, vc_ref, sink_ref, o_ref, *, nb, n_ctx):
    n = pl.program_id(1)
    qi = lax.broadcasted_iota(jnp.int32, (BLOCK, BLOCK), 0)
    kj = lax.broadcasted_iota(jnp.int32, (BLOCK, BLOCK), 1)
    neg = np.float32(-np.inf)
    bias_prev = jnp.where(jnp.logical_and(kj >= qi, n > 0), 0.0, neg)
    bias_next = jnp.where(jnp.logical_and(kj <= qi, n < nb - 1), 0.0, neg)
    bias = jnp.concatenate([bias_prev, jnp.zeros((BLOCK, BLOCK), F32), bias_next,
                            jnp.zeros((BLOCK, n_ctx), F32)], axis=1)
    for hh in range(SWA_KV_HEADS):
        sl = slice(128 * hh, 128 * (hh + 1))
        k_all = jnp.concatenate([kp_ref[:, sl], ko_ref[:, sl], kn_ref[:, sl], kc_ref[:, sl]], axis=0)
        v_all = jnp.concatenate([vp_ref[:, sl], vo_ref[:, sl], vn_ref[:, sl], vc_ref[:, sl]], axis=0)
        for g in range(SWA_GROUP):
            h = hh * SWA_GROUP + g
            hs = slice(128 * h, 128 * (h + 1))
            s = _mm_nt(q_ref[:, hs], k_all) + bias
            o_ref[:, hs] = _sink_softmax_pv(s, sink_ref[0:1, h:h + 1], v_all).astype(BF16)


def _swa_attn(q, k, v, kc, vc, sink, bsz, t_len, n_ctx):
    nb = t_len // BLOCK

    def blk(width, off):
        def idx(b, n):
            return (b * nb + jnp.clip(n + off, 0, nb - 1), 0)
        return pl.BlockSpec((BLOCK, width), idx)

    ctx_spec = pl.BlockSpec((n_ctx, 256), lambda b, n: (b, 0))
    return pl.pallas_call(
        functools.partial(_swa_kernel, nb=nb, n_ctx=n_ctx),
        out_shape=jax.ShapeDtypeStruct((bsz * t_len, 1024), BF16),
        grid=(bsz, nb),
        in_specs=[blk(1024, 0), blk(256, -1), blk(256, 0), blk(256, 1), blk(256, -1), blk(256, 0), blk(256, 1),
                  ctx_spec, ctx_spec, _full((1, SWA_HEADS))],
        out_specs=blk(1024, 0),
        compiler_params=_params("parallel", "parallel"),
        name="swa_attention",
    )(q, k, k, k, v, v, v, kc, vc, sink.reshape(1, SWA_HEADS))


def _swa_ctx_kernel(q_ref, kc_ref, vc_ref, sink_ref, o_ref):
    for hh in range(SWA_KV_HEADS):
        sl = slice(128 * hh, 128 * (hh + 1))
        for g in range(SWA_GROUP):
            h = hh * SWA_GROUP + g
            hs = slice(128 * h, 128 * (h + 1))
            s = _mm_nt(q_ref[:, hs], kc_ref[:, sl])
            o_ref[:, hs] = _sink_softmax_pv(s, sink_ref[0:1, h:h + 1], vc_ref[:, sl]).astype(BF16)


def _swa_ctx_attn(qc, kc, vc, sink, bsz, n_ctx):
    return pl.pallas_call(
        _swa_ctx_kernel,
        out_shape=jax.ShapeDtypeStruct((bsz * n_ctx, 1024), BF16),
        grid=(bsz,),
        in_specs=[pl.BlockSpec((n_ctx, 1024), lambda b: (b, 0)), pl.BlockSpec((n_ctx, 256), lambda b: (b, 0)),
                  pl.BlockSpec((n_ctx, 256), lambda b: (b, 0)), _full((1, SWA_HEADS))],
        out_specs=pl.BlockSpec((n_ctx, 1024), lambda b: (b, 0)),
        compiler_params=_params("parallel"),
        name="swa_ctx_attention",
    )(qc, kc, vc, sink.reshape(1, SWA_HEADS))


def _lru_coef_kernel(x_ref, xp_ref, xn_ref, cw_ref, cb_ref, wh_ref, wl_ref, bg_ref, lam_ref,
                     af_ref, bf_ref, ab_ref, bb_ref, *, tm, seq_tiles):
    i = pl.program_id(0) % seq_tiles
    first = i == 0
    last = i == seq_tiles - 1
    x = x_ref[...]
    row = lax.broadcasted_iota(jnp.int32, x.shape, 0)
    p6 = jnp.where(first, 0.0, xp_ref[6:7, :])
    p7 = jnp.where(first, 0.0, xp_ref[7:8, :])
    n0 = jnp.where(last, 0.0, xn_ref[0:1, :])
    xm2, xm1, xp1 = _shift_rows(x, row, tm, (p6, p7), n0)
    u = (cw_ref[0:1, :] * xm2 + cw_ref[1:2, :] * xm1 + cw_ref[2:3, :] * x + cw_ref[3:4, :] * xp1) + cb_ref[...]
    sp = _softplus(-lam_ref[...])
    outs = ((af_ref, bf_ref), (ab_ref, bb_ref))
    for n in range(LRU_BLOCKS):
        sl = slice(128 * n, 128 * (n + 1))
        un = u[:, sl]
        z = _mm3w(un, wh_ref[n], wl_ref[n]) + bg_ref[n]
        for d in range(2):
            r = jax.nn.sigmoid(z[:, 256 * d:256 * d + 128])
            gi = jax.nn.sigmoid(z[:, 256 * d + 128:256 * d + 256])
            log_a = (-LRU_C) * r * sp[d:d + 1, sl]
            a_ref, b_ref = outs[d]
            a_ref[:, sl] = jnp.exp(log_a)
            b_ref[:, sl] = jnp.sqrt(1.0 - jnp.exp(2.0 * log_a)) * (gi * un)


def _lru_coef(p, t_len, conv_w, conv_b, wg_hi, wg_lo, bg, lam, tm=256):
    m = p.shape[0]
    tm = min(tm, t_len)
    seq_tiles = t_len // tm
    out = jax.ShapeDtypeStruct((m, 1024), F32)
    ospec = pl.BlockSpec((tm, 1024), lambda i: (i, 0))
    return pl.pallas_call(
        functools.partial(_lru_coef_kernel, tm=tm, seq_tiles=seq_tiles),
        out_shape=(out, out, out, out),
        grid=(m // tm,),
        in_specs=_halo_specs(OFF_BX // 1024, 1024, tm, m) + [
            _full((4, 1024)), _full((1, 1024)), _full((8, 128, 512)), _full((8, 128, 512)),
            _full((8, 1, 512)), _full((2, 1024))],
        out_specs=(ospec, ospec, ospec, ospec),
        compiler_params=_params("parallel"),
        name="lru_coefficients",
    )(p, p, p, conv_w, conv_b, wg_hi, wg_lo, bg, lam)


def _lru_scan_kernel(*refs, reverse, final, nc, tc):
    if final:
        a_ref, b_ref, h0_ref, hf_ref, gate_ref, o_ref, hl_ref, hs_ref, hrows_ref = refs
    else:
        a_ref, b_ref, h0_ref, o_ref, hl_ref, hs_ref = refs
        hrows_ref = o_ref
    c = pl.program_id(1)

    @pl.when(c == 0)
    def _():
        hs_ref[...] = h0_ref[...]

    def body(t, h):
        tt = (tc - 1 - t) if reverse else t
        h = a_ref[pl.ds(tt, 1), :] * h + b_ref[pl.ds(tt, 1), :]
        hrows_ref[pl.ds(tt, 1), :] = h
        return h

    h = lax.fori_loop(0, tc, body, hs_ref[...], unroll=8)
    hs_ref[...] = h

    @pl.when(c == (0 if reverse else nc - 1))
    def _():
        hl_ref[...] = hrows_ref[tc - 1:tc, :]

    if final:
        o_ref[...] = ((hf_ref[...] + hrows_ref[...]) * _gelu_tanh(gate_ref[...])).astype(BF16)


def _lru_scan(a, b, h0, bsz, t_len, *, reverse, h_other=None, p=None, tc=256):
    tc = min(tc, t_len)
    nc = t_len // tc
    final = h_other is not None

    def row_idx(bi, c):
        return bi * nc + ((nc - 1 - c) if reverse else c)

    rows = pl.BlockSpec((tc, 1024), lambda bi, c: (row_idx(bi, c), 0))
    state = pl.BlockSpec((None, 1, 1024), lambda bi, c: (bi, 0, 0))
    in_specs = [rows, rows, state]
    args = [a, b, h0]
    scratch = [pltpu.VMEM((1, 1024), F32)]
    if final:
        in_specs += [rows, pl.BlockSpec((tc, 1024), lambda bi, c: (row_idx(bi, c), OFF_BG // 1024))]
        args += [h_other, p]
        scratch.append(pltpu.VMEM((tc, 1024), F32))
    return pl.pallas_call(
        functools.partial(_lru_scan_kernel, reverse=reverse, final=final, nc=nc, tc=tc),
        out_shape=(jax.ShapeDtypeStruct((bsz * t_len, 1024), BF16 if final else F32),
                   jax.ShapeDtypeStruct((bsz, 1, 1024), F32)),
        grid=(bsz, nc),
        in_specs=in_specs,
        out_specs=(rows, state),
        scratch_shapes=scratch,
        compiler_params=_params("parallel", "arbitrary"),
        name="lru_scan_" + ("bwd_out" if final else ("bwd" if reverse else "fwd")),
    )(*args)


def _mla_proj_kernel(*refs, rotate):
    if rotate:
        (qa_ref, kva_ref, kr_ref, qn_ref, kvn_ref, wq_ref, wkv_ref, cos_ref, sa_ref, sb_ref,
         q_ref, k_ref, v_ref) = refs
        cos, sa, sb = cos_ref[...], sa_ref[...], sb_ref[...]
    else:
        qa_ref, kva_ref, kr_ref, qn_ref, kvn_ref, wq_ref, wkv_ref, q_ref, k_ref, v_ref = refs

    def rms(x, g):
        return (x * lax.rsqrt(jnp.mean(x * x, -1, keepdims=True) + RMS_EPS) * g).astype(BF16)

    scale = np.float32((MLA_NOPE + MLA_ROPE) ** -0.5 * np.log2(np.e))
    q = _mm(rms(qa_ref[...], qn_ref[...]), wq_ref[...]) * scale
    kv = _mm(rms(kva_ref[...], kvn_ref[...]), wkv_ref[...])
    kr = kr_ref[...]
    if rotate:
        kr = _rope(kr, cos, sa, sb, 16)
    kr = kr.astype(BF16)
    lane = lax.broadcasted_iota(jnp.int32, kr.shape, 1)
    ones_col = jnp.where(lane == 0, 1.0, 0.0).astype(BF16)
    for h in range(MLA_HEADS):
        lo = MLA_HEAD_PAD * h
        q_ref[:, lo:lo + 128] = q[:, lo:lo + 128].astype(BF16)
        qr = q[:, lo + 128:lo + 256]
        if rotate:
            qr = _rope(qr, cos, sa, sb, 16)
        q_ref[:, lo + 128:lo + 256] = qr.astype(BF16)
        k_ref[:, lo:lo + 128] = kv[:, 128 * h:128 * (h + 1)].astype(BF16)
        k_ref[:, lo + 128:lo + 256] = kr
        v_ref[:, lo:lo + 128] = kv[:, 1024 + 128 * h:1024 + 128 * (h + 1)].astype(BF16)
        v_ref[:, lo + 128:lo + 256] = ones_col


def _mla_proj(p, t_len, q_norm, kv_norm, wq, wkv, tables, tm=256):
    m = p.shape[0]
    tm = min(tm, t_len)
    seq_tiles = t_len // tm
    rotate = tables is not None
    in_specs = [pl.BlockSpec((tm, 1024), lambda i: (i, OFF_CQ // 1024)),
                pl.BlockSpec((tm, 512), lambda i: (i, OFF_CKV // 512)),
                pl.BlockSpec((tm, 128), lambda i: (i, OFF_CR // 128)),
                _full((1, 1024)), _full((1, 512)), _full((1024, 2048)), _full((512, 2048))]
    args = [p, p, p, q_norm.reshape(1, -1), kv_norm.reshape(1, -1), wq, wkv]
    if rotate:
        in_specs += [pl.BlockSpec((tm, 128), lambda i: (i % seq_tiles, 0))] * 3
        args += list(tables)
    wide = pl.BlockSpec((tm, 2048), lambda i: (i, 0))
    return pl.pallas_call(
        functools.partial(_mla_proj_kernel, rotate=rotate),
        out_shape=(jax.ShapeDtypeStruct((m, 2048), BF16),) * 3,
        grid=(m // tm,),
        in_specs=in_specs,
        out_specs=(wide, wide, wide),
        compiler_params=_params("parallel"),
        name="mla_project_rope" if rotate else "mla_project",
    )(*args)


def _mla_flash_kernel(*refs, with_lat, n_chunks, ck):
    if with_lat:
        q_ref, kc_ref, vc_ref, k_ref, v_ref, o_ref = refs
    else:
        q_ref, kc_ref, vc_ref, o_ref = refs
    q = q_ref[...]
    chunks = [(kc_ref, vc_ref, slice(None))]
    if with_lat:
        chunks += [(k_ref, v_ref, slice(j * ck, (j + 1) * ck)) for j in range(n_chunks)]
    k0, _, sl0 = chunks[0]
    s_cur = _mm_nt(q, k0[sl0, :])
    m = acc = pending = s_next = None
    for j, (_, vj_ref, slj) in enumerate(chunks):
        if j + 1 < len(chunks):
            kn_ref, _, sln = chunks[j + 1]
            s_next = _mm_nt(q, kn_ref[sln, :])
        if pending is not None:
            p_prev, v_prev, alpha_prev = pending
            pv = _mm(p_prev, v_prev)
            acc = pv if acc is None else alpha_prev * acc + pv
        mx = jnp.max(s_cur, -1, keepdims=True)
        m_new = mx if m is None else jnp.maximum(m, mx)
        alpha = None if m is None else jnp.exp2(m - m_new)
        pending = (jnp.exp2((s_cur - m_new).astype(BF16)), vj_ref[slj, :], alpha)
        m = m_new
        s_cur = s_next
    p_prev, v_prev, alpha_prev = pending
    pv = _mm(p_prev, v_prev)
    acc = pv if acc is None else alpha_prev * acc + pv
    o_ref[...] = (acc[:, 0:MLA_V] / acc[:, MLA_V:MLA_V + 1]).astype(BF16)


def _mla_flash(q, kc, vc, bsz, tq_len, n_ctx, k=None, v=None, tq=512, ck=512):
    with_lat = k is not None
    tq = min(tq, tq_len)
    nq = tq_len // tq
    in_specs = [pl.BlockSpec((tq, MLA_HEAD_PAD), lambda b, h, i: (b * nq + i, h)),
                pl.BlockSpec((n_ctx, MLA_HEAD_PAD), lambda b, h, i: (b, h)),
                pl.BlockSpec((n_ctx, MLA_HEAD_PAD), lambda b, h, i: (b, h))]
    args = [q, kc, vc]
    n_chunks = 0
    if with_lat:
        t_len = k.shape[0] // bsz
        ck = min(ck, t_len)
        n_chunks = t_len // ck
        in_specs += [pl.BlockSpec((t_len, MLA_HEAD_PAD), lambda b, h, i: (b, h)),
                     pl.BlockSpec((t_len, MLA_HEAD_PAD), lambda b, h, i: (b, h))]
        args += [k, v]
    return pl.pallas_call(
        functools.partial(_mla_flash_kernel, with_lat=with_lat, n_chunks=n_chunks, ck=ck),
        out_shape=jax.ShapeDtypeStruct((bsz * tq_len, 1024), BF16),
        grid=(bsz, MLA_HEADS, nq),
        in_specs=in_specs,
        out_specs=pl.BlockSpec((tq, MLA_V), lambda b, h, i: (b * nq + i, h)),
        compiler_params=_params("parallel", "parallel", "parallel"),
        name="mla_flash" if with_lat else "mla_ctx_attention",
    )(*args)


def _rwkv_feat_kernel(xr_ref, xrp_ref, xrn_ref, xk_ref, xkp_ref, xkn_ref, xv_ref, xvp_ref, xvn_ref,
                      xl_ref, xlp_ref, xln_ref, mup_ref, mun_ref, w0_ref, a0_ref,
                      w2h_ref, w2l_ref, a2h_ref, a2l_ref, g2h_ref, g2l_ref, kk_w_ref, ka_ref, rk_ref, bd_ref,
                      r_o, v_o, kk_o, g_o, bonus_o, lwf_o, kf_o, bf_o, lwb_o, kb_o, bb_o, *, tm, seq_tiles):
    i = pl.program_id(0) % seq_tiles
    first = i == 0
    last = i == seq_tiles - 1

    def shifted(x_ref, xp_ref, xn_ref, lo, hi):
        x = x_ref[...]
        row = lax.broadcasted_iota(jnp.int32, x.shape, 0)
        p7 = jnp.where(first, 0.0, xp_ref[7:8, :])
        n0 = jnp.where(last, 0.0, xn_ref[0:1, :])
        _, xm1, xp1 = _shift_rows(x, row, tm, (None, p7), n0)
        return x + mup_ref[:, lo:hi] * (xm1 - x) + mun_ref[:, lo:hi] * (xp1 - x)

    r = shifted(xr_ref, xrp_ref, xrn_ref, 0, 1024)
    k = shifted(xk_ref, xkp_ref, xkn_ref, 1024, 2048)
    v = shifted(xv_ref, xvp_ref, xvn_ref, 2048, 3072)
    lo = shifted(xl_ref, xlp_ref, xln_ref, 3072, 3584)
    bd = bd_ref[...]

    kkr = k * kk_w_ref[...]
    kk = kkr / jnp.maximum(jnp.sqrt(_seg_sum(kkr * kkr, bd)), 1e-12)
    g = _mm3w(jax.nn.sigmoid(lo[:, 256:512]), g2h_ref[...], g2l_ref[...])
    wl = w0_ref[...] + _mm3w(jnp.tanh(lo[:, 0:128]), w2h_ref[...], w2l_ref[...])
    lw = -jnp.exp(-_softplus(-wl) - 0.5)
    a = jax.nn.sigmoid(a0_ref[...] + _mm3w(lo[:, 128:256], a2h_ref[...], a2l_ref[...]))
    r_o[...] = r
    v_o[...] = v
    kk_o[...] = kk
    g_o[...] = g
    bonus = None
    for d, (lw_o, k_o, b_o) in enumerate(((lwf_o, kf_o, bf_o), (lwb_o, kb_o, bb_o))):
        a_d = a[:, 1024 * d:1024 * (d + 1)]
        k_d = k * (1.0 + (a_d - 1.0) * ka_ref[...])
        bo = _seg_sum(r * k_d * rk_ref[...], bd) * v
        bonus = bo if bonus is None else bonus + bo
        lw_o[...] = lw[:, 1024 * d:1024 * (d + 1)]
        k_o[...] = k_d
        b_o[...] = kk * a_d
    bonus_o[...] = bonus


def _rwkv_feat(p, t_len, wts, tm=256):
    m = p.shape[0]
    tm = min(tm, t_len)
    seq_tiles = t_len // tm
    in_specs = (_halo_specs(OFF_DR // 1024, 1024, tm, m) + _halo_specs(OFF_DK // 1024, 1024, tm, m)
                + _halo_specs(OFF_DV // 1024, 1024, tm, m) + _halo_specs(OFF_DL // 512, 512, tm, m)
                + [_full(w.shape) for w in wts])
    out = jax.ShapeDtypeStruct((m, 1024), F32)
    ospec = pl.BlockSpec((tm, 1024), lambda i: (i, 0))
    return pl.pallas_call(
        functools.partial(_rwkv_feat_kernel, tm=tm, seq_tiles=seq_tiles),
        out_shape=(out,) * 11,
        grid=(m // tm,),
        in_specs=in_specs,
        out_specs=(ospec,) * 11,
        compiler_params=_params("parallel"),
        name="rwkv_features",
    )(*([p] * 12), *wts)


def _wkv_kernel(r_ref, v_ref, kk_ref, lw_ref, k_ref, b_ref, s0_ref, y_ref, sout_ref, s_ref, *, reverse, nc):
    c = pl.program_id(1)
    cl = WKV_CHUNK

    @pl.when(c == 0)
    def _():
        s_ref[...] = s0_ref[...]

    ti = lax.broadcasted_iota(jnp.int32, (cl, cl), 0)
    si = lax.broadcasted_iota(jnp.int32, (cl, cl), 1)
    if reverse:
        incl, strict = si >= ti, si > ti
    else:
        incl, strict = si <= ti, si < ti
    blk = (ti // WKV_INV_BLOCK) == (si // WKV_INV_BLOCK)
    tri = jnp.where(incl, 1.0, 0.0).astype(BF16)

    lw = lw_ref[...]
    l1 = lw.astype(BF16)
    rem = lw - l1.astype(F32)
    l2 = rem.astype(BF16)
    l3 = (rem - l2.astype(F32)).astype(BF16)
    cum = _mm(tri, l1) + (_mm(tri, l2) + _mm(tri, l3))
    total = cum[0:1, :] if reverse else cum[cl - 1:cl, :]
    e_k = jnp.exp(-cum)
    e_t = jnp.exp(total - cum)
    d_c = jnp.exp(total)
    r_t = (r_ref[...] * jnp.exp(cum)).astype(BF16)
    kk_t = (kk_ref[...] * jnp.exp(cum - lw)).astype(BF16)
    k_raw = k_ref[...]
    b_raw = b_ref[...]
    k_t = (k_raw * e_k).astype(BF16)
    b_t = (b_raw * e_k).astype(BF16)
    k_d = (k_raw * e_t).astype(BF16)
    b_d = (b_raw * e_t).astype(BF16)
    v_all = v_ref[...]
    v_bf = v_all.astype(BF16)

    heads = range(RWKV_HEADS)
    sls = [slice(RWKV_HEAD * h, RWKV_HEAD * (h + 1)) for h in heads]
    s_all = s_ref[...]
    s0 = [s_all[sl, :] for sl in sls]
    left = [jnp.concatenate([kk_t[:, sl], r_t[:, sl]], axis=0) for sl in sls]
    right = [jnp.concatenate([k_t[:, sl], b_t[:, sl]], axis=0) for sl in sls]
    a = [_mm_nt(left[h], right[h]) for h in heads]
    ls = [_mm_nt(left[h], s0[h].astype(BF16)) for h in heads]
    a_l = [jnp.where(strict, a[h][0:cl, cl:2 * cl], 0.0) for h in heads]
    a_v = [jnp.concatenate([jnp.where(strict, a[h][0:cl, 0:cl], 0.0),
                            jnp.where(incl, a[h][cl:2 * cl, 0:cl], 0.0)], axis=0).astype(BF16) for h in heads]
    a_rb = [jnp.where(incl, a[h][cl:2 * cl, cl:2 * cl], 0.0).astype(BF16) for h in heads]
    av = [_mm(a_v[h], v_bf[:, sls[h]]) for h in heads]
    dg = [jnp.where(blk, a_l[h], 0.0) for h in heads]
    off = [a_l[h] - dg[h] for h in heads]
    d2 = [_mm1(dg[h], dg[h]) for h in heads]
    d4 = [_mm1(d2[h], d2[h]) for h in heads]
    x = [d2[h] - dg[h] - _mm1(dg[h], d2[h]) for h in heads]
    d8 = [_mm1(d4[h], d4[h]) for h in heads]
    x = [x[h] + d4[h] + _mm1(x[h], d4[h]) for h in heads]
    t16 = [x[h] + d8[h] + _mm1(x[h], d8[h]) for h in heads]
    n1 = [off[h] + _mm1(t16[h], off[h]) for h in heads]
    n2 = [_mm1(n1[h], n1[h]) for h in heads]
    y1 = [t16[h] - n1[h] - _mm1(n1[h], t16[h]) for h in heads]
    t_m = [y1[h] + n2[h] + _mm1(n2[h], y1[h]) for h in heads]
    rhs = [ls[h][0:cl] + av[h][0:cl] for h in heads]
    u = [rhs[h] + _mm1(t_m[h], rhs[h]) for h in heads]
    vu = [jnp.concatenate([v_bf[:, sls[h]], (-u[h]).astype(BF16)], axis=0) for h in heads]
    kb = [jnp.concatenate([k_d[:, sl], b_d[:, sl]], axis=0) for sl in sls]
    states = [s0[h] * d_c[:, sls[h]] + _mm_tn(vu[h], kb[h]) for h in heads]
    ys = [ls[h][cl:2 * cl] + av[h][cl:2 * cl] - _mm(a_rb[h], u[h].astype(BF16)) for h in heads]
    y_ref[...] = jnp.concatenate(ys, axis=1)
    s_ref[...] = jnp.concatenate(states, axis=0)

    @pl.when(c == nc - 1)
    def _():
        sout_ref[...] = s_ref[...]


def _wkv_scan(r, v, kk, lw, k, b, s0, bsz, t_len, *, reverse):
    nc = t_len // WKV_CHUNK

    def row_idx(bi, c):
        return bi * nc + ((nc - 1 - c) if reverse else c)

    rows = pl.BlockSpec((WKV_CHUNK, GROUP_W), lambda bi, c: (row_idx(bi, c), 0))
    state = pl.BlockSpec((None, GROUP_W, RWKV_HEAD), lambda bi, c: (bi, 0, 0))
    return pl.pallas_call(
        functools.partial(_wkv_kernel, reverse=reverse, nc=nc),
        out_shape=(jax.ShapeDtypeStruct((bsz * t_len, GROUP_W), F32),
                   jax.ShapeDtypeStruct((bsz, GROUP_W, RWKV_HEAD), F32)),
        grid=(bsz, nc),
        in_specs=[rows] * 6 + [state],
        out_specs=(rows, state),
        scratch_shapes=[pltpu.VMEM((GROUP_W, RWKV_HEAD), F32)],
        compiler_params=_params("parallel", "arbitrary"),
        name="wkv7_chunked_" + ("bwd" if reverse else "fwd"),
    )(r, v, kk, lw, k, b, s0)


def _rwkv_out_kernel(yf_ref, yb_ref, bonus_ref, g_ref, lng_ref, lnb_ref, bd_ref, o_ref):
    bd = bd_ref[...]
    y = yf_ref[...] + yb_ref[...]
    inv_n = np.float32(1.0 / RWKV_HEAD)
    yc = y - _seg_sum(y, bd) * inv_n
    var = _seg_sum(yc * yc, bd) * inv_n
    yn = yc * lax.rsqrt(var + RWKV_GN_EPS) * lng_ref[...] + lnb_ref[...]
    o_ref[...] = ((yn + bonus_ref[...]) * g_ref[...]).astype(BF16)


def _rwkv_out(y_f, y_b, bonus, g, ln_g, ln_b, bd, t_len, tm=256):
    m = y_f.shape[0]
    tm = min(tm, t_len)
    rows = pl.BlockSpec((tm, 1024), lambda i: (i, 0))
    return pl.pallas_call(
        _rwkv_out_kernel,
        out_shape=jax.ShapeDtypeStruct((m, 1024), BF16),
        grid=(m // tm,),
        in_specs=[rows] * 4 + [_full((1, 1024)), _full((1, 1024)), _full((128, 128))],
        out_specs=rows,
        compiler_params=_params("parallel"),
        name="rwkv_groupnorm_gate",
    )(y_f, y_b, bonus, g, ln_g.reshape(1, -1), ln_b.reshape(1, -1), bd)


def _permute_w_in(w):
    parts = []
    for seg in _PROJ_SEGMENTS:
        if isinstance(seg, tuple):
            parts.append(w[:, seg[0]:seg[1]])
        else:
            parts.append(jnp.zeros((w.shape[0], seg), w.dtype))
    return jnp.concatenate(parts, axis=1).astype(BF16)


def _permute_w_qb(w):
    w = w.reshape(w.shape[0], MLA_HEADS, MLA_NOPE + MLA_ROPE)
    w = jnp.pad(w, ((0, 0), (0, 0), (0, MLA_HEAD_PAD - MLA_NOPE - MLA_ROPE)))
    return w.reshape(w.shape[0], MLA_HEADS * MLA_HEAD_PAD).astype(BF16)


def _permute_w_kvb(w):
    w = w.reshape(w.shape[0], MLA_HEADS, MLA_NOPE + MLA_V)
    return jnp.concatenate([w[:, :, :MLA_NOPE].reshape(w.shape[0], -1),
                            w[:, :, MLA_NOPE:].reshape(w.shape[0], -1)], axis=1).astype(BF16)


def _rope_tables(pos_row, pos_col, half):
    inv = ROPE_BASE ** (-jnp.arange(half, dtype=F32) / half)
    zeros = jnp.zeros((pos_row.shape[0], half), F32)
    cos, sin_a, sin_b = [], [], []
    for pos in (pos_row, pos_col):
        ang = pos.astype(F32)[:, None] * inv[None, :]
        c, s = jnp.cos(ang), jnp.sin(ang)
        cos += [c, c]
        sin_a += [-s, zeros]
        sin_b += [zeros, s]
    pad = jnp.zeros((pos_row.shape[0], 128 - 4 * half), F32)
    return tuple(jnp.concatenate(t + [pad], axis=1) for t in (cos, sin_a, sin_b))


def _block_diag_ones():
    i = np.arange(128)
    return jnp.asarray((i[:, None] // RWKV_HEAD) == (i[None, :] // RWKV_HEAD), BF16)


def _split_w(w):
    hi = w.astype(BF16)
    return hi, (w - hi.astype(F32)).astype(BF16)


def _lru_gate_weights(wa, ba, wi, bi):
    w = jnp.concatenate([wa[0], wi[0], wa[1], wi[1]], axis=-1)
    b = jnp.concatenate([t.reshape(LRU_BLOCKS, 1, LRU_BLOCK_W) for t in (ba[0], bi[0], ba[1], bi[1])], axis=-1)
    return _split_w(w) + (b,)


def _rwkv_weights(mu_prev, mu_next, w0, w2, a0, a2, g2, k_k, k_a, r_k, bd):
    def pad_mu(mu):
        return jnp.pad(mu, (0, 3584 - mu.shape[0])).reshape(1, 3584)

    def two_dir(w):
        z = jnp.zeros_like(w[0])
        return jnp.concatenate([jnp.concatenate([w[0], z], axis=1), jnp.concatenate([z, w[1]], axis=1)], axis=0)

    g2p = jnp.pad(g2, ((0, 256 - RWKV_GATE_LORA), (0, 0)))
    return (pad_mu(mu_prev), pad_mu(mu_next), w0.reshape(1, 2048), a0.reshape(1, 2048),
            *_split_w(two_dir(w2)), *_split_w(two_dir(a2)), *_split_w(g2p),
            k_k.reshape(1, 1024), k_a.reshape(1, 1024), r_k.reshape(1, 1024), bd)


def kernel(x, c, ctx, c_ctx, w_mod, b_mod, w_in, w_out, ln1_g, ln1_b, w_ff1, w_ff2, ln2_g, ln2_b, swa_sink, lru_conv_w, lru_conv_b, lru_wa, lru_ba, lru_wi, lru_bi, lru_lam, mla_q_norm, mla_kv_norm, mla_w_qb, mla_w_kvb, rwkv_mu_prev, rwkv_mu_next, rwkv_w0, rwkv_w2, rwkv_a0, rwkv_a2, rwkv_g2, rwkv_k_k, rwkv_k_a, rwkv_r_k, rwkv_ln_g, rwkv_ln_b):
    bsz, t_len, d = x.shape
    n_ctx = ctx.shape[1]
    depth = w_mod.shape[0]

    cc = jnp.zeros((8, d), F32).at[:bsz].set(c).at[bsz].set(c_ctx)
    mod = _modulation(cc, w_mod, b_mod)

    pos = jnp.arange(t_len, dtype=jnp.int32)
    row, col = pos // GRID_W, pos % GRID_W
    swa_tables = _rope_tables(row, col, SWA_HEAD_DIM // 4)
    mla_tables = _rope_tables(row, col, MLA_ROPE // 4)
    bd = _block_diag_ones()
    zeros_h = jnp.zeros((bsz, 1, GROUP_W), F32)
    zeros_s = jnp.zeros((bsz, RWKV_HEADS * RWKV_HEAD, RWKV_HEAD), F32)

    xl = x.reshape(bsz * t_len, d)
    xc = ctx.reshape(bsz * n_ctx, d)

    for l in range(depth):
        with_ctx = l < depth - 1
        chunks = [mod[l, :, k * d:(k + 1) * d] for k in range(6)]
        lat = [m[:bsz][:, None, :] for m in chunks]
        cxm = [m[bsz:bsz + 1][:, None, :] for m in chunks]

        w_in_p = _permute_w_in(w_in[l])
        p = _ln_mod_matmul(xl, lat[0], lat[1], w_in_p, act=None, out_dtype=F32)
        pc = _ln_mod_matmul(xc, cxm[0], cxm[1], w_in_p, act=None, out_dtype=F32)

        q_a, k_a, v_a = _swa_prep(p, t_len, swa_tables)
        qc_a, kc_a, vc_a = _swa_prep(pc, n_ctx, None)
        o_swa = _swa_attn(q_a, k_a, v_a, kc_a, vc_a, swa_sink[l], bsz, t_len, n_ctx)

        gate_w = _lru_gate_weights(lru_wa[l], lru_ba[l], lru_wi[l], lru_bi[l])
        lru_args = (lru_conv_w[l], lru_conv_b[l].reshape(1, -1), *gate_w, lru_lam[l])
        caf, cbf, cab, cbb = _lru_coef(pc, n_ctx, *lru_args)
        hc_f, hlast_f = _lru_scan(caf, cbf, zeros_h, bsz, n_ctx, reverse=False)
        oc_lru, hlast_b = _lru_scan(cab, cbb, zeros_h, bsz, n_ctx, reverse=True, h_other=hc_f, p=pc)
        laf, lbf, lab, lbb = _lru_coef(p, t_len, *lru_args)
        h_f, _ = _lru_scan(laf, lbf, hlast_f, bsz, t_len, reverse=False)
        o_lru, _ = _lru_scan(lab, lbb, hlast_b, bsz, t_len, reverse=True, h_other=h_f, p=p)

        wq = _permute_w_qb(mla_w_qb[l])
        wkv = _permute_w_kvb(mla_w_kvb[l])
        q_c, k_c, v_c = _mla_proj(p, t_len, mla_q_norm[l], mla_kv_norm[l], wq, wkv, mla_tables)
        qc_c, kc_c, vc_c = _mla_proj(pc, n_ctx, mla_q_norm[l], mla_kv_norm[l], wq, wkv, None)
        o_mla = _mla_flash(q_c, kc_c, vc_c, bsz, t_len, n_ctx, k=k_c, v=v_c)

        rw = _rwkv_weights(rwkv_mu_prev[l], rwkv_mu_next[l], rwkv_w0[l], rwkv_w2[l], rwkv_a0[l], rwkv_a2[l],
                           rwkv_g2[l], rwkv_k_k[l], rwkv_k_a[l], rwkv_r_k[l], bd)
        fc = _rwkv_feat(pc, n_ctx, rw)
        yc_f, s_f = _wkv_scan(fc[0], fc[1], fc[2], fc[5], fc[6], fc[7], zeros_s, bsz, n_ctx, reverse=False)
        yc_b, s_b = _wkv_scan(fc[0], fc[1], fc[2], fc[8], fc[9], fc[10], zeros_s, bsz, n_ctx, reverse=True)
        fl = _rwkv_feat(p, t_len, rw)
        y_f, _ = _wkv_scan(fl[0], fl[1], fl[2], fl[5], fl[6], fl[7], s_f, bsz, t_len, reverse=False)
        y_b, _ = _wkv_scan(fl[0], fl[1], fl[2], fl[8], fl[9], fl[10], s_b, bsz, t_len, reverse=True)
        o_rwkv = _rwkv_out(y_f, y_b, fl[4], fl[3], rwkv_ln_g[l], rwkv_ln_b[l], bd, t_len)

        w_out_b = w_out[l].astype(BF16)
        w_ff1_b = w_ff1[l].astype(BF16)
        w_ff2_b = w_ff2[l].astype(BF16)

        def tail(xin, mix, m):
            x1 = _matmul_res_ln(mix, w_out_b, xin, m[2], ln1_g[l], ln1_b[l])
            hid = _ln_mod_matmul(x1, m[3], m[4], w_ff1_b, act="relu2", out_dtype=BF16, tm=1024)
            return _matmul_res_ln(hid, w_ff2_b, x1, m[5], ln2_g[l], ln2_b[l])

        if with_ctx:
            oc_swa = _swa_ctx_attn(qc_a, kc_a, vc_a, swa_sink[l], bsz, n_ctx)
            oc_mla = _mla_flash(qc_c, kc_c, vc_c, bsz, n_ctx, n_ctx)
            oc_rwkv = _rwkv_out(yc_f, yc_b, fc[4], fc[3], rwkv_ln_g[l], rwkv_ln_b[l], bd, n_ctx)
            xc = tail(xc, jnp.concatenate([oc_swa, oc_lru, oc_mla, oc_rwkv], axis=-1), cxm)

        xl = tail(xl, jnp.concatenate([o_swa, o_lru, o_mla, o_rwkv], axis=-1), lat)

    return xl.reshape(bsz, t_len, d)
```

```python
import functools

import numpy as np
import jax
import jax.numpy as jnp
from jax import lax
from jax.experimental import pallas as pl
from jax.experimental.pallas import tpu as pltpu

F32 = jnp.float32
BF16 = jnp.bfloat16

D_MODEL = 4096
GRID_W = 64
GROUP_W = 1024
D_FF = 4 * D_MODEL
BLOCK = 128

SWA_HEAD_DIM = 128
SWA_HEADS = 8
SWA_KV_HEADS = 2
SWA_GROUP = 4

LRU_BLOCKS = 8
LRU_BLOCK_W = 128
LRU_C = 8.0

MLA_HEADS = 8
MLA_NOPE = 128
MLA_ROPE = 64
MLA_V = 128
MLA_HEAD_PAD = 256

RWKV_HEAD = 64
RWKV_HEADS = 16
RWKV_GATE_LORA = 160
RWKV_GN_EPS = 64e-5
WKV_CHUNK = 64
WKV_INV_BLOCK = 16

ROPE_BASE = 10000.0
LN_EPS = 1e-5
RMS_EPS = 1e-6
DEPTH = 2
ALPHA = (2 * DEPTH) ** 0.25

OFF_AQ, OFF_BX, OFF_BG, OFF_CQ, OFF_DR, OFF_DK, OFF_DV = 0, 1024, 2048, 3072, 4096, 5120, 6144
OFF_CKV, OFF_AK, OFF_AV, OFF_DL, OFF_CR = 7168, 7680, 7936, 8192, 8704
N_PROJ = 9216
_PROJ_SEGMENTS = ((0, 1024), (1536, 2560), (2560, 3584), (3584, 4608), (5184, 6208), (6208, 7232),
                  (7232, 8256), (4608, 5120), (1024, 1280), (1280, 1536), (8256, 8672), 96,
                  (5120, 5184), 64, 384)

VMEM_LIMIT_V7X = 56 * 1024 * 1024


def _params(*sem):
    return pltpu.CompilerParams(dimension_semantics=sem, vmem_limit_bytes=VMEM_LIMIT_V7X)


def _split2(a):
    hi = a.astype(BF16)
    lo = (a - hi.astype(F32)).astype(BF16)
    return hi, lo


def _mm(a, b):
    return jnp.dot(a, b, preferred_element_type=F32)


def _mm_nt(a, b):
    return lax.dot_general(a, b, (((1,), (1,)), ((), ())), preferred_element_type=F32)


def _mm_tn(a, b):
    return lax.dot_general(a, b, (((0,), (0,)), ((), ())), preferred_element_type=F32)


def _mm3(a, b, mm=_mm):
    ah, al = _split2(a)
    bh, bl = _split2(b)
    return mm(ah, bh) + (mm(al, bh) + mm(ah, bl))


def _mm3w(a, bh, bl):
    ah, al = _split2(a)
    return _mm(ah, bh) + (_mm(al, bh) + _mm(ah, bl))


def _mm1(a, b, mm=_mm):
    return mm(a.astype(BF16), b.astype(BF16))


def _seg_sum(x, bd):
    parts = []
    for j in range(x.shape[1] // 128):
        hi, lo = _split2(x[:, 128 * j:128 * (j + 1)])
        parts.append(_mm(hi, bd) + _mm(lo, bd))
    return jnp.concatenate(parts, axis=1)


def _layer_norm_rows(x):
    mu = jnp.mean(x, -1, keepdims=True)
    xc = x - mu
    var = jnp.mean(xc * xc, -1, keepdims=True)
    return xc * lax.rsqrt(var + LN_EPS)


def _softplus(z):
    return jnp.maximum(z, 0.0) + jnp.log1p(jnp.exp(-jnp.abs(z)))


def _gelu_tanh(x):
    return 0.5 * x * (1.0 + jnp.tanh(np.sqrt(2.0 / np.pi).astype(np.float32) * (x + 0.044715 * (x * x * x))))


def _rope(x, cos, sin_a, sin_b, half):
    return x * cos + pltpu.roll(x, 128 - half, 1) * sin_a + pltpu.roll(x, half, 1) * sin_b


def _shift_rows(x, row, tm, prev_rows, next_row):
    p6, p7 = prev_rows
    xm1 = jnp.where(row == 0, p7, pltpu.roll(x, 1, 0))
    xp1 = jnp.where(row == tm - 1, next_row, pltpu.roll(x, tm - 1, 0))
    xm2 = None
    if p6 is not None:
        xm2 = jnp.where(row == 0, p6, jnp.where(row == 1, p7, pltpu.roll(x, 2, 0)))
    return xm2, xm1, xp1


def _halo_specs(col, width, tm, m_rows):
    r8 = tm // 8
    last8 = m_rows // 8 - 1
    return [pl.BlockSpec((tm, width), lambda i: (i, col)),
            pl.BlockSpec((8, width), lambda i: (jnp.maximum(i * r8 - 1, 0), col)),
            pl.BlockSpec((8, width), lambda i: (jnp.minimum((i + 1) * r8, last8), col))]


def _full(shape):
    nd = len(shape)
    return pl.BlockSpec(shape, lambda *_: (0,) * nd)


def _mod_kernel(c_ref, w_ref, b_ref, o_ref):
    cc = c_ref[...]
    a = cc * jax.nn.sigmoid(cc)
    o_ref[...] = _mm3(a, w_ref[...]) + b_ref[...]


def _modulation(cc, w_mod, b_mod, tn=512):
    depth, d, n = w_mod.shape
    return pl.pallas_call(
        _mod_kernel,
        out_shape=jax.ShapeDtypeStruct((depth, 8, n), F32),
        grid=(depth, n // tn),
        in_specs=[pl.BlockSpec((8, d), lambda l, j: (0, 0)),
                  pl.BlockSpec((None, d, tn), lambda l, j: (l, 0, j)),
                  pl.BlockSpec((None, 1, tn), lambda l, j: (l, 0, j))],
        out_specs=pl.BlockSpec((None, 8, tn), lambda l, j: (l, 0, j)),
        compiler_params=_params("parallel", "parallel"),
        name="adaln_modulation",
    )(cc, w_mod, b_mod.reshape(depth, 1, n))


def _lnmm_kernel(x_ref, sh_ref, sc_ref, w_ref, o_ref, xn_ref, *, act, tm):
    @pl.when(pl.program_id(1) == 0)
    def _():
        scale = 1.0 + sc_ref[...]
        shift = sh_ref[...]
        rows = min(tm, 128)

        def body(r, carry):
            sl = pl.ds(pl.multiple_of(r * rows, rows), rows)
            xn_ref[sl, :] = (_layer_norm_rows(x_ref[sl, :]) * scale + shift).astype(BF16)
            return carry

        lax.fori_loop(0, tm // rows, body, 0)

    acc = _mm(xn_ref[...], w_ref[...])
    if act == "relu2":
        acc = jnp.maximum(acc, 0.0)
        acc = acc * acc
    o_ref[...] = acc.astype(o_ref.dtype)


def _ln_mod_matmul(x, shift, scale, w, *, act, out_dtype, tm=512, tn=1024):
    m, k = x.shape
    n = w.shape[1]
    nbm = shift.shape[0]
    tm = min(tm, m // nbm)
    seq_tiles = m // nbm // tm
    mod_spec = pl.BlockSpec((None, 1, k), lambda i, j: (i // seq_tiles, 0, 0))
    x_spec = pl.BlockSpec((tm, k), lambda i, j: (i, 0), pipeline_mode=pl.Buffered(1))
    return pl.pallas_call(
        functools.partial(_lnmm_kernel, act=act, tm=tm),
        out_shape=jax.ShapeDtypeStruct((m, n), out_dtype),
        grid=(m // tm, n // tn),
        in_specs=[x_spec, mod_spec, mod_spec,
                  pl.BlockSpec((k, tn), lambda i, j: (0, j))],
        out_specs=pl.BlockSpec((tm, tn), lambda i, j: (i, j)),
        scratch_shapes=[pltpu.VMEM((tm, k), BF16)],
        compiler_params=_params("parallel", "arbitrary"),
        name="ln_mod_matmul_" + (act or "id"),
    )(x, shift, scale, w)


def _mmln_kernel(a_ref, w_ref, x_ref, ga_ref, g_ref, b_ref, o_ref, *, nk, tm):
    k = pl.program_id(1)

    @pl.when(k == 0)
    def _():
        o_ref[...] = _mm(a_ref[...], w_ref[...])

    @pl.when(k > 0)
    def _():
        o_ref[...] += _mm(a_ref[...], w_ref[...])

    @pl.when(k == nk - 1)
    def _():
        gate = ga_ref[...]
        g = g_ref[...]
        b = b_ref[...]
        rows = min(tm, 128)

        def body(r, carry):
            sl = pl.ds(pl.multiple_of(r * rows, rows), rows)
            z = ALPHA * x_ref[sl, :] + gate * o_ref[sl, :]
            o_ref[sl, :] = _layer_norm_rows(z) * g + b
            return carry

        lax.fori_loop(0, tm // rows, body, 0)


def _matmul_res_ln(a, w, xres, gate, g, b, *, tm=1024, tk=512):
    m, kdim = a.shape
    n = w.shape[1]
    nbm = gate.shape[0]
    tm = min(tm, m // nbm)
    seq_tiles = m // nbm // tm
    nk = kdim // tk
    return pl.pallas_call(
        functools.partial(_mmln_kernel, nk=nk, tm=tm),
        out_shape=jax.ShapeDtypeStruct((m, n), F32),
        grid=(m // tm, nk),
        in_specs=[pl.BlockSpec((tm, tk), lambda i, k: (i, k)),
                  pl.BlockSpec((tk, n), lambda i, k: (k, 0)),
                  pl.BlockSpec((tm, n), lambda i, k: (i, 0), pipeline_mode=pl.Buffered(1)),
                  pl.BlockSpec((None, 1, n), lambda i, k: (i // seq_tiles, 0, 0)),
                  _full((1, n)), _full((1, n))],
        out_specs=pl.BlockSpec((tm, n), lambda i, k: (i, 0), pipeline_mode=pl.Buffered(1)),
        compiler_params=_params("parallel", "arbitrary"),
        name="matmul_res_ln",
    )(a, w, xres, gate, g.reshape(1, n), b.reshape(1, n))


def _swa_prep_kernel(*refs, rotate):
    if rotate:
        q_ref, kv_ref, cos_ref, sa_ref, sb_ref, qo_ref, ko_ref, vo_ref = refs
        cos, sa, sb = cos_ref[...], sa_ref[...], sb_ref[...]
    else:
        q_ref, kv_ref, qo_ref, ko_ref, vo_ref = refs
    scale = np.float32(SWA_HEAD_DIM ** -0.5)
    for h in range(SWA_HEADS):
        sl = slice(128 * h, 128 * (h + 1))
        q = q_ref[:, sl]
        if rotate:
            q = _rope(q, cos, sa, sb, 32)
        qo_ref[:, sl] = (q * scale).astype(BF16)
    for h in range(SWA_KV_HEADS):
        sl = slice(128 * h, 128 * (h + 1))
        k = kv_ref[:, sl]
        if rotate:
            k = _rope(k, cos, sa, sb, 32)
        ko_ref[:, sl] = k.astype(BF16)
    vo_ref[...] = kv_ref[:, 256:512].astype(BF16)


def _swa_prep(p, t_len, tables, tm=256):
    m = p.shape[0]
    tm = min(tm, t_len)
    seq_tiles = t_len // tm
    rotate = tables is not None
    in_specs = [pl.BlockSpec((tm, 1024), lambda i: (i, OFF_AQ // 1024)),
                pl.BlockSpec((tm, 512), lambda i: (i, OFF_AK // 512))]
    args = [p, p]
    if rotate:
        in_specs += [pl.BlockSpec((tm, 128), lambda i: (i % seq_tiles, 0))] * 3
        args += list(tables)
    return pl.pallas_call(
        functools.partial(_swa_prep_kernel, rotate=rotate),
        out_shape=(jax.ShapeDtypeStruct((m, 1024), BF16), jax.ShapeDtypeStruct((m, 256), BF16),
                   jax.ShapeDtypeStruct((m, 256), BF16)),
        grid=(m // tm,),
        in_specs=in_specs,
        out_specs=(pl.BlockSpec((tm, 1024), lambda i: (i, 0)), pl.BlockSpec((tm, 256), lambda i: (i, 0)),
                   pl.BlockSpec((tm, 256), lambda i: (i, 0))),
        compiler_params=_params("parallel"),
        name="swa_prep_rope" if rotate else "swa_prep",
    )(*args)


def _sink_softmax_pv(s, sink, v):
    m = jnp.maximum(jnp.max(s, -1, keepdims=True), sink)
    p = jnp.exp(s - m)
    den = jnp.sum(p, -1, keepdims=True) + jnp.exp(sink - m)
    return _mm(p.astype(BF16), v) / den


def _swa_kernel(q_ref, kp_ref, ko_ref, kn_ref, vp_ref, vo_ref, vn_ref, kc_ref, vc_ref, sink_ref, o_ref, *, nb, n_ctx):
    n = pl.program_id(1)
    qi = lax.broadcasted_iota(jnp.int32, (BLOCK, BLOCK), 0)
    kj = lax.broadcasted_iota(jnp.int32, (BLOCK, BLOCK), 1)
    neg = np.float32(-np.inf)
    bias_prev = jnp.where(jnp.logical_and(kj >= qi, n > 0), 0.0, neg)
    bias_next = jnp.where(jnp.logical_and(kj <= qi, n < nb - 1), 0.0, neg)
    bias = jnp.concatenate([bias_prev, jnp.zeros((BLOCK, BLOCK), F32), bias_next,
                            jnp.zeros((BLOCK, n_ctx), F32)], axis=1)
    for hh in range(SWA_KV_HEADS):
        sl = slice(128 * hh, 128 * (hh + 1))
        k_all = jnp.concatenate([kp_ref[:, sl], ko_ref[:, sl], kn_ref[:, sl], kc_ref[:, sl]], axis=0)
        v_all = jnp.concatenate([vp_ref[:, sl], vo_ref[:, sl], vn_ref[:, sl], vc_ref[:, sl]], axis=0)
        for g in range(SWA_GROUP):
            h = hh * SWA_GROUP + g
            hs = slice(128 * h, 128 * (h + 1))
            s = _mm_nt(q_ref[:, hs], k_all) + bias
            o_ref[:, hs] = _sink_softmax_pv(s, sink_ref[0:1, h:h + 1], v_all).astype(BF16)


def _swa_attn(q, k, v, kc, vc, sink, bsz, t_len, n_ctx):
    nb = t_len // BLOCK

    def blk(width, off):
        def idx(b, n):
            return (b * nb + jnp.clip(n + off, 0, nb - 1), 0)
        return pl.BlockSpec((BLOCK, width), idx)

    ctx_spec = pl.BlockSpec((n_ctx, 256), lambda b, n: (b, 0))
    return pl.pallas_call(
        functools.partial(_swa_kernel, nb=nb, n_ctx=n_ctx),
        out_shape=jax.ShapeDtypeStruct((bsz * t_len, 1024), BF16),
        grid=(bsz, nb),
        in_specs=[blk(1024, 0), blk(256, -1), blk(256, 0), blk(256, 1), blk(256, -1), blk(256, 0), blk(256, 1),
                  ctx_spec, ctx_spec, _full((1, SWA_HEADS))],
        out_specs=blk(1024, 0),
        compiler_params=_params("parallel", "parallel"),
        name="swa_attention",
    )(q, k, k, k, v, v, v, kc, vc, sink.reshape(1, SWA_HEADS))


def _swa_ctx_kernel(q_ref, kc_ref, vc_ref, sink_ref, o_ref):
    for hh in range(SWA_KV_HEADS):
        sl = slice(128 * hh, 128 * (hh + 1))
        for g in range(SWA_GROUP):
            h = hh * SWA_GROUP + g
            hs = slice(128 * h, 128 * (h + 1))
            s = _mm_nt(q_ref[:, hs], kc_ref[:, sl])
            o_ref[:, hs] = _sink_softmax_pv(s, sink_ref[0:1, h:h + 1], vc_ref[:, sl]).astype(BF16)


def _swa_ctx_attn(qc, kc, vc, sink, bsz, n_ctx):
    return pl.pallas_call(
        _swa_ctx_kernel,
        out_shape=jax.ShapeDtypeStruct((bsz * n_ctx, 1024), BF16),
        grid=(bsz,),
        in_specs=[pl.BlockSpec((n_ctx, 1024), lambda b: (b, 0)), pl.BlockSpec((n_ctx, 256), lambda b: (b, 0)),
                  pl.BlockSpec((n_ctx, 256), lambda b: (b, 0)), _full((1, SWA_HEADS))],
        out_specs=pl.BlockSpec((n_ctx, 1024), lambda b: (b, 0)),
        compiler_params=_params("parallel"),
        name="swa_ctx_attention",
    )(qc, kc, vc, sink.reshape(1, SWA_HEADS))


def _lru_coef_kernel(x_ref, xp_ref, xn_ref, cw_ref, cb_ref, wh_ref, wl_ref, bg_ref, lam_ref,
                     af_ref, bf_ref, ab_ref, bb_ref, *, tm, seq_tiles):
    i = pl.program_id(0) % seq_tiles
    first = i == 0
    last = i == seq_tiles - 1
    x = x_ref[...]
    row = lax.broadcasted_iota(jnp.int32, x.shape, 0)
    p6 = jnp.where(first, 0.0, xp_ref[6:7, :])
    p7 = jnp.where(first, 0.0, xp_ref[7:8, :])
    n0 = jnp.where(last, 0.0, xn_ref[0:1, :])
    xm2, xm1, xp1 = _shift_rows(x, row, tm, (p6, p7), n0)
    u = (cw_ref[0:1, :] * xm2 + cw_ref[1:2, :] * xm1 + cw_ref[2:3, :] * x + cw_ref[3:4, :] * xp1) + cb_ref[...]
    sp = _softplus(-lam_ref[...])
    outs = ((af_ref, bf_ref), (ab_ref, bb_ref))
    for n in range(LRU_BLOCKS):
        sl = slice(128 * n, 128 * (n + 1))
        un = u[:, sl]
        z = _mm3w(un, wh_ref[n], wl_ref[n]) + bg_ref[n]
        for d in range(2):
            r = jax.nn.sigmoid(z[:, 256 * d:256 * d + 128])
            gi = jax.nn.sigmoid(z[:, 256 * d + 128:256 * d + 256])
            log_a = (-LRU_C) * r * sp[d:d + 1, sl]
            a_ref, b_ref = outs[d]
            a_ref[:, sl] = jnp.exp(log_a)
            b_ref[:, sl] = jnp.sqrt(1.0 - jnp.exp(2.0 * log_a)) * (gi * un)


def _lru_coef(p, t_len, conv_w, conv_b, wg_hi, wg_lo, bg, lam, tm=256):
    m = p.shape[0]
    tm = min(tm, t_len)
    seq_tiles = t_len // tm
    out = jax.ShapeDtypeStruct((m, 1024), F32)
    ospec = pl.BlockSpec((tm, 1024), lambda i: (i, 0))
    return pl.pallas_call(
        functools.partial(_lru_coef_kernel, tm=tm, seq_tiles=seq_tiles),
        out_shape=(out, out, out, out),
        grid=(m // tm,),
        in_specs=_halo_specs(OFF_BX // 1024, 1024, tm, m) + [
            _full((4, 1024)), _full((1, 1024)), _full((8, 128, 512)), _full((8, 128, 512)),
            _full((8, 1, 512)), _full((2, 1024))],
        out_specs=(ospec, ospec, ospec, ospec),
        compiler_params=_params("parallel"),
        name="lru_coefficients",
    )(p, p, p, conv_w, conv_b, wg_hi, wg_lo, bg, lam)


def _lru_scan_kernel(*refs, reverse, final, nc, tc):
    if final:
        a_ref, b_ref, h0_ref, hf_ref, gate_ref, o_ref, hl_ref, hs_ref, hrows_ref = refs
    else:
        a_ref, b_ref, h0_ref, o_ref, hl_ref, hs_ref = refs
        hrows_ref = o_ref
    c = pl.program_id(1)

    @pl.when(c == 0)
    def _():
        hs_ref[...] = h0_ref[...]

    def body(t, h):
        tt = (tc - 1 - t) if reverse else t
        h = a_ref[pl.ds(tt, 1), :] * h + b_ref[pl.ds(tt, 1), :]
        hrows_ref[pl.ds(tt, 1), :] = h
        return h

    h = lax.fori_loop(0, tc, body, hs_ref[...], unroll=8)
    hs_ref[...] = h

    @pl.when(c == (0 if reverse else nc - 1))
    def _():
        hl_ref[...] = hrows_ref[tc - 1:tc, :]

    if final:
        o_ref[...] = ((hf_ref[...] + hrows_ref[...]) * _gelu_tanh(gate_ref[...])).astype(BF16)


def _lru_scan(a, b, h0, bsz, t_len, *, reverse, h_other=None, p=None, tc=256):
    tc = min(tc, t_len)
    nc = t_len // tc
    final = h_other is not None

    def row_idx(bi, c):
        return bi * nc + ((nc - 1 - c) if reverse else c)

    rows = pl.BlockSpec((tc, 1024), lambda bi, c: (row_idx(bi, c), 0))
    state = pl.BlockSpec((None, 1, 1024), lambda bi, c: (bi, 0, 0))
    in_specs = [rows, rows, state]
    args = [a, b, h0]
    scratch = [pltpu.VMEM((1, 1024), F32)]
    if final:
        in_specs += [rows, pl.BlockSpec((tc, 1024), lambda bi, c: (row_idx(bi, c), OFF_BG // 1024))]
        args += [h_other, p]
        scratch.append(pltpu.VMEM((tc, 1024), F32))
    return pl.pallas_call(
        functools.partial(_lru_scan_kernel, reverse=reverse, final=final, nc=nc, tc=tc),
        out_shape=(jax.ShapeDtypeStruct((bsz * t_len, 1024), BF16 if final else F32),
                   jax.ShapeDtypeStruct((bsz, 1, 1024), F32)),
        grid=(bsz, nc),
        in_specs=in_specs,
        out_specs=(rows, state),
        scratch_shapes=scratch,
        compiler_params=_params("parallel", "arbitrary"),
        name="lru_scan_" + ("bwd_out" if final else ("bwd" if reverse else "fwd")),
    )(*args)


def _mla_proj_kernel(*refs, rotate):
    if rotate:
        (qa_ref, kva_ref, kr_ref, qn_ref, kvn_ref, wq_ref, wkv_ref, cos_ref, sa_ref, sb_ref,
         q_ref, k_ref, v_ref) = refs
        cos, sa, sb = cos_ref[...], sa_ref[...], sb_ref[...]
    else:
        qa_ref, kva_ref, kr_ref, qn_ref, kvn_ref, wq_ref, wkv_ref, q_ref, k_ref, v_ref = refs

    def rms(x, g):
        return (x * lax.rsqrt(jnp.mean(x * x, -1, keepdims=True) + RMS_EPS) * g).astype(BF16)

    scale = np.float32((MLA_NOPE + MLA_ROPE) ** -0.5 * np.log2(np.e))
    q = _mm(rms(qa_ref[...], qn_ref[...]), wq_ref[...]) * scale
    kv = _mm(rms(kva_ref[...], kvn_ref[...]), wkv_ref[...])
    kr = kr_ref[...]
    if rotate:
        kr = _rope(kr, cos, sa, sb, 16)
    kr = kr.astype(BF16)
    lane = lax.broadcasted_iota(jnp.int32, kr.shape, 1)
    ones_col = jnp.where(lane == 0, 1.0, 0.0).astype(BF16)
    for h in range(MLA_HEADS):
        lo = MLA_HEAD_PAD * h
        q_ref[:, lo:lo + 128] = q[:, lo:lo + 128].astype(BF16)
        qr = q[:, lo + 128:lo + 256]
        if rotate:
            qr = _rope(qr, cos, sa, sb, 16)
        q_ref[:, lo + 128:lo + 256] = qr.astype(BF16)
        k_ref[:, lo:lo + 128] = kv[:, 128 * h:128 * (h + 1)].astype(BF16)
        k_ref[:, lo + 128:lo + 256] = kr
        v_ref[:, lo:lo + 128] = kv[:, 1024 + 128 * h:1024 + 128 * (h + 1)].astype(BF16)
        v_ref[:, lo + 128:lo + 256] = ones_col


def _mla_proj(p, t_len, q_norm, kv_norm, wq, wkv, tables, tm=256):
    m = p.shape[0]
    tm = min(tm, t_len)
    seq_tiles = t_len // tm
    rotate = tables is not None
    in_specs = [pl.BlockSpec((tm, 1024), lambda i: (i, OFF_CQ // 1024)),
                pl.BlockSpec((tm, 512), lambda i: (i, OFF_CKV // 512)),
                pl.BlockSpec((tm, 128), lambda i: (i, OFF_CR // 128)),
                _full((1, 1024)), _full((1, 512)), _full((1024, 2048)), _full((512, 2048))]
    args = [p, p, p, q_norm.reshape(1, -1), kv_norm.reshape(1, -1), wq, wkv]
    if rotate:
        in_specs += [pl.BlockSpec((tm, 128), lambda i: (i % seq_tiles, 0))] * 3
        args += list(tables)
    wide = pl.BlockSpec((tm, 2048), lambda i: (i, 0))
    return pl.pallas_call(
        functools.partial(_mla_proj_kernel, rotate=rotate),
        out_shape=(jax.ShapeDtypeStruct((m, 2048), BF16),) * 3,
        grid=(m // tm,),
        in_specs=in_specs,
        out_specs=(wide, wide, wide),
        compiler_params=_params("parallel"),
        name="mla_project_rope" if rotate else "mla_project",
    )(*args)


def _mla_flash_kernel(*refs, with_lat, n_chunks, ck):
    if with_lat:
        q_ref, kc_ref, vc_ref, k_ref, v_ref, o_ref = refs
    else:
        q_ref, kc_ref, vc_ref, o_ref = refs
    q = q_ref[...]
    chunks = [(kc_ref, vc_ref, slice(None))]
    if with_lat:
        chunks += [(k_ref, v_ref, slice(j * ck, (j + 1) * ck)) for j in range(n_chunks)]
    k0, _, sl0 = chunks[0]
    s_cur = _mm_nt(q, k0[sl0, :])
    m = acc = pending = s_next = None
    for j, (_, vj_ref, slj) in enumerate(chunks):
        if j + 1 < len(chunks):
            kn_ref, _, sln = chunks[j + 1]
            s_next = _mm_nt(q, kn_ref[sln, :])
        if pending is not None:
            p_prev, v_prev, alpha_prev = pending
            pv = _mm(p_prev, v_prev)
            acc = pv if acc is None else alpha_prev * acc + pv
        mx = jnp.max(s_cur, -1, keepdims=True)
        m_new = mx if m is None else jnp.maximum(m, mx)
        alpha = None if m is None else jnp.exp2(m - m_new)
        pending = (jnp.exp2((s_cur - m_new).astype(BF16)), vj_ref[slj, :], alpha)
        m = m_new
        s_cur = s_next
    p_prev, v_prev, alpha_prev = pending
    pv = _mm(p_prev, v_prev)
    acc = pv if acc is None else alpha_prev * acc + pv
    o_ref[...] = (acc[:, 0:MLA_V] / acc[:, MLA_V:MLA_V + 1]).astype(BF16)


def _mla_flash(q, kc, vc, bsz, tq_len, n_ctx, k=None, v=None, tq=512, ck=512):
    with_lat = k is not None
    tq = min(tq, tq_len)
    nq = tq_len // tq
    in_specs = [pl.BlockSpec((tq, MLA_HEAD_PAD), lambda b, h, i: (b * nq + i, h)),
                pl.BlockSpec((n_ctx, MLA_HEAD_PAD), lambda b, h, i: (b, h)),
                pl.BlockSpec((n_ctx, MLA_HEAD_PAD), lambda b, h, i: (b, h))]
    args = [q, kc, vc]
    n_chunks = 0
    if with_lat:
        t_len = k.shape[0] // bsz
        ck = min(ck, t_len)
        n_chunks = t_len // ck
        in_specs += [pl.BlockSpec((t_len, MLA_HEAD_PAD), lambda b, h, i: (b, h)),
                     pl.BlockSpec((t_len, MLA_HEAD_PAD), lambda b, h, i: (b, h))]
        args += [k, v]
    return pl.pallas_call(
        functools.partial(_mla_flash_kernel, with_lat=with_lat, n_chunks=n_chunks, ck=ck),
        out_shape=jax.ShapeDtypeStruct((bsz * tq_len, 1024), BF16),
        grid=(bsz, MLA_HEADS, nq),
        in_specs=in_specs,
        out_specs=pl.BlockSpec((tq, MLA_V), lambda b, h, i: (b * nq + i, h)),
        compiler_params=_params("parallel", "parallel", "parallel"),
        name="mla_flash" if with_lat else "mla_ctx_attention",
    )(*args)


def _rwkv_feat_kernel(xr_ref, xrp_ref, xrn_ref, xk_ref, xkp_ref, xkn_ref, xv_ref, xvp_ref, xvn_ref,
                      xl_ref, xlp_ref, xln_ref, mup_ref, mun_ref, w0_ref, a0_ref,
                      w2h_ref, w2l_ref, a2h_ref, a2l_ref, g2h_ref, g2l_ref, kk_w_ref, ka_ref, rk_ref, bd_ref,
                      r_o, v_o, kk_o, g_o, bonus_o, lwf_o, kf_o, bf_o, lwb_o, kb_o, bb_o, *, tm, seq_tiles):
    i = pl.program_id(0) % seq_tiles
    first = i == 0
    last = i == seq_tiles - 1

    def shifted(x_ref, xp_ref, xn_ref, lo, hi):
        x = x_ref[...]
        row = lax.broadcasted_iota(jnp.int32, x.shape, 0)
        p7 = jnp.where(first, 0.0, xp_ref[7:8, :])
        n0 = jnp.where(last, 0.0, xn_ref[0:1, :])
        _, xm1, xp1 = _shift_rows(x, row, tm, (None, p7), n0)
        return x + mup_ref[:, lo:hi] * (xm1 - x) + mun_ref[:, lo:hi] * (xp1 - x)

    r = shifted(xr_ref, xrp_ref, xrn_ref, 0, 1024)
    k = shifted(xk_ref, xkp_ref, xkn_ref, 1024, 2048)
    v = shifted(xv_ref, xvp_ref, xvn_ref, 2048, 3072)
    lo = shifted(xl_ref, xlp_ref, xln_ref, 3072, 3584)
    bd = bd_ref[...]

    kkr = k * kk_w_ref[...]
    kk = kkr / jnp.maximum(jnp.sqrt(_seg_sum(kkr * kkr, bd)), 1e-12)
    g = _mm3w(jax.nn.sigmoid(lo[:, 256:512]), g2h_ref[...], g2l_ref[...])
    wl = w0_ref[...] + _mm3w(jnp.tanh(lo[:, 0:128]), w2h_ref[...], w2l_ref[...])
    lw = np.float32(-np.exp(-0.5)) * jax.nn.sigmoid(wl)
    a = jax.nn.sigmoid(a0_ref[...] + _mm3w(lo[:, 128:256], a2h_ref[...], a2l_ref[...]))
    r_o[...] = r
    v_o[...] = v
    kk_o[...] = kk
    g_o[...] = g
    bonus = None
    for d, (lw_o, k_o, b_o) in enumerate(((lwf_o, kf_o, bf_o), (lwb_o, kb_o, bb_o))):
        a_d = a[:, 1024 * d:1024 * (d + 1)]
        k_d = k * (1.0 + (a_d - 1.0) * ka_ref[...])
        bo = _seg_sum(r * k_d * rk_ref[...], bd) * v
        bonus = bo if bonus is None else bonus + bo
        lw_o[...] = lw[:, 1024 * d:1024 * (d + 1)]
        k_o[...] = k_d
        b_o[...] = kk * a_d
    bonus_o[...] = bonus


def _rwkv_feat(p, t_len, wts, tm=256):
    m = p.shape[0]
    tm = min(tm, t_len)
    seq_tiles = t_len // tm
    in_specs = (_halo_specs(OFF_DR // 1024, 1024, tm, m) + _halo_specs(OFF_DK // 1024, 1024, tm, m)
                + _halo_specs(OFF_DV // 1024, 1024, tm, m) + _halo_specs(OFF_DL // 512, 512, tm, m)
                + [_full(w.shape) for w in wts])
    out = jax.ShapeDtypeStruct((m, 1024), F32)
    ospec = pl.BlockSpec((tm, 1024), lambda i: (i, 0))
    return pl.pallas_call(
        functools.partial(_rwkv_feat_kernel, tm=tm, seq_tiles=seq_tiles),
        out_shape=(out,) * 11,
        grid=(m // tm,),
        in_specs=in_specs,
        out_specs=(ospec,) * 11,
        compiler_params=_params("parallel"),
        name="rwkv_features",
    )(*([p] * 12), *wts)


def _wkv_operands(r_ref, v_ref, kk_ref, lw_ref, k_ref, b_ref, incl, reverse):
    cl = WKV_CHUNK
    tri = jnp.where(incl, 1.0, 0.0).astype(BF16)
    lw = lw_ref[...]
    l1 = lw.astype(BF16)
    rem = lw - l1.astype(F32)
    l2 = rem.astype(BF16)
    l3 = (rem - l2.astype(F32)).astype(BF16)
    cum = _mm(tri, l1) + (_mm(tri, l2) + _mm(tri, l3))
    total = cum[0:1, :] if reverse else cum[cl - 1:cl, :]
    e_k = jnp.exp(-cum)
    e_t = jnp.exp(total - cum)
    k_raw = k_ref[...]
    b_raw = b_ref[...]
    return dict(d_c=jnp.exp(total),
                r_t=(r_ref[...] * jnp.exp(cum)).astype(BF16),
                kk_t=(kk_ref[...] * jnp.exp(cum - lw)).astype(BF16),
                k_t=(k_raw * e_k).astype(BF16), b_t=(b_raw * e_k).astype(BF16),
                k_d=(k_raw * e_t).astype(BF16), b_d=(b_raw * e_t).astype(BF16),
                v=v_ref[...].astype(BF16))


def _wkv_kernel(rf_ref, vf_ref, kkf_ref, lwf_ref, kf_ref, bf_ref, rb_ref, vb_ref, kkb_ref, lwb_ref, kb_ref, bb_ref,
                s0f_ref, s0b_ref, yf_ref, yb_ref, soutf_ref, soutb_ref, sf_ref, sb_ref, *, nc):
    c = pl.program_id(1)
    cl = WKV_CHUNK

    @pl.when(c == 0)
    def _():
        sf_ref[...] = s0f_ref[...]
        sb_ref[...] = s0b_ref[...]

    ti = lax.broadcasted_iota(jnp.int32, (cl, cl), 0)
    si = lax.broadcasted_iota(jnp.int32, (cl, cl), 1)
    blk = (ti // WKV_INV_BLOCK) == (si // WKV_INV_BLOCK)
    masks = ((si <= ti, si < ti), (si >= ti, si > ti))
    ops = (_wkv_operands(rf_ref, vf_ref, kkf_ref, lwf_ref, kf_ref, bf_ref, masks[0][0], False),
           _wkv_operands(rb_ref, vb_ref, kkb_ref, lwb_ref, kb_ref, bb_ref, masks[1][0], True))
    s_all = (sf_ref[...], sb_ref[...])

    chains = [(d, slice(RWKV_HEAD * h, RWKV_HEAD * (h + 1))) for d in range(2) for h in range(RWKV_HEADS)]
    heads = range(len(chains))
    incl = [masks[d][0] for d, _ in chains]
    strict = [masks[d][1] for d, _ in chains]

    def lanes(name):
        return [ops[d][name][:, sl] for d, sl in chains]

    v_h, k_dh, b_dh, d_ch = lanes("v"), lanes("k_d"), lanes("b_d"), lanes("d_c")
    s0 = [s_all[d][sl, :] for d, sl in chains]
    left = [jnp.concatenate([kk, r], axis=0) for kk, r in zip(lanes("kk_t"), lanes("r_t"))]
    right = [jnp.concatenate([k, b], axis=0) for k, b in zip(lanes("k_t"), lanes("b_t"))]
    a = [_mm_nt(left[h], right[h]) for h in heads]
    ls = [_mm_nt(left[h], s0[h].astype(BF16)) for h in heads]
    a_l = [jnp.where(strict[h], a[h][0:cl, cl:2 * cl], 0.0) for h in heads]
    a_v = [jnp.concatenate([jnp.where(strict[h], a[h][0:cl, 0:cl], 0.0),
                            jnp.where(incl[h], a[h][cl:2 * cl, 0:cl], 0.0)], axis=0).astype(BF16) for h in heads]
    a_rb = [jnp.where(incl[h], a[h][cl:2 * cl, cl:2 * cl], 0.0).astype(BF16) for h in heads]
    av = [_mm(a_v[h], v_h[h]) for h in heads]
    dg = [jnp.where(blk, a_l[h], 0.0) for h in heads]
    off = [a_l[h] - dg[h] for h in heads]
    d2 = [_mm1(dg[h], dg[h]) for h in heads]
    d4 = [_mm1(d2[h], d2[h]) for h in heads]
    x = [d2[h] - dg[h] - _mm1(dg[h], d2[h]) for h in heads]
    d8 = [_mm1(d4[h], d4[h]) for h in heads]
    x = [x[h] + d4[h] + _mm1(x[h], d4[h]) for h in heads]
    t16 = [x[h] + d8[h] + _mm1(x[h], d8[h]) for h in heads]
    n1 = [off[h] + _mm1(t16[h], off[h]) for h in heads]
    n2 = [_mm1(n1[h], n1[h]) for h in heads]
    y1 = [t16[h] - n1[h] - _mm1(n1[h], t16[h]) for h in heads]
    t_m = [y1[h] + n2[h] + _mm1(n2[h], y1[h]) for h in heads]
    rhs = [ls[h][0:cl] + av[h][0:cl] for h in heads]
    u = [rhs[h] + _mm1(t_m[h], rhs[h]) for h in heads]
    vu = [jnp.concatenate([v_h[h], (-u[h]).astype(BF16)], axis=0) for h in heads]
    kb = [jnp.concatenate([k_dh[h], b_dh[h]], axis=0) for h in heads]
    states = [s0[h] * d_ch[h] + _mm_tn(vu[h], kb[h]) for h in heads]
    ys = [ls[h][cl:2 * cl] + av[h][cl:2 * cl] - _mm(a_rb[h], u[h].astype(BF16)) for h in heads]
    yf_ref[...] = jnp.concatenate(ys[:RWKV_HEADS], axis=1)
    yb_ref[...] = jnp.concatenate(ys[RWKV_HEADS:], axis=1)
    sf_ref[...] = jnp.concatenate(states[:RWKV_HEADS], axis=0)
    sb_ref[...] = jnp.concatenate(states[RWKV_HEADS:], axis=0)

    @pl.when(c == nc - 1)
    def _():
        soutf_ref[...] = sf_ref[...]
        soutb_ref[...] = sb_ref[...]


def _wkv_scan(feats, s0_f, s0_b, bsz, t_len):
    r, v, kk, _, _, lw_f, k_f, b_f, lw_b, k_b, b_b = feats
    nc = t_len // WKV_CHUNK
    rows_f = pl.BlockSpec((WKV_CHUNK, GROUP_W), lambda bi, c: (bi * nc + c, 0))
    rows_b = pl.BlockSpec((WKV_CHUNK, GROUP_W), lambda bi, c: (bi * nc + nc - 1 - c, 0))
    state = pl.BlockSpec((None, GROUP_W, RWKV_HEAD), lambda bi, c: (bi, 0, 0))
    y_shape = jax.ShapeDtypeStruct((bsz * t_len, GROUP_W), F32)
    s_shape = jax.ShapeDtypeStruct((bsz, GROUP_W, RWKV_HEAD), F32)
    return pl.pallas_call(
        functools.partial(_wkv_kernel, nc=nc),
        out_shape=(y_shape, y_shape, s_shape, s_shape),
        grid=(bsz, nc),
        in_specs=[rows_f] * 6 + [rows_b] * 6 + [state, state],
        out_specs=(rows_f, rows_b, state, state),
        scratch_shapes=[pltpu.VMEM((GROUP_W, RWKV_HEAD), F32)] * 2,
        compiler_params=_params("parallel", "arbitrary"),
        name="wkv7_chunked",
    )(r, v, kk, lw_f, k_f, b_f, r, v, kk, lw_b, k_b, b_b, s0_f, s0_b)


def _rwkv_out_kernel(yf_ref, yb_ref, bonus_ref, g_ref, lng_ref, lnb_ref, bd_ref, o_ref):
    bd = bd_ref[...]
    y = yf_ref[...] + yb_ref[...]
    inv_n = np.float32(1.0 / RWKV_HEAD)
    yc = y - _seg_sum(y, bd) * inv_n
    var = _seg_sum(yc * yc, bd) * inv_n
    yn = yc * lax.rsqrt(var + RWKV_GN_EPS) * lng_ref[...] + lnb_ref[...]
    o_ref[...] = ((yn + bonus_ref[...]) * g_ref[...]).astype(BF16)


def _rwkv_out(y_f, y_b, bonus, g, ln_g, ln_b, bd, t_len, tm=256):
    m = y_f.shape[0]
    tm = min(tm, t_len)
    rows = pl.BlockSpec((tm, 1024), lambda i: (i, 0))
    return pl.pallas_call(
        _rwkv_out_kernel,
        out_shape=jax.ShapeDtypeStruct((m, 1024), BF16),
        grid=(m // tm,),
        in_specs=[rows] * 4 + [_full((1, 1024)), _full((1, 1024)), _full((128, 128))],
        out_specs=rows,
        compiler_params=_params("parallel"),
        name="rwkv_groupnorm_gate",
    )(y_f, y_b, bonus, g, ln_g.reshape(1, -1), ln_b.reshape(1, -1), bd)


def _permute_w_in(w):
    parts = []
    for seg in _PROJ_SEGMENTS:
        if isinstance(seg, tuple):
            parts.append(w[:, seg[0]:seg[1]])
        else:
            parts.append(jnp.zeros((w.shape[0], seg), w.dtype))
    return jnp.concatenate(parts, axis=1).astype(BF16)


def _permute_w_qb(w):
    w = w.reshape(w.shape[0], MLA_HEADS, MLA_NOPE + MLA_ROPE)
    w = jnp.pad(w, ((0, 0), (0, 0), (0, MLA_HEAD_PAD - MLA_NOPE - MLA_ROPE)))
    return w.reshape(w.shape[0], MLA_HEADS * MLA_HEAD_PAD).astype(BF16)


def _permute_w_kvb(w):
    w = w.reshape(w.shape[0], MLA_HEADS, MLA_NOPE + MLA_V)
    return jnp.concatenate([w[:, :, :MLA_NOPE].reshape(w.shape[0], -1),
                            w[:, :, MLA_NOPE:].reshape(w.shape[0], -1)], axis=1).astype(BF16)


def _rope_tables(pos_row, pos_col, half):
    inv = ROPE_BASE ** (-jnp.arange(half, dtype=F32) / half)
    zeros = jnp.zeros((pos_row.shape[0], half), F32)
    cos, sin_a, sin_b = [], [], []
    for pos in (pos_row, pos_col):
        ang = pos.astype(F32)[:, None] * inv[None, :]
        c, s = jnp.cos(ang), jnp.sin(ang)
        cos += [c, c]
        sin_a += [-s, zeros]
        sin_b += [zeros, s]
    pad = jnp.zeros((pos_row.shape[0], 128 - 4 * half), F32)
    return tuple(jnp.concatenate(t + [pad], axis=1) for t in (cos, sin_a, sin_b))


def _block_diag_ones():
    i = np.arange(128)
    return jnp.asarray((i[:, None] // RWKV_HEAD) == (i[None, :] // RWKV_HEAD), BF16)


def _split_w(w):
    hi = w.astype(BF16)
    return hi, (w - hi.astype(F32)).astype(BF16)


def _lru_gate_weights(wa, ba, wi, bi):
    w = jnp.concatenate([wa[0], wi[0], wa[1], wi[1]], axis=-1)
    b = jnp.concatenate([t.reshape(LRU_BLOCKS, 1, LRU_BLOCK_W) for t in (ba[0], bi[0], ba[1], bi[1])], axis=-1)
    return _split_w(w) + (b,)


def _rwkv_weights(mu_prev, mu_next, w0, w2, a0, a2, g2, k_k, k_a, r_k, bd):
    def pad_mu(mu):
        return jnp.pad(mu, (0, 3584 - mu.shape[0])).reshape(1, 3584)

    def two_dir(w):
        z = jnp.zeros_like(w[0])
        return jnp.concatenate([jnp.concatenate([w[0], z], axis=1), jnp.concatenate([z, w[1]], axis=1)], axis=0)

    g2p = jnp.pad(g2, ((0, 256 - RWKV_GATE_LORA), (0, 0)))
    return (pad_mu(mu_prev), pad_mu(mu_next), w0.reshape(1, 2048), a0.reshape(1, 2048),
            *_split_w(two_dir(w2)), *_split_w(two_dir(a2)), *_split_w(g2p),
            k_k.reshape(1, 1024), k_a.reshape(1, 1024), r_k.reshape(1, 1024), bd)


def kernel(x, c, ctx, c_ctx, w_mod, b_mod, w_in, w_out, ln1_g, ln1_b, w_ff1, w_ff2, ln2_g, ln2_b, swa_sink, lru_conv_w, lru_conv_b, lru_wa, lru_ba, lru_wi, lru_bi, lru_lam, mla_q_norm, mla_kv_norm, mla_w_qb, mla_w_kvb, rwkv_mu_prev, rwkv_mu_next, rwkv_w0, rwkv_w2, rwkv_a0, rwkv_a2, rwkv_g2, rwkv_k_k, rwkv_k_a, rwkv_r_k, rwkv_ln_g, rwkv_ln_b):
    bsz, t_len, d = x.shape
    n_ctx = ctx.shape[1]
    depth = w_mod.shape[0]

    cc = jnp.zeros((8, d), F32).at[:bsz].set(c).at[bsz].set(c_ctx)
    mod = _modulation(cc, w_mod, b_mod)

    pos = jnp.arange(t_len, dtype=jnp.int32)
    row, col = pos // GRID_W, pos % GRID_W
    swa_tables = _rope_tables(row, col, SWA_HEAD_DIM // 4)
    mla_tables = _rope_tables(row, col, MLA_ROPE // 4)
    bd = _block_diag_ones()
    zeros_h = jnp.zeros((bsz, 1, GROUP_W), F32)
    zeros_s = jnp.zeros((bsz, RWKV_HEADS * RWKV_HEAD, RWKV_HEAD), F32)

    xl = x.reshape(bsz * t_len, d)
    xc = ctx.reshape(bsz * n_ctx, d)

    for l in range(depth):
        with_ctx = l < depth - 1
        chunks = [mod[l, :, k * d:(k + 1) * d] for k in range(6)]
        lat = [m[:bsz][:, None, :] for m in chunks]
        cxm = [m[bsz:bsz + 1][:, None, :] for m in chunks]

        w_in_p = _permute_w_in(w_in[l])
        p = _ln_mod_matmul(xl, lat[0], lat[1], w_in_p, act=None, out_dtype=F32, tm=1024, tn=512)
        pc = _ln_mod_matmul(xc, cxm[0], cxm[1], w_in_p, act=None, out_dtype=F32)

        q_a, k_a, v_a = _swa_prep(p, t_len, swa_tables)
        qc_a, kc_a, vc_a = _swa_prep(pc, n_ctx, None)
        o_swa = _swa_attn(q_a, k_a, v_a, kc_a, vc_a, swa_sink[l], bsz, t_len, n_ctx)

        gate_w = _lru_gate_weights(lru_wa[l], lru_ba[l], lru_wi[l], lru_bi[l])
        lru_args = (lru_conv_w[l], lru_conv_b[l].reshape(1, -1), *gate_w, lru_lam[l])
        caf, cbf, cab, cbb = _lru_coef(pc, n_ctx, *lru_args)
        hc_f, hlast_f = _lru_scan(caf, cbf, zeros_h, bsz, n_ctx, reverse=False)
        oc_lru, hlast_b = _lru_scan(cab, cbb, zeros_h, bsz, n_ctx, reverse=True, h_other=hc_f, p=pc)
        laf, lbf, lab, lbb = _lru_coef(p, t_len, *lru_args)
        h_f, _ = _lru_scan(laf, lbf, hlast_f, bsz, t_len, reverse=False)
        o_lru, _ = _lru_scan(lab, lbb, hlast_b, bsz, t_len, reverse=True, h_other=h_f, p=p)

        wq = _permute_w_qb(mla_w_qb[l])
        wkv = _permute_w_kvb(mla_w_kvb[l])
        q_c, k_c, v_c = _mla_proj(p, t_len, mla_q_norm[l], mla_kv_norm[l], wq, wkv, mla_tables)
        qc_c, kc_c, vc_c = _mla_proj(pc, n_ctx, mla_q_norm[l], mla_kv_norm[l], wq, wkv, None)
        o_mla = _mla_flash(q_c, kc_c, vc_c, bsz, t_len, n_ctx, k=k_c, v=v_c)

        rw = _rwkv_weights(rwkv_mu_prev[l], rwkv_mu_next[l], rwkv_w0[l], rwkv_w2[l], rwkv_a0[l], rwkv_a2[l],
                           rwkv_g2[l], rwkv_k_k[l], rwkv_k_a[l], rwkv_r_k[l], bd)
        fc = _rwkv_feat(pc, n_ctx, rw)
        yc_f, yc_b, s_f, s_b = _wkv_scan(fc, zeros_s, zeros_s, bsz, n_ctx)
        fl = _rwkv_feat(p, t_len, rw)
        y_f, y_b, _, _ = _wkv_scan(fl, s_f, s_b, bsz, t_len)
        o_rwkv = _rwkv_out(y_f, y_b, fl[4], fl[3], rwkv_ln_g[l], rwkv_ln_b[l], bd, t_len)

        w_out_b = w_out[l].astype(BF16)
        w_ff1_b = w_ff1[l].astype(BF16)
        w_ff2_b = w_ff2[l].astype(BF16)

        def tail(xin, mix, m):
            x1 = _matmul_res_ln(mix, w_out_b, xin, m[2], ln1_g[l], ln1_b[l])
            hid = _ln_mod_matmul(x1, m[3], m[4], w_ff1_b, act="relu2", out_dtype=BF16, tm=1024)
            return _matmul_res_ln(hid, w_ff2_b, x1, m[5], ln2_g[l], ln2_b[l])

        if with_ctx:
            oc_swa = _swa_ctx_attn(qc_a, kc_a, vc_a, swa_sink[l], bsz, n_ctx)
            oc_mla = _mla_flash(qc_c, kc_c, vc_c, bsz, n_ctx, n_ctx)
            oc_rwkv = _rwkv_out(yc_f, yc_b, fc[4], fc[3], rwkv_ln_g[l], rwkv_ln_b[l], bd, n_ctx)
            xc = tail(xc, jnp.concatenate([oc_swa, oc_lru, oc_mla, oc_rwkv], axis=-1), cxm)

        xl = tail(xl, jnp.concatenate([o_swa, o_lru, o_mla, o_rwkv], axis=-1), lat)

    return xl.reshape(bsz, t_len, d)
```

```python
import functools

import numpy as np
import jax
import jax.numpy as jnp
from jax import lax
from jax.experimental import pallas as pl
from jax.experimental.pallas import tpu as pltpu

F32 = jnp.float32
BF16 = jnp.bfloat16

D_MODEL = 4096
GRID_W = 64
GROUP_W = 1024
D_FF = 4 * D_MODEL
BLOCK = 128

SWA_HEAD_DIM = 128
SWA_HEADS = 8
SWA_KV_HEADS = 2
SWA_GROUP = 4

LRU_BLOCKS = 8
LRU_BLOCK_W = 128
LRU_C = 8.0

MLA_HEADS = 8
MLA_NOPE = 128
MLA_ROPE = 64
MLA_V = 128
MLA_HEAD_PAD = 256

RWKV_HEAD = 64
RWKV_HEADS = 16
RWKV_GATE_LORA = 160
RWKV_GN_EPS = 64e-5
WKV_CHUNK = 64
WKV_INV_BLOCK = 16

ROPE_BASE = 10000.0
LN_EPS = 1e-5
RMS_EPS = 1e-6
DEPTH = 2
ALPHA = (2 * DEPTH) ** 0.25

OFF_AQ, OFF_BX, OFF_BG, OFF_CQ, OFF_DR, OFF_DK, OFF_DV = 0, 1024, 2048, 3072, 4096, 5120, 6144
OFF_CKV, OFF_AK, OFF_AV, OFF_DL, OFF_CR = 7168, 7680, 7936, 8192, 8704
N_PROJ = 9216
_PROJ_SEGMENTS = ((0, 1024), (1536, 2560), (2560, 3584), (3584, 4608), (5184, 6208), (6208, 7232),
                  (7232, 8256), (4608, 5120), (1024, 1280), (1280, 1536), (8256, 8672), 96,
                  (5120, 5184), 64, 384)

VMEM_LIMIT_V7X = 56 * 1024 * 1024


def _params(*sem):
    return pltpu.CompilerParams(dimension_semantics=sem, vmem_limit_bytes=VMEM_LIMIT_V7X)


def _split2(a):
    hi = a.astype(BF16)
    lo = (a - hi.astype(F32)).astype(BF16)
    return hi, lo


def _mm(a, b):
    return jnp.dot(a, b, preferred_element_type=F32)


def _mm_nt(a, b):
    return lax.dot_general(a, b, (((1,), (1,)), ((), ())), preferred_element_type=F32)


def _mm_tn(a, b):
    return lax.dot_general(a, b, (((0,), (0,)), ((), ())), preferred_element_type=F32)


def _mm3(a, b, mm=_mm):
    ah, al = _split2(a)
    bh, bl = _split2(b)
    return mm(ah, bh) + (mm(al, bh) + mm(ah, bl))


def _mm3w(a, bh, bl):
    ah, al = _split2(a)
    return _mm(ah, bh) + (_mm(al, bh) + _mm(ah, bl))


def _mm1(a, b, mm=_mm):
    return mm(a.astype(BF16), b.astype(BF16))


def _seg_sum(x, bd):
    parts = []
    for j in range(x.shape[1] // 128):
        hi, lo = _split2(x[:, 128 * j:128 * (j + 1)])
        parts.append(_mm(hi, bd) + _mm(lo, bd))
    return jnp.concatenate(parts, axis=1)


def _layer_norm_rows(x):
    mu = jnp.mean(x, -1, keepdims=True)
    xc = x - mu
    var = jnp.mean(xc * xc, -1, keepdims=True)
    return xc * lax.rsqrt(var + LN_EPS)


def _softplus(z):
    return jnp.maximum(z, 0.0) + jnp.log1p(jnp.exp(-jnp.abs(z)))


def _gelu_tanh(x):
    return 0.5 * x * (1.0 + jnp.tanh(np.sqrt(2.0 / np.pi).astype(np.float32) * (x + 0.044715 * (x * x * x))))


def _rope(x, cos, sin_a, sin_b, half):
    return x * cos + pltpu.roll(x, 128 - half, 1) * sin_a + pltpu.roll(x, half, 1) * sin_b


def _shift_rows(x, row, tm, prev_rows, next_row):
    p6, p7 = prev_rows
    xm1 = jnp.where(row == 0, p7, pltpu.roll(x, 1, 0))
    xp1 = jnp.where(row == tm - 1, next_row, pltpu.roll(x, tm - 1, 0))
    xm2 = None
    if p6 is not None:
        xm2 = jnp.where(row == 0, p6, jnp.where(row == 1, p7, pltpu.roll(x, 2, 0)))
    return xm2, xm1, xp1


def _halo_specs(col, width, tm, m_rows):
    r8 = tm // 8
    last8 = m_rows // 8 - 1
    return [pl.BlockSpec((tm, width), lambda i: (i, col)),
            pl.BlockSpec((8, width), lambda i: (jnp.maximum(i * r8 - 1, 0), col)),
            pl.BlockSpec((8, width), lambda i: (jnp.minimum((i + 1) * r8, last8), col))]


def _full(shape):
    nd = len(shape)
    return pl.BlockSpec(shape, lambda *_: (0,) * nd)


def _mod_kernel(c_ref, w_ref, b_ref, o_ref):
    cc = c_ref[...]
    a = cc * jax.nn.sigmoid(cc)
    o_ref[...] = _mm3(a, w_ref[...]) + b_ref[...]


def _modulation(cc, w_mod, b_mod, tn=512):
    depth, d, n = w_mod.shape
    return pl.pallas_call(
        _mod_kernel,
        out_shape=jax.ShapeDtypeStruct((depth, 8, n), F32),
        grid=(depth, n // tn),
        in_specs=[pl.BlockSpec((8, d), lambda l, j: (0, 0)),
                  pl.BlockSpec((None, d, tn), lambda l, j: (l, 0, j)),
                  pl.BlockSpec((None, 1, tn), lambda l, j: (l, 0, j))],
        out_specs=pl.BlockSpec((None, 8, tn), lambda l, j: (l, 0, j)),
        compiler_params=_params("parallel", "parallel"),
        name="adaln_modulation",
    )(cc, w_mod, b_mod.reshape(depth, 1, n))


def _lnmm_kernel(x_ref, sh_ref, sc_ref, w_ref, o_ref, xn_ref, *, act, tm):
    @pl.when(pl.program_id(1) == 0)
    def _():
        scale = 1.0 + sc_ref[...]
        shift = sh_ref[...]
        rows = min(tm, 128)

        def body(r, carry):
            sl = pl.ds(pl.multiple_of(r * rows, rows), rows)
            xn_ref[sl, :] = (_layer_norm_rows(x_ref[sl, :]) * scale + shift).astype(BF16)
            return carry

        lax.fori_loop(0, tm // rows, body, 0)

    acc = _mm(xn_ref[...], w_ref[...])
    if act == "relu2":
        acc = jnp.maximum(acc, 0.0)
        acc = acc * acc
    o_ref[...] = acc.astype(o_ref.dtype)


def _ln_mod_matmul(x, shift, scale, w, *, act, out_dtype, tm=512, tn=1024):
    m, k = x.shape
    n = w.shape[1]
    nbm = shift.shape[0]
    tm = min(tm, m // nbm)
    seq_tiles = m // nbm // tm
    mod_spec = pl.BlockSpec((None, 1, k), lambda i, j: (i // seq_tiles, 0, 0))
    x_spec = pl.BlockSpec((tm, k), lambda i, j: (i, 0), pipeline_mode=pl.Buffered(1))
    return pl.pallas_call(
        functools.partial(_lnmm_kernel, act=act, tm=tm),
        out_shape=jax.ShapeDtypeStruct((m, n), out_dtype),
        grid=(m // tm, n // tn),
        in_specs=[x_spec, mod_spec, mod_spec,
                  pl.BlockSpec((k, tn), lambda i, j: (0, j))],
        out_specs=pl.BlockSpec((tm, tn), lambda i, j: (i, j)),
        scratch_shapes=[pltpu.VMEM((tm, k), BF16)],
        compiler_params=_params("parallel", "arbitrary"),
        name="ln_mod_matmul_" + (act or "id"),
    )(x, shift, scale, w)


def _mmln_kernel(a_ref, w_ref, x_ref, ga_ref, g_ref, b_ref, o_ref, *, nk, tm):
    k = pl.program_id(1)

    @pl.when(k == 0)
    def _():
        o_ref[...] = _mm(a_ref[...], w_ref[...])

    @pl.when(k > 0)
    def _():
        o_ref[...] += _mm(a_ref[...], w_ref[...])

    @pl.when(k == nk - 1)
    def _():
        gate = ga_ref[...]
        g = g_ref[...]
        b = b_ref[...]
        rows = min(tm, 128)

        def body(r, carry):
            sl = pl.ds(pl.multiple_of(r * rows, rows), rows)
            z = ALPHA * x_ref[sl, :] + gate * o_ref[sl, :]
            o_ref[sl, :] = _layer_norm_rows(z) * g + b
            return carry

        lax.fori_loop(0, tm // rows, body, 0)


def _matmul_res_ln(a, w, xres, gate, g, b, *, tm=1024, tk=512):
    m, kdim = a.shape
    n = w.shape[1]
    nbm = gate.shape[0]
    tm = min(tm, m // nbm)
    seq_tiles = m // nbm // tm
    nk = kdim // tk
    mode = dict(pipeline_mode=pl.Buffered(1)) if nk >= 16 else {}
    return pl.pallas_call(
        functools.partial(_mmln_kernel, nk=nk, tm=tm),
        out_shape=jax.ShapeDtypeStruct((m, n), F32),
        grid=(m // tm, nk),
        in_specs=[pl.BlockSpec((tm, tk), lambda i, k: (i, k)),
                  pl.BlockSpec((tk, n), lambda i, k: (k, 0)),
                  pl.BlockSpec((tm, n), lambda i, k: (i, 0), **mode),
                  pl.BlockSpec((None, 1, n), lambda i, k: (i // seq_tiles, 0, 0)),
                  _full((1, n)), _full((1, n))],
        out_specs=pl.BlockSpec((tm, n), lambda i, k: (i, 0), **mode),
        compiler_params=_params("parallel", "arbitrary"),
        name="matmul_res_ln",
    )(a, w, xres, gate, g.reshape(1, n), b.reshape(1, n))


def _swa_prep_kernel(*refs, rotate):
    if rotate:
        q_ref, kv_ref, cos_ref, sa_ref, sb_ref, qo_ref, ko_ref, vo_ref = refs
        cos, sa, sb = cos_ref[...], sa_ref[...], sb_ref[...]
    else:
        q_ref, kv_ref, qo_ref, ko_ref, vo_ref = refs
    scale = np.float32(SWA_HEAD_DIM ** -0.5)
    for h in range(SWA_HEADS):
        sl = slice(128 * h, 128 * (h + 1))
        q = q_ref[:, sl]
        if rotate:
            q = _rope(q, cos, sa, sb, 32)
        qo_ref[:, sl] = (q * scale).astype(BF16)
    for h in range(SWA_KV_HEADS):
        sl = slice(128 * h, 128 * (h + 1))
        k = kv_ref[:, sl]
        if rotate:
            k = _rope(k, cos, sa, sb, 32)
        ko_ref[:, sl] = k.astype(BF16)
    vo_ref[...] = kv_ref[:, 256:512].astype(BF16)


def _swa_prep(p, t_len, tables, tm=256):
    m = p.shape[0]
    tm = min(tm, t_len)
    seq_tiles = t_len // tm
    rotate = tables is not None
    in_specs = [pl.BlockSpec((tm, 1024), lambda i: (i, OFF_AQ // 1024)),
                pl.BlockSpec((tm, 512), lambda i: (i, OFF_AK // 512))]
    args = [p, p]
    if rotate:
        in_specs += [pl.BlockSpec((tm, 128), lambda i: (i % seq_tiles, 0))] * 3
        args += list(tables)
    return pl.pallas_call(
        functools.partial(_swa_prep_kernel, rotate=rotate),
        out_shape=(jax.ShapeDtypeStruct((m, 1024), BF16), jax.ShapeDtypeStruct((m, 256), BF16),
                   jax.ShapeDtypeStruct((m, 256), BF16)),
        grid=(m // tm,),
        in_specs=in_specs,
        out_specs=(pl.BlockSpec((tm, 1024), lambda i: (i, 0)), pl.BlockSpec((tm, 256), lambda i: (i, 0)),
                   pl.BlockSpec((tm, 256), lambda i: (i, 0))),
        compiler_params=_params("parallel"),
        name="swa_prep_rope" if rotate else "swa_prep",
    )(*args)


def _sink_softmax_pv(s, sink, v):
    m = jnp.maximum(jnp.max(s, -1, keepdims=True), sink)
    p = jnp.exp(s - m)
    den = jnp.sum(p, -1, keepdims=True) + jnp.exp(sink - m)
    return _mm(p.astype(BF16), v) / den


def _swa_kernel(q_ref, kp_ref, ko_ref, kn_ref, vp_ref, vo_ref, vn_ref, kc_ref, vc_ref, sink_ref, o_ref, *, nb, n_ctx):
    n = pl.program_id(1)
    qi = lax.broadcasted_iota(jnp.int32, (BLOCK, BLOCK), 0)
    kj = lax.broadcasted_iota(jnp.int32, (BLOCK, BLOCK), 1)
    neg = np.float32(-np.inf)
    bias_prev = jnp.where(jnp.logical_and(kj >= qi, n > 0), 0.0, neg)
    bias_next = jnp.where(jnp.logical_and(kj <= qi, n < nb - 1), 0.0, neg)
    bias = jnp.concatenate([bias_prev, jnp.zeros((BLOCK, BLOCK), F32), bias_next,
                            jnp.zeros((BLOCK, n_ctx), F32)], axis=1)
    kv_sl = [slice(128 * hh, 128 * (hh + 1)) for hh in range(SWA_KV_HEADS)]
    k_all = [jnp.concatenate([kp_ref[:, sl], ko_ref[:, sl], kn_ref[:, sl], kc_ref[:, sl]], axis=0) for sl in kv_sl]
    v_all = [jnp.concatenate([vp_ref[:, sl], vo_ref[:, sl], vn_ref[:, sl], vc_ref[:, sl]], axis=0) for sl in kv_sl]
    heads = range(SWA_HEADS)
    s = [_mm_nt(q_ref[:, 128 * h:128 * (h + 1)], k_all[h // SWA_GROUP]) + bias for h in heads]
    sink = [sink_ref[0:1, h:h + 1] for h in heads]
    m = [jnp.maximum(jnp.max(s[h], -1, keepdims=True), sink[h]) for h in heads]
    p = [jnp.exp(s[h] - m[h]) for h in heads]
    den = [jnp.sum(p[h], -1, keepdims=True) + jnp.exp(sink[h] - m[h]) for h in heads]
    o = [_mm(p[h].astype(BF16), v_all[h // SWA_GROUP]) for h in heads]
    for h in heads:
        o_ref[:, 128 * h:128 * (h + 1)] = (o[h] / den[h]).astype(BF16)


def _swa_attn(q, k, v, kc, vc, sink, bsz, t_len, n_ctx):
    nb = t_len // BLOCK

    def blk(width, off):
        def idx(b, n):
            return (b * nb + jnp.clip(n + off, 0, nb - 1), 0)
        return pl.BlockSpec((BLOCK, width), idx)

    ctx_spec = pl.BlockSpec((n_ctx, 256), lambda b, n: (b, 0))
    return pl.pallas_call(
        functools.partial(_swa_kernel, nb=nb, n_ctx=n_ctx),
        out_shape=jax.ShapeDtypeStruct((bsz * t_len, 1024), BF16),
        grid=(bsz, nb),
        in_specs=[blk(1024, 0), blk(256, -1), blk(256, 0), blk(256, 1), blk(256, -1), blk(256, 0), blk(256, 1),
                  ctx_spec, ctx_spec, _full((1, SWA_HEADS))],
        out_specs=blk(1024, 0),
        compiler_params=_params("parallel", "parallel"),
        name="swa_attention",
    )(q, k, k, k, v, v, v, kc, vc, sink.reshape(1, SWA_HEADS))


def _swa_ctx_kernel(q_ref, kc_ref, vc_ref, sink_ref, o_ref):
    for hh in range(SWA_KV_HEADS):
        sl = slice(128 * hh, 128 * (hh + 1))
        for g in range(SWA_GROUP):
            h = hh * SWA_GROUP + g
            hs = slice(128 * h, 128 * (h + 1))
            s = _mm_nt(q_ref[:, hs], kc_ref[:, sl])
            o_ref[:, hs] = _sink_softmax_pv(s, sink_ref[0:1, h:h + 1], vc_ref[:, sl]).astype(BF16)


def _swa_ctx_attn(qc, kc, vc, sink, bsz, n_ctx):
    return pl.pallas_call(
        _swa_ctx_kernel,
        out_shape=jax.ShapeDtypeStruct((bsz * n_ctx, 1024), BF16),
        grid=(bsz,),
        in_specs=[pl.BlockSpec((n_ctx, 1024), lambda b: (b, 0)), pl.BlockSpec((n_ctx, 256), lambda b: (b, 0)),
                  pl.BlockSpec((n_ctx, 256), lambda b: (b, 0)), _full((1, SWA_HEADS))],
        out_specs=pl.BlockSpec((n_ctx, 1024), lambda b: (b, 0)),
        compiler_params=_params("parallel"),
        name="swa_ctx_attention",
    )(qc, kc, vc, sink.reshape(1, SWA_HEADS))


def _lru_coef_kernel(x_ref, xp_ref, xn_ref, cw_ref, cb_ref, wh_ref, wl_ref, bg_ref, lam_ref,
                     af_ref, bf_ref, ab_ref, bb_ref, *, tm, seq_tiles):
    i = pl.program_id(0) % seq_tiles
    first = i == 0
    last = i == seq_tiles - 1
    x = x_ref[...]
    row = lax.broadcasted_iota(jnp.int32, x.shape, 0)
    p6 = jnp.where(first, 0.0, xp_ref[6:7, :])
    p7 = jnp.where(first, 0.0, xp_ref[7:8, :])
    n0 = jnp.where(last, 0.0, xn_ref[0:1, :])
    xm2, xm1, xp1 = _shift_rows(x, row, tm, (p6, p7), n0)
    u = (cw_ref[0:1, :] * xm2 + cw_ref[1:2, :] * xm1 + cw_ref[2:3, :] * x + cw_ref[3:4, :] * xp1) + cb_ref[...]
    sp = _softplus(-lam_ref[...])
    outs = ((af_ref, bf_ref), (ab_ref, bb_ref))
    for n in range(LRU_BLOCKS):
        sl = slice(128 * n, 128 * (n + 1))
        un = u[:, sl]
        z = _mm3w(un, wh_ref[n], wl_ref[n]) + bg_ref[n]
        for d in range(2):
            r = jax.nn.sigmoid(z[:, 256 * d:256 * d + 128])
            gi = jax.nn.sigmoid(z[:, 256 * d + 128:256 * d + 256])
            log_a = (-LRU_C) * r * sp[d:d + 1, sl]
            a_ref, b_ref = outs[d]
            a_ref[:, sl] = jnp.exp(log_a)
            b_ref[:, sl] = jnp.sqrt(1.0 - jnp.exp(2.0 * log_a)) * (gi * un)


def _lru_coef(p, t_len, conv_w, conv_b, wg_hi, wg_lo, bg, lam, tm=256):
    m = p.shape[0]
    tm = min(tm, t_len)
    seq_tiles = t_len // tm
    out = jax.ShapeDtypeStruct((m, 1024), F32)
    ospec = pl.BlockSpec((tm, 1024), lambda i: (i, 0))
    return pl.pallas_call(
        functools.partial(_lru_coef_kernel, tm=tm, seq_tiles=seq_tiles),
        out_shape=(out, out, out, out),
        grid=(m // tm,),
        in_specs=_halo_specs(OFF_BX // 1024, 1024, tm, m) + [
            _full((4, 1024)), _full((1, 1024)), _full((8, 128, 512)), _full((8, 128, 512)),
            _full((8, 1, 512)), _full((2, 1024))],
        out_specs=(ospec, ospec, ospec, ospec),
        compiler_params=_params("parallel"),
        name="lru_coefficients",
    )(p, p, p, conv_w, conv_b, wg_hi, wg_lo, bg, lam)


def _lru_scan_kernel(*refs, reverse, final, nc, tc):
    if final:
        a_ref, b_ref, h0_ref, hf_ref, gate_ref, o_ref, hl_ref, hs_ref, hrows_ref = refs
    else:
        a_ref, b_ref, h0_ref, o_ref, hl_ref, hs_ref = refs
        hrows_ref = o_ref
    c = pl.program_id(1)

    @pl.when(c == 0)
    def _():
        hs_ref[...] = h0_ref[...]

    def body(t, h):
        tt = (tc - 1 - t) if reverse else t
        h = a_ref[pl.ds(tt, 1), :] * h + b_ref[pl.ds(tt, 1), :]
        hrows_ref[pl.ds(tt, 1), :] = h
        return h

    h = lax.fori_loop(0, tc, body, hs_ref[...], unroll=8)
    hs_ref[...] = h

    @pl.when(c == (0 if reverse else nc - 1))
    def _():
        hl_ref[...] = hrows_ref[tc - 1:tc, :]

    if final:
        o_ref[...] = ((hf_ref[...] + hrows_ref[...]) * _gelu_tanh(gate_ref[...])).astype(BF16)


def _lru_scan(a, b, h0, bsz, t_len, *, reverse, h_other=None, p=None, tc=256):
    tc = min(tc, t_len)
    nc = t_len // tc
    final = h_other is not None

    def row_idx(bi, c):
        return bi * nc + ((nc - 1 - c) if reverse else c)

    rows = pl.BlockSpec((tc, 1024), lambda bi, c: (row_idx(bi, c), 0))
    state = pl.BlockSpec((None, 1, 1024), lambda bi, c: (bi, 0, 0))
    in_specs = [rows, rows, state]
    args = [a, b, h0]
    scratch = [pltpu.VMEM((1, 1024), F32)]
    if final:
        in_specs += [rows, pl.BlockSpec((tc, 1024), lambda bi, c: (row_idx(bi, c), OFF_BG // 1024))]
        args += [h_other, p]
        scratch.append(pltpu.VMEM((tc, 1024), F32))
    return pl.pallas_call(
        functools.partial(_lru_scan_kernel, reverse=reverse, final=final, nc=nc, tc=tc),
        out_shape=(jax.ShapeDtypeStruct((bsz * t_len, 1024), BF16 if final else F32),
                   jax.ShapeDtypeStruct((bsz, 1, 1024), F32)),
        grid=(bsz, nc),
        in_specs=in_specs,
        out_specs=(rows, state),
        scratch_shapes=scratch,
        compiler_params=_params("parallel", "arbitrary"),
        name="lru_scan_" + ("bwd_out" if final else ("bwd" if reverse else "fwd")),
    )(*args)


def _mla_proj_kernel(*refs, rotate):
    if rotate:
        (qa_ref, kva_ref, kr_ref, qn_ref, kvn_ref, wq_ref, wkv_ref, cos_ref, sa_ref, sb_ref,
         q_ref, k_ref, v_ref) = refs
        cos, sa, sb = cos_ref[...], sa_ref[...], sb_ref[...]
    else:
        qa_ref, kva_ref, kr_ref, qn_ref, kvn_ref, wq_ref, wkv_ref, q_ref, k_ref, v_ref = refs

    def rms(x, g):
        return (x * lax.rsqrt(jnp.mean(x * x, -1, keepdims=True) + RMS_EPS) * g).astype(BF16)

    scale = np.float32((MLA_NOPE + MLA_ROPE) ** -0.5 * np.log2(np.e))
    q = _mm(rms(qa_ref[...], qn_ref[...]), wq_ref[...]) * scale
    kv = _mm(rms(kva_ref[...], kvn_ref[...]), wkv_ref[...])
    kr = kr_ref[...]
    if rotate:
        kr = _rope(kr, cos, sa, sb, 16)
    kr = kr.astype(BF16)
    lane = lax.broadcasted_iota(jnp.int32, kr.shape, 1)
    ones_col = jnp.where(lane == 0, 1.0, 0.0).astype(BF16)
    for h in range(MLA_HEADS):
        lo = MLA_HEAD_PAD * h
        q_ref[:, lo:lo + 128] = q[:, lo:lo + 128].astype(BF16)
        qr = q[:, lo + 128:lo + 256]
        if rotate:
            qr = _rope(qr, cos, sa, sb, 16)
        q_ref[:, lo + 128:lo + 256] = qr.astype(BF16)
        k_ref[:, lo:lo + 128] = kv[:, 128 * h:128 * (h + 1)].astype(BF16)
        k_ref[:, lo + 128:lo + 256] = kr
        v_ref[:, lo:lo + 128] = kv[:, 1024 + 128 * h:1024 + 128 * (h + 1)].astype(BF16)
        v_ref[:, lo + 128:lo + 256] = ones_col


def _mla_proj(p, t_len, q_norm, kv_norm, wq, wkv, tables, tm=256):
    m = p.shape[0]
    tm = min(tm, t_len)
    seq_tiles = t_len // tm
    rotate = tables is not None
    in_specs = [pl.BlockSpec((tm, 1024), lambda i: (i, OFF_CQ // 1024)),
                pl.BlockSpec((tm, 512), lambda i: (i, OFF_CKV // 512)),
                pl.BlockSpec((tm, 128), lambda i: (i, OFF_CR // 128)),
                _full((1, 1024)), _full((1, 512)), _full((1024, 2048)), _full((512, 2048))]
    args = [p, p, p, q_norm.reshape(1, -1), kv_norm.reshape(1, -1), wq, wkv]
    if rotate:
        in_specs += [pl.BlockSpec((tm, 128), lambda i: (i % seq_tiles, 0))] * 3
        args += list(tables)
    wide = pl.BlockSpec((tm, 2048), lambda i: (i, 0))
    return pl.pallas_call(
        functools.partial(_mla_proj_kernel, rotate=rotate),
        out_shape=(jax.ShapeDtypeStruct((m, 2048), BF16),) * 3,
        grid=(m // tm,),
        in_specs=in_specs,
        out_specs=(wide, wide, wide),
        compiler_params=_params("parallel"),
        name="mla_project_rope" if rotate else "mla_project",
    )(*args)


def _mla_flash_kernel(*refs, with_lat, n_chunks, ck):
    if with_lat:
        q_ref, kc_ref, vc_ref, k_ref, v_ref, o_ref = refs
    else:
        q_ref, kc_ref, vc_ref, o_ref = refs
    q = q_ref[...]
    chunks = [(kc_ref, vc_ref, slice(None))]
    if with_lat:
        chunks += [(k_ref, v_ref, slice(j * ck, (j + 1) * ck)) for j in range(n_chunks)]
    k0, _, sl0 = chunks[0]
    s_cur = _mm_nt(q, k0[sl0, :])
    m = acc = pending = s_next = None
    for j, (_, vj_ref, slj) in enumerate(chunks):
        if j + 1 < len(chunks):
            kn_ref, _, sln = chunks[j + 1]
            s_next = _mm_nt(q, kn_ref[sln, :])
        if pending is not None:
            p_prev, v_prev, alpha_prev = pending
            pv = _mm(p_prev, v_prev)
            acc = pv if acc is None else alpha_prev * acc + pv
        mx = jnp.max(s_cur, -1, keepdims=True)
        m_new = mx if m is None else jnp.maximum(m, mx)
        alpha = None if m is None else jnp.exp2(m - m_new)
        pending = (jnp.exp2((s_cur - m_new).astype(BF16)), vj_ref[slj, :], alpha)
        m = m_new
        s_cur = s_next
    p_prev, v_prev, alpha_prev = pending
    pv = _mm(p_prev, v_prev)
    acc = pv if acc is None else alpha_prev * acc + pv
    o_ref[...] = (acc[:, 0:MLA_V] / acc[:, MLA_V:MLA_V + 1]).astype(BF16)


def _mla_flash(q, kc, vc, bsz, tq_len, n_ctx, k=None, v=None, tq=512, ck=512):
    with_lat = k is not None
    tq = min(tq, tq_len)
    nq = tq_len // tq
    in_specs = [pl.BlockSpec((tq, MLA_HEAD_PAD), lambda b, h, i: (b * nq + i, h)),
                pl.BlockSpec((n_ctx, MLA_HEAD_PAD), lambda b, h, i: (b, h)),
                pl.BlockSpec((n_ctx, MLA_HEAD_PAD), lambda b, h, i: (b, h))]
    args = [q, kc, vc]
    n_chunks = 0
    if with_lat:
        t_len = k.shape[0] // bsz
        ck = min(ck, t_len)
        n_chunks = t_len // ck
        in_specs += [pl.BlockSpec((t_len, MLA_HEAD_PAD), lambda b, h, i: (b, h)),
                     pl.BlockSpec((t_len, MLA_HEAD_PAD), lambda b, h, i: (b, h))]
        args += [k, v]
    return pl.pallas_call(
        functools.partial(_mla_flash_kernel, with_lat=with_lat, n_chunks=n_chunks, ck=ck),
        out_shape=jax.ShapeDtypeStruct((bsz * tq_len, 1024), BF16),
        grid=(bsz, MLA_HEADS, nq),
        in_specs=in_specs,
        out_specs=pl.BlockSpec((tq, MLA_V), lambda b, h, i: (b * nq + i, h)),
        compiler_params=_params("parallel", "parallel", "parallel"),
        name="mla_flash" if with_lat else "mla_ctx_attention",
    )(*args)


def _rwkv_feat_kernel(xr_ref, xrp_ref, xrn_ref, xk_ref, xkp_ref, xkn_ref, xv_ref, xvp_ref, xvn_ref,
                      xl_ref, xlp_ref, xln_ref, mup_ref, mun_ref, w0_ref, a0_ref,
                      w2h_ref, w2l_ref, a2h_ref, a2l_ref, g2h_ref, g2l_ref, kk_w_ref, ka_ref, rk_ref, bd_ref,
                      r_o, v_o, kk_o, g_o, bonus_o, lwf_o, kf_o, bf_o, lwb_o, kb_o, bb_o, *, tm, seq_tiles):
    i = pl.program_id(0) % seq_tiles
    first = i == 0
    last = i == seq_tiles - 1

    def shifted(x_ref, xp_ref, xn_ref, lo, hi):
        x = x_ref[...]
        row = lax.broadcasted_iota(jnp.int32, x.shape, 0)
        p7 = jnp.where(first, 0.0, xp_ref[7:8, :])
        n0 = jnp.where(last, 0.0, xn_ref[0:1, :])
        _, xm1, xp1 = _shift_rows(x, row, tm, (None, p7), n0)
        return x + mup_ref[:, lo:hi] * (xm1 - x) + mun_ref[:, lo:hi] * (xp1 - x)

    r = shifted(xr_ref, xrp_ref, xrn_ref, 0, 1024)
    k = shifted(xk_ref, xkp_ref, xkn_ref, 1024, 2048)
    v = shifted(xv_ref, xvp_ref, xvn_ref, 2048, 3072)
    lo = shifted(xl_ref, xlp_ref, xln_ref, 3072, 3584)
    bd = bd_ref[...]

    kkr = k * kk_w_ref[...]
    kk = kkr / jnp.maximum(jnp.sqrt(_seg_sum(kkr * kkr, bd)), 1e-12)
    g = _mm3w(jax.nn.sigmoid(lo[:, 256:512]), g2h_ref[...], g2l_ref[...])
    wl = w0_ref[...] + _mm3w(jnp.tanh(lo[:, 0:128]), w2h_ref[...], w2l_ref[...])
    lw = np.float32(-np.exp(-0.5)) * jax.nn.sigmoid(wl)
    a = jax.nn.sigmoid(a0_ref[...] + _mm3w(lo[:, 128:256], a2h_ref[...], a2l_ref[...]))
    r_o[...] = r
    v_o[...] = v
    kk_o[...] = kk
    g_o[...] = g
    bonus = None
    for d, (lw_o, k_o, b_o) in enumerate(((lwf_o, kf_o, bf_o), (lwb_o, kb_o, bb_o))):
        a_d = a[:, 1024 * d:1024 * (d + 1)]
        k_d = k * (1.0 + (a_d - 1.0) * ka_ref[...])
        bo = _seg_sum(r * k_d * rk_ref[...], bd) * v
        bonus = bo if bonus is None else bonus + bo
        lw_o[...] = lw[:, 1024 * d:1024 * (d + 1)]
        k_o[...] = k_d
        b_o[...] = kk * a_d
    bonus_o[...] = bonus


def _rwkv_feat(p, t_len, wts, tm=256):
    m = p.shape[0]
    tm = min(tm, t_len)
    seq_tiles = t_len // tm
    in_specs = (_halo_specs(OFF_DR // 1024, 1024, tm, m) + _halo_specs(OFF_DK // 1024, 1024, tm, m)
                + _halo_specs(OFF_DV // 1024, 1024, tm, m) + _halo_specs(OFF_DL // 512, 512, tm, m)
                + [_full(w.shape) for w in wts])
    out = jax.ShapeDtypeStruct((m, 1024), F32)
    ospec = pl.BlockSpec((tm, 1024), lambda i: (i, 0))
    return pl.pallas_call(
        functools.partial(_rwkv_feat_kernel, tm=tm, seq_tiles=seq_tiles),
        out_shape=(out,) * 11,
        grid=(m // tm,),
        in_specs=in_specs,
        out_specs=(ospec,) * 11,
        compiler_params=_params("parallel"),
        name="rwkv_features",
    )(*([p] * 12), *wts)


def _wkv_operands(r_ref, v_ref, kk_ref, lw_ref, k_ref, b_ref, incl, reverse):
    cl = WKV_CHUNK
    tri = jnp.where(incl, 1.0, 0.0).astype(BF16)
    lw = lw_ref[...]
    l1 = lw.astype(BF16)
    rem = lw - l1.astype(F32)
    l2 = rem.astype(BF16)
    l3 = (rem - l2.astype(F32)).astype(BF16)
    cum = _mm(tri, l1) + (_mm(tri, l2) + _mm(tri, l3))
    total = cum[0:1, :] if reverse else cum[cl - 1:cl, :]
    e_k = jnp.exp(-cum)
    e_t = jnp.exp(total - cum)
    k_raw = k_ref[...]
    b_raw = b_ref[...]
    return dict(d_c=jnp.exp(total),
                r_t=(r_ref[...] * jnp.exp(cum)).astype(BF16),
                kk_t=(kk_ref[...] * jnp.exp(cum - lw)).astype(BF16),
                k_t=(k_raw * e_k).astype(BF16), b_t=(b_raw * e_k).astype(BF16),
                k_d=(k_raw * e_t).astype(BF16), b_d=(b_raw * e_t).astype(BF16),
                v=v_ref[...].astype(BF16))


def _wkv_kernel(rf_ref, vf_ref, kkf_ref, lwf_ref, kf_ref, bf_ref, rb_ref, vb_ref, kkb_ref, lwb_ref, kb_ref, bb_ref,
                s0f_ref, s0b_ref, yf_ref, yb_ref, soutf_ref, soutb_ref, sf_ref, sb_ref, *, nc):
    c = pl.program_id(1)
    cl = WKV_CHUNK

    @pl.when(c == 0)
    def _():
        sf_ref[...] = s0f_ref[...]
        sb_ref[...] = s0b_ref[...]

    ti = lax.broadcasted_iota(jnp.int32, (cl, cl), 0)
    si = lax.broadcasted_iota(jnp.int32, (cl, cl), 1)
    blk = (ti // WKV_INV_BLOCK) == (si // WKV_INV_BLOCK)
    masks = ((si <= ti, si < ti), (si >= ti, si > ti))
    ops = (_wkv_operands(rf_ref, vf_ref, kkf_ref, lwf_ref, kf_ref, bf_ref, masks[0][0], False),
           _wkv_operands(rb_ref, vb_ref, kkb_ref, lwb_ref, kb_ref, bb_ref, masks[1][0], True))
    s_all = (sf_ref[...], sb_ref[...])

    chains = [(d, slice(RWKV_HEAD * h, RWKV_HEAD * (h + 1))) for d in range(2) for h in range(RWKV_HEADS)]
    heads = range(len(chains))
    incl = [masks[d][0] for d, _ in chains]
    strict = [masks[d][1] for d, _ in chains]

    def lanes(name):
        return [ops[d][name][:, sl] for d, sl in chains]

    v_h, k_dh, b_dh, d_ch = lanes("v"), lanes("k_d"), lanes("b_d"), lanes("d_c")
    s0 = [s_all[d][sl, :] for d, sl in chains]
    left = [jnp.concatenate([kk, r], axis=0) for kk, r in zip(lanes("kk_t"), lanes("r_t"))]
    right = [jnp.concatenate([k, b], axis=0) for k, b in zip(lanes("k_t"), lanes("b_t"))]
    a = [_mm_nt(left[h], right[h]) for h in heads]
    ls = [_mm_nt(left[h], s0[h].astype(BF16)) for h in heads]
    a_l = [jnp.where(strict[h], a[h][0:cl, cl:2 * cl], 0.0) for h in heads]
    a_v = [jnp.concatenate([jnp.where(strict[h], a[h][0:cl, 0:cl], 0.0),
                            jnp.where(incl[h], a[h][cl:2 * cl, 0:cl], 0.0)], axis=0).astype(BF16) for h in heads]
    a_rb = [jnp.where(incl[h], a[h][cl:2 * cl, cl:2 * cl], 0.0).astype(BF16) for h in heads]
    av = [_mm(a_v[h], v_h[h]) for h in heads]
    dg = [jnp.where(blk, a_l[h], 0.0) for h in heads]
    off = [a_l[h] - dg[h] for h in heads]
    d2 = [_mm1(dg[h], dg[h]) for h in heads]
    d4 = [_mm1(d2[h], d2[h]) for h in heads]
    x = [d2[h] - dg[h] - _mm1(dg[h], d2[h]) for h in heads]
    d8 = [_mm1(d4[h], d4[h]) for h in heads]
    x = [x[h] + d4[h] + _mm1(x[h], d4[h]) for h in heads]
    t16 = [x[h] + d8[h] + _mm1(x[h], d8[h]) for h in heads]
    n1 = [off[h] + _mm1(t16[h], off[h]) for h in heads]
    n2 = [_mm1(n1[h], n1[h]) for h in heads]
    y1 = [t16[h] - n1[h] - _mm1(n1[h], t16[h]) for h in heads]
    t_m = [y1[h] + n2[h] + _mm1(n2[h], y1[h]) for h in heads]
    rhs = [ls[h][0:cl] + av[h][0:cl] for h in heads]
    u = [rhs[h] + _mm1(t_m[h], rhs[h]) for h in heads]
    vu = [jnp.concatenate([v_h[h], (-u[h]).astype(BF16)], axis=0) for h in heads]
    kb = [jnp.concatenate([k_dh[h], b_dh[h]], axis=0) for h in heads]
    states = [s0[h] * d_ch[h] + _mm_tn(vu[h], kb[h]) for h in heads]
    ys = [ls[h][cl:2 * cl] + av[h][cl:2 * cl] - _mm(a_rb[h], u[h].astype(BF16)) for h in heads]
    yf_ref[...] = jnp.concatenate(ys[:RWKV_HEADS], axis=1)
    yb_ref[...] = jnp.concatenate(ys[RWKV_HEADS:], axis=1)
    sf_ref[...] = jnp.concatenate(states[:RWKV_HEADS], axis=0)
    sb_ref[...] = jnp.concatenate(states[RWKV_HEADS:], axis=0)

    @pl.when(c == nc - 1)
    def _():
        soutf_ref[...] = sf_ref[...]
        soutb_ref[...] = sb_ref[...]


def _wkv_scan(feats, s0_f, s0_b, bsz, t_len):
    r, v, kk, _, _, lw_f, k_f, b_f, lw_b, k_b, b_b = feats
    nc = t_len // WKV_CHUNK
    rows_f = pl.BlockSpec((WKV_CHUNK, GROUP_W), lambda bi, c: (bi * nc + c, 0))
    rows_b = pl.BlockSpec((WKV_CHUNK, GROUP_W), lambda bi, c: (bi * nc + nc - 1 - c, 0))
    state = pl.BlockSpec((None, GROUP_W, RWKV_HEAD), lambda bi, c: (bi, 0, 0))
    y_shape = jax.ShapeDtypeStruct((bsz * t_len, GROUP_W), F32)
    s_shape = jax.ShapeDtypeStruct((bsz, GROUP_W, RWKV_HEAD), F32)
    return pl.pallas_call(
        functools.partial(_wkv_kernel, nc=nc),
        out_shape=(y_shape, y_shape, s_shape, s_shape),
        grid=(bsz, nc),
        in_specs=[rows_f] * 6 + [rows_b] * 6 + [state, state],
        out_specs=(rows_f, rows_b, state, state),
        scratch_shapes=[pltpu.VMEM((GROUP_W, RWKV_HEAD), F32)] * 2,
        compiler_params=_params("parallel", "arbitrary"),
        name="wkv7_chunked",
    )(r, v, kk, lw_f, k_f, b_f, r, v, kk, lw_b, k_b, b_b, s0_f, s0_b)


def _rwkv_out_kernel(yf_ref, yb_ref, bonus_ref, g_ref, lng_ref, lnb_ref, bd_ref, o_ref):
    bd = bd_ref[...]
    y = yf_ref[...] + yb_ref[...]
    inv_n = np.float32(1.0 / RWKV_HEAD)
    yc = y - _seg_sum(y, bd) * inv_n
    var = _seg_sum(yc * yc, bd) * inv_n
    yn = yc * lax.rsqrt(var + RWKV_GN_EPS) * lng_ref[...] + lnb_ref[...]
    o_ref[...] = ((yn + bonus_ref[...]) * g_ref[...]).astype(BF16)


def _rwkv_out(y_f, y_b, bonus, g, ln_g, ln_b, bd, t_len, tm=256):
    m = y_f.shape[0]
    tm = min(tm, t_len)
    rows = pl.BlockSpec((tm, 1024), lambda i: (i, 0))
    return pl.pallas_call(
        _rwkv_out_kernel,
        out_shape=jax.ShapeDtypeStruct((m, 1024), BF16),
        grid=(m // tm,),
        in_specs=[rows] * 4 + [_full((1, 1024)), _full((1, 1024)), _full((128, 128))],
        out_specs=rows,
        compiler_params=_params("parallel"),
        name="rwkv_groupnorm_gate",
    )(y_f, y_b, bonus, g, ln_g.reshape(1, -1), ln_b.reshape(1, -1), bd)


def _w_in_layout_kernel(w_ref, o_ref):
    dst = 0
    for seg in _PROJ_SEGMENTS:
        if isinstance(seg, tuple):
            width = seg[1] - seg[0]
            o_ref[:, dst:dst + width] = w_ref[:, seg[0]:seg[1]].astype(BF16)
        else:
            width = seg
            o_ref[:, dst:dst + width] = jnp.zeros((o_ref.shape[0], width), BF16)
        dst += width


def _permute_w_in(w, tr=256):
    rows, cols = w.shape
    return pl.pallas_call(
        _w_in_layout_kernel,
        out_shape=jax.ShapeDtypeStruct((rows, N_PROJ), BF16),
        grid=(rows // tr,),
        in_specs=[pl.BlockSpec((tr, cols), lambda i: (i, 0))],
        out_specs=pl.BlockSpec((tr, N_PROJ), lambda i: (i, 0)),
        compiler_params=_params("parallel"),
        name="w_in_layout",
    )(w)


def _permute_w_qb(w):
    w = w.reshape(w.shape[0], MLA_HEADS, MLA_NOPE + MLA_ROPE)
    w = jnp.pad(w, ((0, 0), (0, 0), (0, MLA_HEAD_PAD - MLA_NOPE - MLA_ROPE)))
    return w.reshape(w.shape[0], MLA_HEADS * MLA_HEAD_PAD).astype(BF16)


def _permute_w_kvb(w):
    w = w.reshape(w.shape[0], MLA_HEADS, MLA_NOPE + MLA_V)
    return jnp.concatenate([w[:, :, :MLA_NOPE].reshape(w.shape[0], -1),
                            w[:, :, MLA_NOPE:].reshape(w.shape[0], -1)], axis=1).astype(BF16)


def _rope_tables(pos_row, pos_col, half):
    inv = ROPE_BASE ** (-jnp.arange(half, dtype=F32) / half)
    zeros = jnp.zeros((pos_row.shape[0], half), F32)
    cos, sin_a, sin_b = [], [], []
    for pos in (pos_row, pos_col):
        ang = pos.astype(F32)[:, None] * inv[None, :]
        c, s = jnp.cos(ang), jnp.sin(ang)
        cos += [c, c]
        sin_a += [-s, zeros]
        sin_b += [zeros, s]
    pad = jnp.zeros((pos_row.shape[0], 128 - 4 * half), F32)
    return tuple(jnp.concatenate(t + [pad], axis=1) for t in (cos, sin_a, sin_b))


def _block_diag_ones():
    i = np.arange(128)
    return jnp.asarray((i[:, None] // RWKV_HEAD) == (i[None, :] // RWKV_HEAD), BF16)


def _split_w(w):
    hi = w.astype(BF16)
    return hi, (w - hi.astype(F32)).astype(BF16)


def _lru_gate_weights(wa, ba, wi, bi):
    w = jnp.concatenate([wa[0], wi[0], wa[1], wi[1]], axis=-1)
    b = jnp.concatenate([t.reshape(LRU_BLOCKS, 1, LRU_BLOCK_W) for t in (ba[0], bi[0], ba[1], bi[1])], axis=-1)
    return _split_w(w) + (b,)


def _rwkv_weights(mu_prev, mu_next, w0, w2, a0, a2, g2, k_k, k_a, r_k, bd):
    def pad_mu(mu):
        return jnp.pad(mu, (0, 3584 - mu.shape[0])).reshape(1, 3584)

    def two_dir(w):
        z = jnp.zeros_like(w[0])
        return jnp.concatenate([jnp.concatenate([w[0], z], axis=1), jnp.concatenate([z, w[1]], axis=1)], axis=0)

    g2p = jnp.pad(g2, ((0, 256 - RWKV_GATE_LORA), (0, 0)))
    return (pad_mu(mu_prev), pad_mu(mu_next), w0.reshape(1, 2048), a0.reshape(1, 2048),
            *_split_w(two_dir(w2)), *_split_w(two_dir(a2)), *_split_w(g2p),
            k_k.reshape(1, 1024), k_a.reshape(1, 1024), r_k.reshape(1, 1024), bd)


def kernel(x, c, ctx, c_ctx, w_mod, b_mod, w_in, w_out, ln1_g, ln1_b, w_ff1, w_ff2, ln2_g, ln2_b, swa_sink, lru_conv_w, lru_conv_b, lru_wa, lru_ba, lru_wi, lru_bi, lru_lam, mla_q_norm, mla_kv_norm, mla_w_qb, mla_w_kvb, rwkv_mu_prev, rwkv_mu_next, rwkv_w0, rwkv_w2, rwkv_a0, rwkv_a2, rwkv_g2, rwkv_k_k, rwkv_k_a, rwkv_r_k, rwkv_ln_g, rwkv_ln_b):
    bsz, t_len, d = x.shape
    n_ctx = ctx.shape[1]
    depth = w_mod.shape[0]

    cc = jnp.zeros((8, d), F32).at[:bsz].set(c).at[bsz].set(c_ctx)
    mod = _modulation(cc, w_mod, b_mod)

    pos = jnp.arange(t_len, dtype=jnp.int32)
    row, col = pos // GRID_W, pos % GRID_W
    swa_tables = _rope_tables(row, col, SWA_HEAD_DIM // 4)
    mla_tables = _rope_tables(row, col, MLA_ROPE // 4)
    bd = _block_diag_ones()
    zeros_h = jnp.zeros((bsz, 1, GROUP_W), F32)
    zeros_s = jnp.zeros((bsz, RWKV_HEADS * RWKV_HEAD, RWKV_HEAD), F32)

    xl = x.reshape(bsz * t_len, d)
    xc = ctx.reshape(bsz * n_ctx, d)

    for l in range(depth):
        with_ctx = l < depth - 1
        chunks = [mod[l, :, k * d:(k + 1) * d] for k in range(6)]
        lat = [m[:bsz][:, None, :] for m in chunks]
        cxm = [m[bsz:bsz + 1][:, None, :] for m in chunks]

        w_in_p = _permute_w_in(w_in[l])
        p = _ln_mod_matmul(xl, lat[0], lat[1], w_in_p, act=None, out_dtype=F32, tm=1024, tn=512)
        pc = _ln_mod_matmul(xc, cxm[0], cxm[1], w_in_p, act=None, out_dtype=F32)

        q_a, k_a, v_a = _swa_prep(p, t_len, swa_tables)
        qc_a, kc_a, vc_a = _swa_prep(pc, n_ctx, None)
        o_swa = _swa_attn(q_a, k_a, v_a, kc_a, vc_a, swa_sink[l], bsz, t_len, n_ctx)

        gate_w = _lru_gate_weights(lru_wa[l], lru_ba[l], lru_wi[l], lru_bi[l])
        lru_args = (lru_conv_w[l], lru_conv_b[l].reshape(1, -1), *gate_w, lru_lam[l])
        caf, cbf, cab, cbb = _lru_coef(pc, n_ctx, *lru_args)
        hc_f, hlast_f = _lru_scan(caf, cbf, zeros_h, bsz, n_ctx, reverse=False)
        oc_lru, hlast_b = _lru_scan(cab, cbb, zeros_h, bsz, n_ctx, reverse=True, h_other=hc_f, p=pc)
        laf, lbf, lab, lbb = _lru_coef(p, t_len, *lru_args)
        h_f, _ = _lru_scan(laf, lbf, hlast_f, bsz, t_len, reverse=False)
        o_lru, _ = _lru_scan(lab, lbb, hlast_b, bsz, t_len, reverse=True, h_other=h_f, p=p)

        wq = _permute_w_qb(mla_w_qb[l])
        wkv = _permute_w_kvb(mla_w_kvb[l])
        q_c, k_c, v_c = _mla_proj(p, t_len, mla_q_norm[l], mla_kv_norm[l], wq, wkv, mla_tables)
        qc_c, kc_c, vc_c = _mla_proj(pc, n_ctx, mla_q_norm[l], mla_kv_norm[l], wq, wkv, None)
        o_mla = _mla_flash(q_c, kc_c, vc_c, bsz, t_len, n_ctx, k=k_c, v=v_c)

        rw = _rwkv_weights(rwkv_mu_prev[l], rwkv_mu_next[l], rwkv_w0[l], rwkv_w2[l], rwkv_a0[l], rwkv_a2[l],
                           rwkv_g2[l], rwkv_k_k[l], rwkv_k_a[l], rwkv_r_k[l], bd)
        fc = _rwkv_feat(pc, n_ctx, rw)
        yc_f, yc_b, s_f, s_b = _wkv_scan(fc, zeros_s, zeros_s, bsz, n_ctx)
        fl = _rwkv_feat(p, t_len, rw)
        y_f, y_b, _, _ = _wkv_scan(fl, s_f, s_b, bsz, t_len)
        o_rwkv = _rwkv_out(y_f, y_b, fl[4], fl[3], rwkv_ln_g[l], rwkv_ln_b[l], bd, t_len)

        w_out_b = w_out[l].astype(BF16)
        w_ff1_b = w_ff1[l].astype(BF16)
        w_ff2_b = w_ff2[l].astype(BF16)

        def tail(xin, mix, m):
            x1 = _matmul_res_ln(mix, w_out_b, xin, m[2], ln1_g[l], ln1_b[l], tm=512)
            hid = _ln_mod_matmul(x1, m[3], m[4], w_ff1_b, act="relu2", out_dtype=BF16, tm=1024)
            return _matmul_res_ln(hid, w_ff2_b, x1, m[5], ln2_g[l], ln2_b[l])

        if with_ctx:
            oc_swa = _swa_ctx_attn(qc_a, kc_a, vc_a, swa_sink[l], bsz, n_ctx)
            oc_mla = _mla_flash(qc_c, kc_c, vc_c, bsz, n_ctx, n_ctx)
            oc_rwkv = _rwkv_out(yc_f, yc_b, fc[4], fc[3], rwkv_ln_g[l], rwkv_ln_b[l], bd, n_ctx)
            xc = tail(xc, jnp.concatenate([oc_swa, oc_lru, oc_mla, oc_rwkv], axis=-1), cxm)

        xl = tail(xl, jnp.concatenate([o_swa, o_lru, o_mla, o_rwkv], axis=-1), lat)

    return xl.reshape(bsz, t_len, d)
```

```python
import functools

import numpy as np
import jax
import jax.numpy as jnp
from jax import lax
from jax.experimental import pallas as pl
from jax.experimental.pallas import tpu as pltpu

F32 = jnp.float32
BF16 = jnp.bfloat16

D_MODEL = 4096
GRID_W = 64
GROUP_W = 1024
D_FF = 4 * D_MODEL
BLOCK = 128

SWA_HEAD_DIM = 128
SWA_HEADS = 8
SWA_KV_HEADS = 2
SWA_GROUP = 4

LRU_BLOCKS = 8
LRU_BLOCK_W = 128
LRU_C = 8.0

MLA_HEADS = 8
MLA_NOPE = 128
MLA_ROPE = 64
MLA_V = 128
MLA_HEAD_PAD = 256

RWKV_HEAD = 64
RWKV_HEADS = 16
RWKV_GATE_LORA = 160
RWKV_GN_EPS = 64e-5
WKV_CHUNK = 64
WKV_INV_BLOCK = 16

ROPE_BASE = 10000.0
LN_EPS = 1e-5
RMS_EPS = 1e-6
DEPTH = 2
ALPHA = (2 * DEPTH) ** 0.25

OFF_AQ, OFF_BX, OFF_BG, OFF_CQ, OFF_DR, OFF_DK, OFF_DV = 0, 1024, 2048, 3072, 4096, 5120, 6144
OFF_CKV, OFF_AK, OFF_AV, OFF_DL, OFF_CR = 7168, 7680, 7936, 8192, 8704
N_PROJ = 9216
_PROJ_SEGMENTS = ((0, 1024), (1536, 2560), (2560, 3584), (3584, 4608), (5184, 6208), (6208, 7232),
                  (7232, 8256), (4608, 5120), (1024, 1280), (1280, 1536), (8256, 8672), 96,
                  (5120, 5184), 64, 384)

VMEM_LIMIT_V7X = 56 * 1024 * 1024


def _params(*sem):
    return pltpu.CompilerParams(dimension_semantics=sem, vmem_limit_bytes=VMEM_LIMIT_V7X)


def _split2(a):
    hi = a.astype(BF16)
    lo = (a - hi.astype(F32)).astype(BF16)
    return hi, lo


def _mm(a, b):
    return jnp.dot(a, b, preferred_element_type=F32)


def _mm_nt(a, b):
    return lax.dot_general(a, b, (((1,), (1,)), ((), ())), preferred_element_type=F32)


def _mm_tn(a, b):
    return lax.dot_general(a, b, (((0,), (0,)), ((), ())), preferred_element_type=F32)


def _mm3(a, b, mm=_mm):
    ah, al = _split2(a)
    bh, bl = _split2(b)
    return mm(ah, bh) + (mm(al, bh) + mm(ah, bl))


def _mm3w(a, bh, bl):
    ah, al = _split2(a)
    return _mm(ah, bh) + (_mm(al, bh) + _mm(ah, bl))


def _mm1(a, b, mm=_mm):
    return mm(a.astype(BF16), b.astype(BF16))


def _seg_sum(x, bd):
    parts = []
    for j in range(x.shape[1] // 128):
        hi, lo = _split2(x[:, 128 * j:128 * (j + 1)])
        parts.append(_mm(hi, bd) + _mm(lo, bd))
    return jnp.concatenate(parts, axis=1)


def _layer_norm_rows(x):
    mu = jnp.mean(x, -1, keepdims=True)
    xc = x - mu
    var = jnp.mean(xc * xc, -1, keepdims=True)
    return xc * lax.rsqrt(var + LN_EPS)


def _softplus(z):
    return jnp.maximum(z, 0.0) + jnp.log1p(jnp.exp(-jnp.abs(z)))


def _gelu_tanh(x):
    return 0.5 * x * (1.0 + jnp.tanh(np.sqrt(2.0 / np.pi).astype(np.float32) * (x + 0.044715 * (x * x * x))))


def _rope(x, cos, sin_a, sin_b, half):
    return x * cos + pltpu.roll(x, 128 - half, 1) * sin_a + pltpu.roll(x, half, 1) * sin_b


def _shift_rows(x, row, tm, prev_rows, next_row):
    p6, p7 = prev_rows
    xm1 = jnp.where(row == 0, p7, pltpu.roll(x, 1, 0))
    xp1 = jnp.where(row == tm - 1, next_row, pltpu.roll(x, tm - 1, 0))
    xm2 = None
    if p6 is not None:
        xm2 = jnp.where(row == 0, p6, jnp.where(row == 1, p7, pltpu.roll(x, 2, 0)))
    return xm2, xm1, xp1


def _halo_specs(col, width, tm, m_rows):
    r8 = tm // 8
    last8 = m_rows // 8 - 1
    return [pl.BlockSpec((tm, width), lambda i: (i, col)),
            pl.BlockSpec((8, width), lambda i: (jnp.maximum(i * r8 - 1, 0), col)),
            pl.BlockSpec((8, width), lambda i: (jnp.minimum((i + 1) * r8, last8), col))]


def _full(shape):
    nd = len(shape)
    return pl.BlockSpec(shape, lambda *_: (0,) * nd)


def _mod_kernel(c_ref, w_ref, b_ref, o_ref):
    cc = c_ref[...]
    a = cc * jax.nn.sigmoid(cc)
    o_ref[...] = _mm3(a, w_ref[...]) + b_ref[...]


def _modulation(cc, w_mod, b_mod, tn=512):
    depth, d, n = w_mod.shape
    return pl.pallas_call(
        _mod_kernel,
        out_shape=jax.ShapeDtypeStruct((depth, 8, n), F32),
        grid=(depth, n // tn),
        in_specs=[pl.BlockSpec((8, d), lambda l, j: (0, 0)),
                  pl.BlockSpec((None, d, tn), lambda l, j: (l, 0, j)),
                  pl.BlockSpec((None, 1, tn), lambda l, j: (l, 0, j))],
        out_specs=pl.BlockSpec((None, 8, tn), lambda l, j: (l, 0, j)),
        compiler_params=_params("parallel", "parallel"),
        name="adaln_modulation",
    )(cc, w_mod, b_mod.reshape(depth, 1, n))


def _lnmm_kernel(x_ref, sh_ref, sc_ref, w_ref, o_ref, xn_ref, *, act, tm):
    @pl.when(pl.program_id(1) == 0)
    def _():
        scale = 1.0 + sc_ref[...]
        shift = sh_ref[...]
        rows = min(tm, 128)

        def body(r, carry):
            sl = pl.ds(pl.multiple_of(r * rows, rows), rows)
            xn_ref[sl, :] = (_layer_norm_rows(x_ref[sl, :]) * scale + shift).astype(BF16)
            return carry

        lax.fori_loop(0, tm // rows, body, 0)

    acc = _mm(xn_ref[...], w_ref[...])
    if act == "relu2":
        acc = jnp.maximum(acc, 0.0)
        acc = acc * acc
    o_ref[...] = acc.astype(o_ref.dtype)


def _ln_mod_matmul(x, shift, scale, w, layer, *, act, out_dtype, tm=512, tn=1024):
    m, k = x.shape
    n = w.shape[2]
    nbm = shift.shape[0]
    tm = min(tm, m // nbm)
    seq_tiles = m // nbm // tm
    mod_spec = pl.BlockSpec((None, 1, k), lambda i, j: (i // seq_tiles, 0, 0))
    x_spec = pl.BlockSpec((tm, k), lambda i, j: (i, 0), pipeline_mode=pl.Buffered(1))
    return pl.pallas_call(
        functools.partial(_lnmm_kernel, act=act, tm=tm),
        out_shape=jax.ShapeDtypeStruct((m, n), out_dtype),
        grid=(m // tm, n // tn),
        in_specs=[x_spec, mod_spec, mod_spec,
                  pl.BlockSpec((None, k, tn), lambda i, j: (layer, 0, j))],
        out_specs=pl.BlockSpec((tm, tn), lambda i, j: (i, j)),
        scratch_shapes=[pltpu.VMEM((tm, k), BF16)],
        compiler_params=_params("parallel", "arbitrary"),
        name="ln_mod_matmul_" + (act or "id"),
    )(x, shift, scale, w)


def _mmln_kernel(a_ref, w_ref, x_ref, ga_ref, g_ref, b_ref, o_ref, *, nk, tm):
    k = pl.program_id(1)

    @pl.when(k == 0)
    def _():
        o_ref[...] = _mm(a_ref[...], w_ref[...])

    @pl.when(k > 0)
    def _():
        o_ref[...] += _mm(a_ref[...], w_ref[...])

    @pl.when(k == nk - 1)
    def _():
        gate = ga_ref[...]
        g = g_ref[...]
        b = b_ref[...]
        rows = min(tm, 128)

        def body(r, carry):
            sl = pl.ds(pl.multiple_of(r * rows, rows), rows)
            z = ALPHA * x_ref[sl, :] + gate * o_ref[sl, :]
            o_ref[sl, :] = _layer_norm_rows(z) * g + b
            return carry

        lax.fori_loop(0, tm // rows, body, 0)


def _matmul_res_ln(a, w, layer, xres, gate, g, b, *, tm=1024, tk=512):
    m, kdim = a.shape
    n = w.shape[2]
    nbm = gate.shape[0]
    tm = min(tm, m // nbm)
    seq_tiles = m // nbm // tm
    nk = kdim // tk
    mode = dict(pipeline_mode=pl.Buffered(1)) if nk >= 16 else {}
    return pl.pallas_call(
        functools.partial(_mmln_kernel, nk=nk, tm=tm),
        out_shape=jax.ShapeDtypeStruct((m, n), F32),
        grid=(m // tm, nk),
        in_specs=[pl.BlockSpec((tm, tk), lambda i, k: (i, k)),
                  pl.BlockSpec((None, tk, n), lambda i, k: (layer, k, 0)),
                  pl.BlockSpec((tm, n), lambda i, k: (i, 0), **mode),
                  pl.BlockSpec((None, 1, n), lambda i, k: (i // seq_tiles, 0, 0)),
                  _full((1, n)), _full((1, n))],
        out_specs=pl.BlockSpec((tm, n), lambda i, k: (i, 0), **mode),
        compiler_params=_params("parallel", "arbitrary"),
        name="matmul_res_ln",
    )(a, w, xres, gate, g.reshape(1, n), b.reshape(1, n))


def _swa_prep_kernel(*refs, rotate):
    if rotate:
        q_ref, kv_ref, cos_ref, sa_ref, sb_ref, qo_ref, ko_ref, vo_ref = refs
        cos, sa, sb = cos_ref[...], sa_ref[...], sb_ref[...]
    else:
        q_ref, kv_ref, qo_ref, ko_ref, vo_ref = refs
    scale = np.float32(SWA_HEAD_DIM ** -0.5)
    for h in range(SWA_HEADS):
        sl = slice(128 * h, 128 * (h + 1))
        q = q_ref[:, sl]
        if rotate:
            q = _rope(q, cos, sa, sb, 32)
        qo_ref[:, sl] = (q * scale).astype(BF16)
    for h in range(SWA_KV_HEADS):
        sl = slice(128 * h, 128 * (h + 1))
        k = kv_ref[:, sl]
        if rotate:
            k = _rope(k, cos, sa, sb, 32)
        ko_ref[:, sl] = k.astype(BF16)
    vo_ref[...] = kv_ref[:, 256:512].astype(BF16)


def _swa_prep(p, t_len, tables, tm=256):
    m = p.shape[0]
    tm = min(tm, t_len)
    seq_tiles = t_len // tm
    rotate = tables is not None
    in_specs = [pl.BlockSpec((tm, 1024), lambda i: (i, OFF_AQ // 1024)),
                pl.BlockSpec((tm, 512), lambda i: (i, OFF_AK // 512))]
    args = [p, p]
    if rotate:
        in_specs += [pl.BlockSpec((tm, 128), lambda i: (i % seq_tiles, 0))] * 3
        args += list(tables)
    return pl.pallas_call(
        functools.partial(_swa_prep_kernel, rotate=rotate),
        out_shape=(jax.ShapeDtypeStruct((m, 1024), BF16), jax.ShapeDtypeStruct((m, 256), BF16),
                   jax.ShapeDtypeStruct((m, 256), BF16)),
        grid=(m // tm,),
        in_specs=in_specs,
        out_specs=(pl.BlockSpec((tm, 1024), lambda i: (i, 0)), pl.BlockSpec((tm, 256), lambda i: (i, 0)),
                   pl.BlockSpec((tm, 256), lambda i: (i, 0))),
        compiler_params=_params("parallel"),
        name="swa_prep_rope" if rotate else "swa_prep",
    )(*args)


def _sink_softmax_pv(s, sink, v):
    m = jnp.maximum(jnp.max(s, -1, keepdims=True), sink)
    p = jnp.exp(s - m)
    den = jnp.sum(p, -1, keepdims=True) + jnp.exp(sink - m)
    return _mm(p.astype(BF16), v) / den


def _swa_kernel(q_ref, kp_ref, ko_ref, kn_ref, vp_ref, vo_ref, vn_ref, kc_ref, vc_ref, sink_ref, o_ref, *, nb, n_ctx):
    n = pl.program_id(1)
    qi = lax.broadcasted_iota(jnp.int32, (BLOCK, BLOCK), 0)
    kj = lax.broadcasted_iota(jnp.int32, (BLOCK, BLOCK), 1)
    neg = np.float32(-np.inf)
    bias_prev = jnp.where(jnp.logical_and(kj >= qi, n > 0), 0.0, neg)
    bias_next = jnp.where(jnp.logical_and(kj <= qi, n < nb - 1), 0.0, neg)
    bias = jnp.concatenate([bias_prev, jnp.zeros((BLOCK, BLOCK), F32), bias_next,
                            jnp.zeros((BLOCK, n_ctx), F32)], axis=1)
    kv_sl = [slice(128 * hh, 128 * (hh + 1)) for hh in range(SWA_KV_HEADS)]
    k_all = [jnp.concatenate([kp_ref[:, sl], ko_ref[:, sl], kn_ref[:, sl], kc_ref[:, sl]], axis=0) for sl in kv_sl]
    v_all = [jnp.concatenate([vp_ref[:, sl], vo_ref[:, sl], vn_ref[:, sl], vc_ref[:, sl]], axis=0) for sl in kv_sl]
    heads = range(SWA_HEADS)
    s = [_mm_nt(q_ref[:, 128 * h:128 * (h + 1)], k_all[h // SWA_GROUP]) + bias for h in heads]
    sink = [sink_ref[0:1, h:h + 1] for h in heads]
    m = [jnp.maximum(jnp.max(s[h], -1, keepdims=True), sink[h]) for h in heads]
    p = [jnp.exp(s[h] - m[h]) for h in heads]
    den = [jnp.sum(p[h], -1, keepdims=True) + jnp.exp(sink[h] - m[h]) for h in heads]
    o = [_mm(p[h].astype(BF16), v_all[h // SWA_GROUP]) for h in heads]
    for h in heads:
        o_ref[:, 128 * h:128 * (h + 1)] = (o[h] / den[h]).astype(BF16)


def _swa_attn(q, k, v, kc, vc, sink, bsz, t_len, n_ctx):
    nb = t_len // BLOCK

    def blk(width, off):
        def idx(b, n):
            return (b * nb + jnp.clip(n + off, 0, nb - 1), 0)
        return pl.BlockSpec((BLOCK, width), idx)

    ctx_spec = pl.BlockSpec((n_ctx, 256), lambda b, n: (b, 0))
    return pl.pallas_call(
        functools.partial(_swa_kernel, nb=nb, n_ctx=n_ctx),
        out_shape=jax.ShapeDtypeStruct((bsz * t_len, 1024), BF16),
        grid=(bsz, nb),
        in_specs=[blk(1024, 0), blk(256, -1), blk(256, 0), blk(256, 1), blk(256, -1), blk(256, 0), blk(256, 1),
                  ctx_spec, ctx_spec, _full((1, SWA_HEADS))],
        out_specs=blk(1024, 0),
        compiler_params=_params("parallel", "parallel"),
        name="swa_attention",
    )(q, k, k, k, v, v, v, kc, vc, sink.reshape(1, SWA_HEADS))


def _swa_ctx_kernel(q_ref, kc_ref, vc_ref, sink_ref, o_ref):
    for hh in range(SWA_KV_HEADS):
        sl = slice(128 * hh, 128 * (hh + 1))
        for g in range(SWA_GROUP):
            h = hh * SWA_GROUP + g
            hs = slice(128 * h, 128 * (h + 1))
            s = _mm_nt(q_ref[:, hs], kc_ref[:, sl])
            o_ref[:, hs] = _sink_softmax_pv(s, sink_ref[0:1, h:h + 1], vc_ref[:, sl]).astype(BF16)


def _swa_ctx_attn(qc, kc, vc, sink, bsz, n_ctx):
    return pl.pallas_call(
        _swa_ctx_kernel,
        out_shape=jax.ShapeDtypeStruct((bsz * n_ctx, 1024), BF16),
        grid=(bsz,),
        in_specs=[pl.BlockSpec((n_ctx, 1024), lambda b: (b, 0)), pl.BlockSpec((n_ctx, 256), lambda b: (b, 0)),
                  pl.BlockSpec((n_ctx, 256), lambda b: (b, 0)), _full((1, SWA_HEADS))],
        out_specs=pl.BlockSpec((n_ctx, 1024), lambda b: (b, 0)),
        compiler_params=_params("parallel"),
        name="swa_ctx_attention",
    )(qc, kc, vc, sink.reshape(1, SWA_HEADS))


def _lru_coef_kernel(x_ref, xp_ref, xn_ref, cw_ref, cb_ref, wh_ref, wl_ref, bg_ref, lam_ref,
                     af_ref, bf_ref, ab_ref, bb_ref, *, tm, seq_tiles):
    i = pl.program_id(0) % seq_tiles
    first = i == 0
    last = i == seq_tiles - 1
    x = x_ref[...]
    row = lax.broadcasted_iota(jnp.int32, x.shape, 0)
    p6 = jnp.where(first, 0.0, xp_ref[6:7, :])
    p7 = jnp.where(first, 0.0, xp_ref[7:8, :])
    n0 = jnp.where(last, 0.0, xn_ref[0:1, :])
    xm2, xm1, xp1 = _shift_rows(x, row, tm, (p6, p7), n0)
    u = (cw_ref[0:1, :] * xm2 + cw_ref[1:2, :] * xm1 + cw_ref[2:3, :] * x + cw_ref[3:4, :] * xp1) + cb_ref[...]
    sp = _softplus(-lam_ref[...])
    outs = ((af_ref, bf_ref), (ab_ref, bb_ref))
    for n in range(LRU_BLOCKS):
        sl = slice(128 * n, 128 * (n + 1))
        un = u[:, sl]
        z = _mm3w(un, wh_ref[n], wl_ref[n]) + bg_ref[n]
        for d in range(2):
            r = jax.nn.sigmoid(z[:, 256 * d:256 * d + 128])
            gi = jax.nn.sigmoid(z[:, 256 * d + 128:256 * d + 256])
            log_a = (-LRU_C) * r * sp[d:d + 1, sl]
            a_ref, b_ref = outs[d]
            a_ref[:, sl] = jnp.exp(log_a)
            b_ref[:, sl] = jnp.sqrt(1.0 - jnp.exp(2.0 * log_a)) * (gi * un)


def _lru_coef(p, t_len, conv_w, conv_b, wg_hi, wg_lo, bg, lam, tm=256):
    m = p.shape[0]
    tm = min(tm, t_len)
    seq_tiles = t_len // tm
    out = jax.ShapeDtypeStruct((m, 1024), F32)
    ospec = pl.BlockSpec((tm, 1024), lambda i: (i, 0))
    return pl.pallas_call(
        functools.partial(_lru_coef_kernel, tm=tm, seq_tiles=seq_tiles),
        out_shape=(out, out, out, out),
        grid=(m // tm,),
        in_specs=_halo_specs(OFF_BX // 1024, 1024, tm, m) + [
            _full((4, 1024)), _full((1, 1024)), _full((8, 128, 512)), _full((8, 128, 512)),
            _full((8, 1, 512)), _full((2, 1024))],
        out_specs=(ospec, ospec, ospec, ospec),
        compiler_params=_params("parallel"),
        name="lru_coefficients",
    )(p, p, p, conv_w, conv_b, wg_hi, wg_lo, bg, lam)


def _lru_scan_kernel(*refs, reverse, final, nc, tc):
    if final:
        a_ref, b_ref, h0_ref, hf_ref, gate_ref, o_ref, hl_ref, hs_ref, hrows_ref = refs
    else:
        a_ref, b_ref, h0_ref, o_ref, hl_ref, hs_ref = refs
        hrows_ref = o_ref
    c = pl.program_id(1)

    @pl.when(c == 0)
    def _():
        hs_ref[...] = h0_ref[...]

    def body(t, h):
        tt = (tc - 1 - t) if reverse else t
        h = a_ref[pl.ds(tt, 1), :] * h + b_ref[pl.ds(tt, 1), :]
        hrows_ref[pl.ds(tt, 1), :] = h
        return h

    h = lax.fori_loop(0, tc, body, hs_ref[...], unroll=8)
    hs_ref[...] = h

    @pl.when(c == (0 if reverse else nc - 1))
    def _():
        hl_ref[...] = hrows_ref[tc - 1:tc, :]

    if final:
        o_ref[...] = ((hf_ref[...] + hrows_ref[...]) * _gelu_tanh(gate_ref[...])).astype(BF16)


def _lru_scan(a, b, h0, bsz, t_len, *, reverse, h_other=None, p=None, tc=256):
    tc = min(tc, t_len)
    nc = t_len // tc
    final = h_other is not None

    def row_idx(bi, c):
        return bi * nc + ((nc - 1 - c) if reverse else c)

    rows = pl.BlockSpec((tc, 1024), lambda bi, c: (row_idx(bi, c), 0))
    state = pl.BlockSpec((None, 1, 1024), lambda bi, c: (bi, 0, 0))
    in_specs = [rows, rows, state]
    args = [a, b, h0]
    scratch = [pltpu.VMEM((1, 1024), F32)]
    if final:
        in_specs += [rows, pl.BlockSpec((tc, 1024), lambda bi, c: (row_idx(bi, c), OFF_BG // 1024))]
        args += [h_other, p]
        scratch.append(pltpu.VMEM((tc, 1024), F32))
    return pl.pallas_call(
        functools.partial(_lru_scan_kernel, reverse=reverse, final=final, nc=nc, tc=tc),
        out_shape=(jax.ShapeDtypeStruct((bsz * t_len, 1024), BF16 if final else F32),
                   jax.ShapeDtypeStruct((bsz, 1, 1024), F32)),
        grid=(bsz, nc),
        in_specs=in_specs,
        out_specs=(rows, state),
        scratch_shapes=scratch,
        compiler_params=_params("parallel", "arbitrary"),
        name="lru_scan_" + ("bwd_out" if final else ("bwd" if reverse else "fwd")),
    )(*args)


def _mla_proj_kernel(*refs, rotate):
    if rotate:
        (qa_ref, kva_ref, kr_ref, qn_ref, kvn_ref, wq_ref, wkv_ref, cos_ref, sa_ref, sb_ref,
         q_ref, k_ref, v_ref) = refs
        cos, sa, sb = cos_ref[...], sa_ref[...], sb_ref[...]
    else:
        qa_ref, kva_ref, kr_ref, qn_ref, kvn_ref, wq_ref, wkv_ref, q_ref, k_ref, v_ref = refs

    def rms(x, g):
        return (x * lax.rsqrt(jnp.mean(x * x, -1, keepdims=True) + RMS_EPS) * g).astype(BF16)

    scale = np.float32((MLA_NOPE + MLA_ROPE) ** -0.5 * np.log2(np.e))
    q = _mm(rms(qa_ref[...], qn_ref[...]), wq_ref[...]) * scale
    kv = _mm(rms(kva_ref[...], kvn_ref[...]), wkv_ref[...])
    kr = kr_ref[...]
    if rotate:
        kr = _rope(kr, cos, sa, sb, 16)
    kr = kr.astype(BF16)
    lane = lax.broadcasted_iota(jnp.int32, kr.shape, 1)
    ones_col = jnp.where(lane == 0, 1.0, 0.0).astype(BF16)
    for h in range(MLA_HEADS):
        lo = MLA_HEAD_PAD * h
        q_ref[:, lo:lo + 128] = q[:, lo:lo + 128].astype(BF16)
        qr = q[:, lo + 128:lo + 256]
        if rotate:
            qr = _rope(qr, cos, sa, sb, 16)
        q_ref[:, lo + 128:lo + 256] = qr.astype(BF16)
        k_ref[:, lo:lo + 128] = kv[:, 128 * h:128 * (h + 1)].astype(BF16)
        k_ref[:, lo + 128:lo + 256] = kr
        v_ref[:, lo:lo + 128] = kv[:, 1024 + 128 * h:1024 + 128 * (h + 1)].astype(BF16)
        v_ref[:, lo + 128:lo + 256] = ones_col


def _mla_proj(p, t_len, q_norm, kv_norm, wq, wkv, tables, tm=256):
    m = p.shape[0]
    tm = min(tm, t_len)
    seq_tiles = t_len // tm
    rotate = tables is not None
    in_specs = [pl.BlockSpec((tm, 1024), lambda i: (i, OFF_CQ // 1024)),
                pl.BlockSpec((tm, 512), lambda i: (i, OFF_CKV // 512)),
                pl.BlockSpec((tm, 128), lambda i: (i, OFF_CR // 128)),
                _full((1, 1024)), _full((1, 512)), _full((1024, 2048)), _full((512, 2048))]
    args = [p, p, p, q_norm.reshape(1, -1), kv_norm.reshape(1, -1), wq, wkv]
    if rotate:
        in_specs += [pl.BlockSpec((tm, 128), lambda i: (i % seq_tiles, 0))] * 3
        args += list(tables)
    wide = pl.BlockSpec((tm, 2048), lambda i: (i, 0))
    return pl.pallas_call(
        functools.partial(_mla_proj_kernel, rotate=rotate),
        out_shape=(jax.ShapeDtypeStruct((m, 2048), BF16),) * 3,
        grid=(m // tm,),
        in_specs=in_specs,
        out_specs=(wide, wide, wide),
        compiler_params=_params("parallel"),
        name="mla_project_rope" if rotate else "mla_project",
    )(*args)


def _mla_flash_kernel(*refs, with_lat, n_chunks, ck):
    if with_lat:
        q_ref, kc_ref, vc_ref, k_ref, v_ref, o_ref = refs
    else:
        q_ref, kc_ref, vc_ref, o_ref = refs
    q = q_ref[...]
    chunks = [(kc_ref, vc_ref, slice(None))]
    if with_lat:
        chunks += [(k_ref, v_ref, slice(j * ck, (j + 1) * ck)) for j in range(n_chunks)]
    k0, _, sl0 = chunks[0]
    s_cur = _mm_nt(q, k0[sl0, :])
    m = acc = pending = s_next = None
    for j, (_, vj_ref, slj) in enumerate(chunks):
        if j + 1 < len(chunks):
            kn_ref, _, sln = chunks[j + 1]
            s_next = _mm_nt(q, kn_ref[sln, :])
        if pending is not None:
            p_prev, v_prev, alpha_prev = pending
            pv = _mm(p_prev, v_prev)
            acc = pv if acc is None else alpha_prev * acc + pv
        mx = jnp.max(s_cur, -1, keepdims=True)
        m_new = mx if m is None else jnp.maximum(m, mx)
        alpha = None if m is None else jnp.exp2(m - m_new)
        pending = (jnp.exp2((s_cur - m_new).astype(BF16)), vj_ref[slj, :], alpha)
        m = m_new
        s_cur = s_next
    p_prev, v_prev, alpha_prev = pending
    pv = _mm(p_prev, v_prev)
    acc = pv if acc is None else alpha_prev * acc + pv
    o_ref[...] = (acc[:, 0:MLA_V] / acc[:, MLA_V:MLA_V + 1]).astype(BF16)


def _mla_flash(q, kc, vc, bsz, tq_len, n_ctx, k=None, v=None, tq=512, ck=512):
    with_lat = k is not None
    tq = min(tq, tq_len)
    nq = tq_len // tq
    in_specs = [pl.BlockSpec((tq, MLA_HEAD_PAD), lambda b, h, i: (b * nq + i, h)),
                pl.BlockSpec((n_ctx, MLA_HEAD_PAD), lambda b, h, i: (b, h)),
                pl.BlockSpec((n_ctx, MLA_HEAD_PAD), lambda b, h, i: (b, h))]
    args = [q, kc, vc]
    n_chunks = 0
    if with_lat:
        t_len = k.shape[0] // bsz
        ck = min(ck, t_len)
        n_chunks = t_len // ck
        in_specs += [pl.BlockSpec((t_len, MLA_HEAD_PAD), lambda b, h, i: (b, h)),
                     pl.BlockSpec((t_len, MLA_HEAD_PAD), lambda b, h, i: (b, h))]
        args += [k, v]
    return pl.pallas_call(
        functools.partial(_mla_flash_kernel, with_lat=with_lat, n_chunks=n_chunks, ck=ck),
        out_shape=jax.ShapeDtypeStruct((bsz * tq_len, 1024), BF16),
        grid=(bsz, MLA_HEADS, nq),
        in_specs=in_specs,
        out_specs=pl.BlockSpec((tq, MLA_V), lambda b, h, i: (b * nq + i, h)),
        compiler_params=_params("parallel", "parallel", "parallel"),
        name="mla_flash" if with_lat else "mla_ctx_attention",
    )(*args)


def _rwkv_feat_kernel(xr_ref, xrp_ref, xrn_ref, xk_ref, xkp_ref, xkn_ref, xv_ref, xvp_ref, xvn_ref,
                      xl_ref, xlp_ref, xln_ref, mup_ref, mun_ref, w0_ref, a0_ref,
                      w2h_ref, w2l_ref, a2h_ref, a2l_ref, g2h_ref, g2l_ref, kk_w_ref, ka_ref, rk_ref, bd_ref,
                      r_o, v_o, kk_o, g_o, bonus_o, lwf_o, kf_o, bf_o, lwb_o, kb_o, bb_o, *, tm, seq_tiles):
    i = pl.program_id(0) % seq_tiles
    first = i == 0
    last = i == seq_tiles - 1

    def shifted(x_ref, xp_ref, xn_ref, lo, hi):
        x = x_ref[...]
        row = lax.broadcasted_iota(jnp.int32, x.shape, 0)
        p7 = jnp.where(first, 0.0, xp_ref[7:8, :])
        n0 = jnp.where(last, 0.0, xn_ref[0:1, :])
        _, xm1, xp1 = _shift_rows(x, row, tm, (None, p7), n0)
        return x + mup_ref[:, lo:hi] * (xm1 - x) + mun_ref[:, lo:hi] * (xp1 - x)

    r = shifted(xr_ref, xrp_ref, xrn_ref, 0, 1024)
    k = shifted(xk_ref, xkp_ref, xkn_ref, 1024, 2048)
    v = shifted(xv_ref, xvp_ref, xvn_ref, 2048, 3072)
    lo = shifted(xl_ref, xlp_ref, xln_ref, 3072, 3584)
    bd = bd_ref[...]

    kkr = k * kk_w_ref[...]
    kk = kkr / jnp.maximum(jnp.sqrt(_seg_sum(kkr * kkr, bd)), 1e-12)
    g = _mm3w(jax.nn.sigmoid(lo[:, 256:512]), g2h_ref[...], g2l_ref[...])
    wl = w0_ref[...] + _mm3w(jnp.tanh(lo[:, 0:128]), w2h_ref[...], w2l_ref[...])
    lw = np.float32(-np.exp(-0.5)) * jax.nn.sigmoid(wl)
    a = jax.nn.sigmoid(a0_ref[...] + _mm3w(lo[:, 128:256], a2h_ref[...], a2l_ref[...]))
    r_o[...] = r
    v_o[...] = v
    kk_o[...] = kk
    g_o[...] = g
    bonus = None
    for d, (lw_o, k_o, b_o) in enumerate(((lwf_o, kf_o, bf_o), (lwb_o, kb_o, bb_o))):
        a_d = a[:, 1024 * d:1024 * (d + 1)]
        k_d = k * (1.0 + (a_d - 1.0) * ka_ref[...])
        bo = _seg_sum(r * k_d * rk_ref[...], bd) * v
        bonus = bo if bonus is None else bonus + bo
        lw_o[...] = lw[:, 1024 * d:1024 * (d + 1)]
        k_o[...] = k_d
        b_o[...] = kk * a_d
    bonus_o[...] = bonus


def _rwkv_feat(p, t_len, wts, tm=256):
    m = p.shape[0]
    tm = min(tm, t_len)
    seq_tiles = t_len // tm
    in_specs = (_halo_specs(OFF_DR // 1024, 1024, tm, m) + _halo_specs(OFF_DK // 1024, 1024, tm, m)
                + _halo_specs(OFF_DV // 1024, 1024, tm, m) + _halo_specs(OFF_DL // 512, 512, tm, m)
                + [_full(w.shape) for w in wts])
    out = jax.ShapeDtypeStruct((m, 1024), F32)
    ospec = pl.BlockSpec((tm, 1024), lambda i: (i, 0))
    return pl.pallas_call(
        functools.partial(_rwkv_feat_kernel, tm=tm, seq_tiles=seq_tiles),
        out_shape=(out,) * 11,
        grid=(m // tm,),
        in_specs=in_specs,
        out_specs=(ospec,) * 11,
        compiler_params=_params("parallel"),
        name="rwkv_features",
    )(*([p] * 12), *wts)


def _wkv_operands(r_ref, v_ref, kk_ref, lw_ref, k_ref, b_ref, incl, reverse):
    cl = WKV_CHUNK
    tri = jnp.where(incl, 1.0, 0.0).astype(BF16)
    lw = lw_ref[...]
    l1 = lw.astype(BF16)
    rem = lw - l1.astype(F32)
    l2 = rem.astype(BF16)
    l3 = (rem - l2.astype(F32)).astype(BF16)
    cum = _mm(tri, l1) + (_mm(tri, l2) + _mm(tri, l3))
    total = cum[0:1, :] if reverse else cum[cl - 1:cl, :]
    e_k = jnp.exp(-cum)
    e_t = jnp.exp(total - cum)
    k_raw = k_ref[...]
    b_raw = b_ref[...]
    return dict(d_c=jnp.exp(total),
                r_t=(r_ref[...] * jnp.exp(cum)).astype(BF16),
                kk_t=(kk_ref[...] * jnp.exp(cum - lw)).astype(BF16),
                k_t=(k_raw * e_k).astype(BF16), b_t=(b_raw * e_k).astype(BF16),
                k_d=(k_raw * e_t).astype(BF16), b_d=(b_raw * e_t).astype(BF16),
                v=v_ref[...].astype(BF16))


def _wkv_kernel(rf_ref, vf_ref, kkf_ref, lwf_ref, kf_ref, bf_ref, rb_ref, vb_ref, kkb_ref, lwb_ref, kb_ref, bb_ref,
                s0f_ref, s0b_ref, yf_ref, yb_ref, soutf_ref, soutb_ref, sf_ref, sb_ref, *, nc):
    c = pl.program_id(1)
    cl = WKV_CHUNK

    @pl.when(c == 0)
    def _():
        sf_ref[...] = s0f_ref[...]
        sb_ref[...] = s0b_ref[...]

    ti = lax.broadcasted_iota(jnp.int32, (cl, cl), 0)
    si = lax.broadcasted_iota(jnp.int32, (cl, cl), 1)
    blk = (ti // WKV_INV_BLOCK) == (si // WKV_INV_BLOCK)
    masks = ((si <= ti, si < ti), (si >= ti, si > ti))
    ops = (_wkv_operands(rf_ref, vf_ref, kkf_ref, lwf_ref, kf_ref, bf_ref, masks[0][0], False),
           _wkv_operands(rb_ref, vb_ref, kkb_ref, lwb_ref, kb_ref, bb_ref, masks[1][0], True))
    s_all = (sf_ref[...], sb_ref[...])

    chains = [(d, slice(RWKV_HEAD * h, RWKV_HEAD * (h + 1))) for d in range(2) for h in range(RWKV_HEADS)]
    heads = range(len(chains))
    incl = [masks[d][0] for d, _ in chains]
    strict = [masks[d][1] for d, _ in chains]

    def lanes(name):
        return [ops[d][name][:, sl] for d, sl in chains]

    v_h, k_dh, b_dh, d_ch = lanes("v"), lanes("k_d"), lanes("b_d"), lanes("d_c")
    s0 = [s_all[d][sl, :] for d, sl in chains]
    left = [jnp.concatenate([kk, r], axis=0) for kk, r in zip(lanes("kk_t"), lanes("r_t"))]
    right = [jnp.concatenate([k, b], axis=0) for k, b in zip(lanes("k_t"), lanes("b_t"))]
    a = [_mm_nt(left[h], right[h]) for h in heads]
    ls = [_mm_nt(left[h], s0[h].astype(BF16)) for h in heads]
    a_l = [jnp.where(strict[h], a[h][0:cl, cl:2 * cl], 0.0) for h in heads]
    a_v = [jnp.concatenate([jnp.where(strict[h], a[h][0:cl, 0:cl], 0.0),
                            jnp.where(incl[h], a[h][cl:2 * cl, 0:cl], 0.0)], axis=0).astype(BF16) for h in heads]
    a_rb = [jnp.where(incl[h], a[h][cl:2 * cl, cl:2 * cl], 0.0).astype(BF16) for h in heads]
    av = [_mm(a_v[h], v_h[h]) for h in heads]
    dg = [jnp.where(blk, a_l[h], 0.0) for h in heads]
    off = [a_l[h] - dg[h] for h in heads]
    d2 = [_mm1(dg[h], dg[h]) for h in heads]
    d4 = [_mm1(d2[h], d2[h]) for h in heads]
    x = [d2[h] - dg[h] - _mm1(dg[h], d2[h]) for h in heads]
    d8 = [_mm1(d4[h], d4[h]) for h in heads]
    x = [x[h] + d4[h] + _mm1(x[h], d4[h]) for h in heads]
    t16 = [x[h] + d8[h] + _mm1(x[h], d8[h]) for h in heads]
    n1 = [off[h] + _mm1(t16[h], off[h]) for h in heads]
    n2 = [_mm1(n1[h], n1[h]) for h in heads]
    y1 = [t16[h] - n1[h] - _mm1(n1[h], t16[h]) for h in heads]
    t_m = [y1[h] + n2[h] + _mm1(n2[h], y1[h]) for h in heads]
    rhs = [ls[h][0:cl] + av[h][0:cl] for h in heads]
    u = [rhs[h] + _mm1(t_m[h], rhs[h]) for h in heads]
    vu = [jnp.concatenate([v_h[h], (-u[h]).astype(BF16)], axis=0) for h in heads]
    kb = [jnp.concatenate([k_dh[h], b_dh[h]], axis=0) for h in heads]
    states = [s0[h] * d_ch[h] + _mm_tn(vu[h], kb[h]) for h in heads]
    ys = [ls[h][cl:2 * cl] + av[h][cl:2 * cl] - _mm(a_rb[h], u[h].astype(BF16)) for h in heads]
    yf_ref[...] = jnp.concatenate(ys[:RWKV_HEADS], axis=1)
    yb_ref[...] = jnp.concatenate(ys[RWKV_HEADS:], axis=1)
    sf_ref[...] = jnp.concatenate(states[:RWKV_HEADS], axis=0)
    sb_ref[...] = jnp.concatenate(states[RWKV_HEADS:], axis=0)

    @pl.when(c == nc - 1)
    def _():
        soutf_ref[...] = sf_ref[...]
        soutb_ref[...] = sb_ref[...]


def _wkv_scan(feats, s0_f, s0_b, bsz, t_len):
    r, v, kk, _, _, lw_f, k_f, b_f, lw_b, k_b, b_b = feats
    nc = t_len // WKV_CHUNK
    rows_f = pl.BlockSpec((WKV_CHUNK, GROUP_W), lambda bi, c: (bi * nc + c, 0))
    rows_b = pl.BlockSpec((WKV_CHUNK, GROUP_W), lambda bi, c: (bi * nc + nc - 1 - c, 0))
    state = pl.BlockSpec((None, GROUP_W, RWKV_HEAD), lambda bi, c: (bi, 0, 0))
    y_shape = jax.ShapeDtypeStruct((bsz * t_len, GROUP_W), F32)
    s_shape = jax.ShapeDtypeStruct((bsz, GROUP_W, RWKV_HEAD), F32)
    return pl.pallas_call(
        functools.partial(_wkv_kernel, nc=nc),
        out_shape=(y_shape, y_shape, s_shape, s_shape),
        grid=(bsz, nc),
        in_specs=[rows_f] * 6 + [rows_b] * 6 + [state, state],
        out_specs=(rows_f, rows_b, state, state),
        scratch_shapes=[pltpu.VMEM((GROUP_W, RWKV_HEAD), F32)] * 2,
        compiler_params=_params("parallel", "arbitrary"),
        name="wkv7_chunked",
    )(r, v, kk, lw_f, k_f, b_f, r, v, kk, lw_b, k_b, b_b, s0_f, s0_b)


def _rwkv_out_kernel(yf_ref, yb_ref, bonus_ref, g_ref, lng_ref, lnb_ref, bd_ref, o_ref):
    bd = bd_ref[...]
    y = yf_ref[...] + yb_ref[...]
    inv_n = np.float32(1.0 / RWKV_HEAD)
    yc = y - _seg_sum(y, bd) * inv_n
    var = _seg_sum(yc * yc, bd) * inv_n
    yn = yc * lax.rsqrt(var + RWKV_GN_EPS) * lng_ref[...] + lnb_ref[...]
    o_ref[...] = ((yn + bonus_ref[...]) * g_ref[...]).astype(BF16)


def _rwkv_out(y_f, y_b, bonus, g, ln_g, ln_b, bd, t_len, tm=256):
    m = y_f.shape[0]
    tm = min(tm, t_len)
    rows = pl.BlockSpec((tm, 1024), lambda i: (i, 0))
    return pl.pallas_call(
        _rwkv_out_kernel,
        out_shape=jax.ShapeDtypeStruct((m, 1024), BF16),
        grid=(m // tm,),
        in_specs=[rows] * 4 + [_full((1, 1024)), _full((1, 1024)), _full((128, 128))],
        out_specs=rows,
        compiler_params=_params("parallel"),
        name="rwkv_groupnorm_gate",
    )(y_f, y_b, bonus, g, ln_g.reshape(1, -1), ln_b.reshape(1, -1), bd)


def _w_in_layout_kernel(w_ref, o_ref):
    dst = 0
    for seg in _PROJ_SEGMENTS:
        if isinstance(seg, tuple):
            width = seg[1] - seg[0]
            o_ref[:, dst:dst + width] = w_ref[:, seg[0]:seg[1]].astype(BF16)
        else:
            width = seg
            o_ref[:, dst:dst + width] = jnp.zeros((o_ref.shape[0], width), BF16)
        dst += width


def _permute_w_in(w, tr=256):
    depth, rows, cols = w.shape
    return pl.pallas_call(
        _w_in_layout_kernel,
        out_shape=jax.ShapeDtypeStruct((depth, rows, N_PROJ), BF16),
        grid=(depth, rows // tr),
        in_specs=[pl.BlockSpec((None, tr, cols), lambda l, i: (l, i, 0))],
        out_specs=pl.BlockSpec((None, tr, N_PROJ), lambda l, i: (l, i, 0)),
        compiler_params=_params("parallel", "parallel"),
        name="w_in_layout",
    )(w)


def _permute_w_qb(w):
    w = w.reshape(w.shape[0], MLA_HEADS, MLA_NOPE + MLA_ROPE)
    w = jnp.pad(w, ((0, 0), (0, 0), (0, MLA_HEAD_PAD - MLA_NOPE - MLA_ROPE)))
    return w.reshape(w.shape[0], MLA_HEADS * MLA_HEAD_PAD).astype(BF16)


def _permute_w_kvb(w):
    w = w.reshape(w.shape[0], MLA_HEADS, MLA_NOPE + MLA_V)
    return jnp.concatenate([w[:, :, :MLA_NOPE].reshape(w.shape[0], -1),
                            w[:, :, MLA_NOPE:].reshape(w.shape[0], -1)], axis=1).astype(BF16)


def _rope_tables(pos_row, pos_col, half):
    inv = ROPE_BASE ** (-jnp.arange(half, dtype=F32) / half)
    zeros = jnp.zeros((pos_row.shape[0], half), F32)
    cos, sin_a, sin_b = [], [], []
    for pos in (pos_row, pos_col):
        ang = pos.astype(F32)[:, None] * inv[None, :]
        c, s = jnp.cos(ang), jnp.sin(ang)
        cos += [c, c]
        sin_a += [-s, zeros]
        sin_b += [zeros, s]
    pad = jnp.zeros((pos_row.shape[0], 128 - 4 * half), F32)
    return tuple(jnp.concatenate(t + [pad], axis=1) for t in (cos, sin_a, sin_b))


def _block_diag_ones():
    i = np.arange(128)
    return jnp.asarray((i[:, None] // RWKV_HEAD) == (i[None, :] // RWKV_HEAD), BF16)


def _split_w(w):
    hi = w.astype(BF16)
    return hi, (w - hi.astype(F32)).astype(BF16)


def _lru_gate_weights(wa, ba, wi, bi):
    w = jnp.concatenate([wa[0], wi[0], wa[1], wi[1]], axis=-1)
    b = jnp.concatenate([t.reshape(LRU_BLOCKS, 1, LRU_BLOCK_W) for t in (ba[0], bi[0], ba[1], bi[1])], axis=-1)
    return _split_w(w) + (b,)


def _rwkv_weights(mu_prev, mu_next, w0, w2, a0, a2, g2, k_k, k_a, r_k, bd):
    def pad_mu(mu):
        return jnp.pad(mu, (0, 3584 - mu.shape[0])).reshape(1, 3584)

    def two_dir(w):
        z = jnp.zeros_like(w[0])
        return jnp.concatenate([jnp.concatenate([w[0], z], axis=1), jnp.concatenate([z, w[1]], axis=1)], axis=0)

    g2p = jnp.pad(g2, ((0, 256 - RWKV_GATE_LORA), (0, 0)))
    return (pad_mu(mu_prev), pad_mu(mu_next), w0.reshape(1, 2048), a0.reshape(1, 2048),
            *_split_w(two_dir(w2)), *_split_w(two_dir(a2)), *_split_w(g2p),
            k_k.reshape(1, 1024), k_a.reshape(1, 1024), r_k.reshape(1, 1024), bd)


def kernel(x, c, ctx, c_ctx, w_mod, b_mod, w_in, w_out, ln1_g, ln1_b, w_ff1, w_ff2, ln2_g, ln2_b, swa_sink, lru_conv_w, lru_conv_b, lru_wa, lru_ba, lru_wi, lru_bi, lru_lam, mla_q_norm, mla_kv_norm, mla_w_qb, mla_w_kvb, rwkv_mu_prev, rwkv_mu_next, rwkv_w0, rwkv_w2, rwkv_a0, rwkv_a2, rwkv_g2, rwkv_k_k, rwkv_k_a, rwkv_r_k, rwkv_ln_g, rwkv_ln_b):
    bsz, t_len, d = x.shape
    n_ctx = ctx.shape[1]
    depth = w_mod.shape[0]

    cc = jnp.zeros((8, d), F32).at[:bsz].set(c).at[bsz].set(c_ctx)
    mod = _modulation(cc, w_mod, b_mod)

    pos = jnp.arange(t_len, dtype=jnp.int32)
    row, col = pos // GRID_W, pos % GRID_W
    swa_tables = _rope_tables(row, col, SWA_HEAD_DIM // 4)
    mla_tables = _rope_tables(row, col, MLA_ROPE // 4)
    bd = _block_diag_ones()
    zeros_h = jnp.zeros((bsz, 1, GROUP_W), F32)
    zeros_s = jnp.zeros((bsz, RWKV_HEADS * RWKV_HEAD, RWKV_HEAD), F32)

    xl = x.reshape(bsz * t_len, d)
    xc = ctx.reshape(bsz * n_ctx, d)

    w_in_p = _permute_w_in(w_in)
    w_out_b = w_out.astype(BF16)
    w_ff1_b = w_ff1.astype(BF16)
    w_ff2_b = w_ff2.astype(BF16)

    for l in range(depth):
        with_ctx = l < depth - 1
        chunks = [mod[l, :, k * d:(k + 1) * d] for k in range(6)]
        lat = [m[:bsz][:, None, :] for m in chunks]
        cxm = [m[bsz:bsz + 1][:, None, :] for m in chunks]

        p = _ln_mod_matmul(xl, lat[0], lat[1], w_in_p, l, act=None, out_dtype=F32, tm=1024, tn=512)
        pc = _ln_mod_matmul(xc, cxm[0], cxm[1], w_in_p, l, act=None, out_dtype=F32)

        q_a, k_a, v_a = _swa_prep(p, t_len, swa_tables)
        qc_a, kc_a, vc_a = _swa_prep(pc, n_ctx, None)
        o_swa = _swa_attn(q_a, k_a, v_a, kc_a, vc_a, swa_sink[l], bsz, t_len, n_ctx)

        gate_w = _lru_gate_weights(lru_wa[l], lru_ba[l], lru_wi[l], lru_bi[l])
        lru_args = (lru_conv_w[l], lru_conv_b[l].reshape(1, -1), *gate_w, lru_lam[l])
        caf, cbf, cab, cbb = _lru_coef(pc, n_ctx, *lru_args)
        hc_f, hlast_f = _lru_scan(caf, cbf, zeros_h, bsz, n_ctx, reverse=False)
        oc_lru, hlast_b = _lru_scan(cab, cbb, zeros_h, bsz, n_ctx, reverse=True, h_other=hc_f, p=pc)
        laf, lbf, lab, lbb = _lru_coef(p, t_len, *lru_args)
        h_f, _ = _lru_scan(laf, lbf, hlast_f, bsz, t_len, reverse=False)
        o_lru, _ = _lru_scan(lab, lbb, hlast_b, bsz, t_len, reverse=True, h_other=h_f, p=p)

        wq = _permute_w_qb(mla_w_qb[l])
        wkv = _permute_w_kvb(mla_w_kvb[l])
        q_c, k_c, v_c = _mla_proj(p, t_len, mla_q_norm[l], mla_kv_norm[l], wq, wkv, mla_tables)
        qc_c, kc_c, vc_c = _mla_proj(pc, n_ctx, mla_q_norm[l], mla_kv_norm[l], wq, wkv, None)
        o_mla = _mla_flash(q_c, kc_c, vc_c, bsz, t_len, n_ctx, k=k_c, v=v_c)

        rw = _rwkv_weights(rwkv_mu_prev[l], rwkv_mu_next[l], rwkv_w0[l], rwkv_w2[l], rwkv_a0[l], rwkv_a2[l],
                           rwkv_g2[l], rwkv_k_k[l], rwkv_k_a[l], rwkv_r_k[l], bd)
        fc = _rwkv_feat(pc, n_ctx, rw)
        yc_f, yc_b, s_f, s_b = _wkv_scan(fc, zeros_s, zeros_s, bsz, n_ctx)
        fl = _rwkv_feat(p, t_len, rw)
        y_f, y_b, _, _ = _wkv_scan(fl, s_f, s_b, bsz, t_len)
        o_rwkv = _rwkv_out(y_f, y_b, fl[4], fl[3], rwkv_ln_g[l], rwkv_ln_b[l], bd, t_len)

        def tail(xin, mix, m):
            x1 = _matmul_res_ln(mix, w_out_b, l, xin, m[2], ln1_g[l], ln1_b[l], tm=512)
            hid = _ln_mod_matmul(x1, m[3], m[4], w_ff1_b, l, act="relu2", out_dtype=BF16, tm=1024)
            return _matmul_res_ln(hid, w_ff2_b, l, x1, m[5], ln2_g[l], ln2_b[l])

        if with_ctx:
            oc_swa = _swa_ctx_attn(qc_a, kc_a, vc_a, swa_sink[l], bsz, n_ctx)
            oc_mla = _mla_flash(qc_c, kc_c, vc_c, bsz, n_ctx, n_ctx)
            oc_rwkv = _rwkv_out(yc_f, yc_b, fc[4], fc[3], rwkv_ln_g[l], rwkv_ln_b[l], bd, n_ctx)
            xc = tail(xc, jnp.concatenate([oc_swa, oc_lru, oc_mla, oc_rwkv], axis=-1), cxm)

        xl = tail(xl, jnp.concatenate([o_swa, o_lru, o_mla, o_rwkv], axis=-1), lat)

    return xl.reshape(bsz, t_len, d)
```

```python
import functools

import numpy as np
import jax
import jax.numpy as jnp
from jax import lax
from jax.experimental import pallas as pl
from jax.experimental.pallas import tpu as pltpu

F32 = jnp.float32
BF16 = jnp.bfloat16

D_MODEL = 4096
GRID_W = 64
GROUP_W = 1024
D_FF = 4 * D_MODEL
BLOCK = 128

SWA_HEAD_DIM = 128
SWA_HEADS = 8
SWA_KV_HEADS = 2
SWA_GROUP = 4

LRU_BLOCKS = 8
LRU_BLOCK_W = 128
LRU_C = 8.0

MLA_HEADS = 8
MLA_NOPE = 128
MLA_ROPE = 64
MLA_V = 128
MLA_HEAD_PAD = 256

RWKV_HEAD = 64
RWKV_HEADS = 16
RWKV_GATE_LORA = 160
RWKV_GN_EPS = 64e-5
WKV_CHUNK = 64
WKV_INV_BLOCK = 16

ROPE_BASE = 10000.0
LN_EPS = 1e-5
RMS_EPS = 1e-6
DEPTH = 2
ALPHA = (2 * DEPTH) ** 0.25

OFF_AQ, OFF_BX, OFF_BG, OFF_CQ, OFF_DR, OFF_DK, OFF_DV = 0, 1024, 2048, 3072, 4096, 5120, 6144
OFF_CKV, OFF_AK, OFF_AV, OFF_DL, OFF_CR = 7168, 7680, 7936, 8192, 8704
N_PROJ = 9216
_PROJ_LAYOUT = ((0, 1024, 1024), (1536, 1024, 1024), (2560, 1024, 1024), (3584, 1024, 1024), (5184, 1024, 1024),
                (6208, 1024, 1024), (7232, 1024, 1024), (4608, 512, 512), (1024, 256, 256), (1280, 256, 256),
                (8256, 416, 512), (5120, 64, 128), (None, 0, 384))

VMEM_LIMIT_V7X = 56 * 1024 * 1024


def _params(*sem):
    return pltpu.CompilerParams(dimension_semantics=sem, vmem_limit_bytes=VMEM_LIMIT_V7X)


def _split2(a):
    hi = a.astype(BF16)
    lo = (a - hi.astype(F32)).astype(BF16)
    return hi, lo


def _mm(a, b):
    return jnp.dot(a, b, preferred_element_type=F32)


def _mm_nt(a, b):
    return lax.dot_general(a, b, (((1,), (1,)), ((), ())), preferred_element_type=F32)


def _mm_tn(a, b):
    return lax.dot_general(a, b, (((0,), (0,)), ((), ())), preferred_element_type=F32)


def _mm3(a, b, mm=_mm):
    ah, al = _split2(a)
    bh, bl = _split2(b)
    return mm(ah, bh) + (mm(al, bh) + mm(ah, bl))


def _mm3w(a, bh, bl):
    ah, al = _split2(a)
    return _mm(ah, bh) + (_mm(al, bh) + _mm(ah, bl))


def _mm1(a, b, mm=_mm):
    return mm(a.astype(BF16), b.astype(BF16))


def _seg_sum(x, bd):
    parts = []
    for j in range(x.shape[1] // 128):
        hi, lo = _split2(x[:, 128 * j:128 * (j + 1)])
        parts.append(_mm(hi, bd) + _mm(lo, bd))
    return jnp.concatenate(parts, axis=1)


def _layer_norm_rows(x):
    mu = jnp.mean(x, -1, keepdims=True)
    xc = x - mu
    var = jnp.mean(xc * xc, -1, keepdims=True)
    return xc * lax.rsqrt(var + LN_EPS)


def _softplus(z):
    return jnp.maximum(z, 0.0) + jnp.log1p(jnp.exp(-jnp.abs(z)))


def _gelu_tanh(x):
    return 0.5 * x * (1.0 + jnp.tanh(np.sqrt(2.0 / np.pi).astype(np.float32) * (x + 0.044715 * (x * x * x))))


def _rope(x, cos, sin_a, sin_b, half):
    return x * cos + pltpu.roll(x, 128 - half, 1) * sin_a + pltpu.roll(x, half, 1) * sin_b


def _shift_rows(x, row, tm, prev_rows, next_row):
    p6, p7 = prev_rows
    xm1 = jnp.where(row == 0, p7, pltpu.roll(x, 1, 0))
    xp1 = jnp.where(row == tm - 1, next_row, pltpu.roll(x, tm - 1, 0))
    xm2 = None
    if p6 is not None:
        xm2 = jnp.where(row == 0, p6, jnp.where(row == 1, p7, pltpu.roll(x, 2, 0)))
    return xm2, xm1, xp1


def _halo_specs(col, width, tm, m_rows):
    r8 = tm // 8
    last8 = m_rows // 8 - 1
    return [pl.BlockSpec((tm, width), lambda i: (i, col)),
            pl.BlockSpec((8, width), lambda i: (jnp.maximum(i * r8 - 1, 0), col)),
            pl.BlockSpec((8, width), lambda i: (jnp.minimum((i + 1) * r8, last8), col))]


def _full(shape):
    nd = len(shape)
    return pl.BlockSpec(shape, lambda *_: (0,) * nd)


def _mod_kernel(c_ref, w_ref, b_ref, o_ref):
    cc = c_ref[...]
    a = cc * jax.nn.sigmoid(cc)
    o_ref[...] = _mm3(a, w_ref[...]) + b_ref[...]


def _modulation(cc, w_mod, b_mod, tn=512):
    depth, d, n = w_mod.shape
    return pl.pallas_call(
        _mod_kernel,
        out_shape=jax.ShapeDtypeStruct((depth, 8, n), F32),
        grid=(depth, n // tn),
        in_specs=[pl.BlockSpec((8, d), lambda l, j: (0, 0)),
                  pl.BlockSpec((None, d, tn), lambda l, j: (l, 0, j)),
                  pl.BlockSpec((None, 1, tn), lambda l, j: (l, 0, j))],
        out_specs=pl.BlockSpec((None, 8, tn), lambda l, j: (l, 0, j)),
        compiler_params=_params("parallel", "parallel"),
        name="adaln_modulation",
    )(cc, w_mod, b_mod.reshape(depth, 1, n))


def _lnmm_kernel(x_ref, sh_ref, sc_ref, w_ref, o_ref, xn_ref, *, act, tm):
    @pl.when(pl.program_id(1) == 0)
    def _():
        scale = 1.0 + sc_ref[...]
        shift = sh_ref[...]
        rows = min(tm, 128)

        def body(r, carry):
            sl = pl.ds(pl.multiple_of(r * rows, rows), rows)
            xn_ref[sl, :] = (_layer_norm_rows(x_ref[sl, :]) * scale + shift).astype(BF16)
            return carry

        lax.fori_loop(0, tm // rows, body, 0)

    acc = _mm(xn_ref[...], w_ref[...])
    if act == "relu2":
        acc = jnp.maximum(acc, 0.0)
        acc = acc * acc
    o_ref[...] = acc.astype(o_ref.dtype)


def _ln_mod_matmul(x, shift, scale, w, layer, *, act, out_dtype, tm=512, tn=1024):
    m, k = x.shape
    n = w.shape[2]
    nbm = shift.shape[0]
    tm = min(tm, m // nbm)
    seq_tiles = m // nbm // tm
    mod_spec = pl.BlockSpec((None, 1, k), lambda i, j: (i // seq_tiles, 0, 0))
    x_spec = pl.BlockSpec((tm, k), lambda i, j: (i, 0), pipeline_mode=pl.Buffered(1))
    return pl.pallas_call(
        functools.partial(_lnmm_kernel, act=act, tm=tm),
        out_shape=jax.ShapeDtypeStruct((m, n), out_dtype),
        grid=(m // tm, n // tn),
        in_specs=[x_spec, mod_spec, mod_spec,
                  pl.BlockSpec((None, k, tn), lambda i, j: (layer, 0, j))],
        out_specs=pl.BlockSpec((tm, tn), lambda i, j: (i, j)),
        scratch_shapes=[pltpu.VMEM((tm, k), BF16)],
        compiler_params=_params("parallel", "arbitrary"),
        name="ln_mod_matmul_" + (act or "id"),
    )(x, shift, scale, w)


def _mmln_kernel(a_ref, w_ref, x_ref, ga_ref, g_ref, b_ref, o_ref, *, nk, tm):
    k = pl.program_id(1)

    @pl.when(k == 0)
    def _():
        o_ref[...] = _mm(a_ref[...], w_ref[...])

    @pl.when(k > 0)
    def _():
        o_ref[...] += _mm(a_ref[...], w_ref[...])

    @pl.when(k == nk - 1)
    def _():
        gate = ga_ref[...]
        g = g_ref[...]
        b = b_ref[...]
        rows = min(tm, 128)

        def body(r, carry):
            sl = pl.ds(pl.multiple_of(r * rows, rows), rows)
            z = ALPHA * x_ref[sl, :] + gate * o_ref[sl, :]
            o_ref[sl, :] = _layer_norm_rows(z) * g + b
            return carry

        lax.fori_loop(0, tm // rows, body, 0)


def _matmul_res_ln(a, w, layer, xres, gate, g, b, *, tm=1024, tk=512):
    m, kdim = a.shape
    n = w.shape[2]
    nbm = gate.shape[0]
    tm = min(tm, m // nbm)
    seq_tiles = m // nbm // tm
    nk = kdim // tk
    mode = dict(pipeline_mode=pl.Buffered(1)) if nk >= 16 else {}
    return pl.pallas_call(
        functools.partial(_mmln_kernel, nk=nk, tm=tm),
        out_shape=jax.ShapeDtypeStruct((m, n), F32),
        grid=(m // tm, nk),
        in_specs=[pl.BlockSpec((tm, tk), lambda i, k: (i, k)),
                  pl.BlockSpec((None, tk, n), lambda i, k: (layer, k, 0)),
                  pl.BlockSpec((tm, n), lambda i, k: (i, 0), **mode),
                  pl.BlockSpec((None, 1, n), lambda i, k: (i // seq_tiles, 0, 0)),
                  _full((1, n)), _full((1, n))],
        out_specs=pl.BlockSpec((tm, n), lambda i, k: (i, 0), **mode),
        compiler_params=_params("parallel", "arbitrary"),
        name="matmul_res_ln",
    )(a, w, xres, gate, g.reshape(1, n), b.reshape(1, n))


def _swa_prep_kernel(*refs, rotate):
    if rotate:
        q_ref, kv_ref, cos_ref, sa_ref, sb_ref, qo_ref, ko_ref, vo_ref = refs
        cos, sa, sb = cos_ref[...], sa_ref[...], sb_ref[...]
    else:
        q_ref, kv_ref, qo_ref, ko_ref, vo_ref = refs
    scale = np.float32(SWA_HEAD_DIM ** -0.5)
    for h in range(SWA_HEADS):
        sl = slice(128 * h, 128 * (h + 1))
        q = q_ref[:, sl]
        if rotate:
            q = _rope(q, cos, sa, sb, 32)
        qo_ref[:, sl] = (q * scale).astype(BF16)
    for h in range(SWA_KV_HEADS):
        sl = slice(128 * h, 128 * (h + 1))
        k = kv_ref[:, sl]
        if rotate:
            k = _rope(k, cos, sa, sb, 32)
        ko_ref[:, sl] = k.astype(BF16)
    vo_ref[...] = kv_ref[:, 256:512].astype(BF16)


def _swa_prep(p, t_len, tables, tm=256):
    m = p.shape[0]
    tm = min(tm, t_len)
    seq_tiles = t_len // tm
    rotate = tables is not None
    in_specs = [pl.BlockSpec((tm, 1024), lambda i: (i, OFF_AQ // 1024)),
                pl.BlockSpec((tm, 512), lambda i: (i, OFF_AK // 512))]
    args = [p, p]
    if rotate:
        in_specs += [pl.BlockSpec((tm, 128), lambda i: (i % seq_tiles, 0))] * 3
        args += list(tables)
    return pl.pallas_call(
        functools.partial(_swa_prep_kernel, rotate=rotate),
        out_shape=(jax.ShapeDtypeStruct((m, 1024), BF16), jax.ShapeDtypeStruct((m, 256), BF16),
                   jax.ShapeDtypeStruct((m, 256), BF16)),
        grid=(m // tm,),
        in_specs=in_specs,
        out_specs=(pl.BlockSpec((tm, 1024), lambda i: (i, 0)), pl.BlockSpec((tm, 256), lambda i: (i, 0)),
                   pl.BlockSpec((tm, 256), lambda i: (i, 0))),
        compiler_params=_params("parallel"),
        name="swa_prep_rope" if rotate else "swa_prep",
    )(*args)


def _sink_softmax_pv(s, sink, v):
    m = jnp.maximum(jnp.max(s, -1, keepdims=True), sink)
    p = jnp.exp(s - m)
    den = jnp.sum(p, -1, keepdims=True) + jnp.exp(sink - m)
    return _mm(p.astype(BF16), v) / den


def _swa_kernel(q_ref, kp_ref, ko_ref, kn_ref, vp_ref, vo_ref, vn_ref, kc_ref, vc_ref, sink_ref, o_ref, *, nb, n_ctx):
    n = pl.program_id(1)
    qi = lax.broadcasted_iota(jnp.int32, (BLOCK, BLOCK), 0)
    kj = lax.broadcasted_iota(jnp.int32, (BLOCK, BLOCK), 1)
    neg = np.float32(-np.inf)
    bias_prev = jnp.where(jnp.logical_and(kj >= qi, n > 0), 0.0, neg)
    bias_next = jnp.where(jnp.logical_and(kj <= qi, n < nb - 1), 0.0, neg)
    bias = jnp.concatenate([bias_prev, jnp.zeros((BLOCK, BLOCK), F32), bias_next,
                            jnp.zeros((BLOCK, n_ctx), F32)], axis=1)
    kv_sl = [slice(128 * hh, 128 * (hh + 1)) for hh in range(SWA_KV_HEADS)]
    k_all = [jnp.concatenate([kp_ref[:, sl], ko_ref[:, sl], kn_ref[:, sl], kc_ref[:, sl]], axis=0) for sl in kv_sl]
    v_all = [jnp.concatenate([vp_ref[:, sl], vo_ref[:, sl], vn_ref[:, sl], vc_ref[:, sl]], axis=0) for sl in kv_sl]
    heads = range(SWA_HEADS)
    s = [_mm_nt(q_ref[:, 128 * h:128 * (h + 1)], k_all[h // SWA_GROUP]) + bias for h in heads]
    sink = [sink_ref[0:1, h:h + 1] for h in heads]
    m = [jnp.maximum(jnp.max(s[h], -1, keepdims=True), sink[h]) for h in heads]
    p = [jnp.exp(s[h] - m[h]) for h in heads]
    den = [jnp.sum(p[h], -1, keepdims=True) + jnp.exp(sink[h] - m[h]) for h in heads]
    o = [_mm(p[h].astype(BF16), v_all[h // SWA_GROUP]) for h in heads]
    for h in heads:
        o_ref[:, 128 * h:128 * (h + 1)] = (o[h] / den[h]).astype(BF16)


def _swa_attn(q, k, v, kc, vc, sink, bsz, t_len, n_ctx):
    nb = t_len // BLOCK

    def blk(width, off):
        def idx(b, n):
            return (b * nb + jnp.clip(n + off, 0, nb - 1), 0)
        return pl.BlockSpec((BLOCK, width), idx)

    ctx_spec = pl.BlockSpec((n_ctx, 256), lambda b, n: (b, 0))
    return pl.pallas_call(
        functools.partial(_swa_kernel, nb=nb, n_ctx=n_ctx),
        out_shape=jax.ShapeDtypeStruct((bsz * t_len, 1024), BF16),
        grid=(bsz, nb),
        in_specs=[blk(1024, 0), blk(256, -1), blk(256, 0), blk(256, 1), blk(256, -1), blk(256, 0), blk(256, 1),
                  ctx_spec, ctx_spec, _full((1, SWA_HEADS))],
        out_specs=blk(1024, 0),
        compiler_params=_params("parallel", "parallel"),
        name="swa_attention",
    )(q, k, k, k, v, v, v, kc, vc, sink.reshape(1, SWA_HEADS))


def _swa_ctx_kernel(q_ref, kc_ref, vc_ref, sink_ref, o_ref):
    for hh in range(SWA_KV_HEADS):
        sl = slice(128 * hh, 128 * (hh + 1))
        for g in range(SWA_GROUP):
            h = hh * SWA_GROUP + g
            hs = slice(128 * h, 128 * (h + 1))
            s = _mm_nt(q_ref[:, hs], kc_ref[:, sl])
            o_ref[:, hs] = _sink_softmax_pv(s, sink_ref[0:1, h:h + 1], vc_ref[:, sl]).astype(BF16)


def _swa_ctx_attn(qc, kc, vc, sink, bsz, n_ctx):
    return pl.pallas_call(
        _swa_ctx_kernel,
        out_shape=jax.ShapeDtypeStruct((bsz * n_ctx, 1024), BF16),
        grid=(bsz,),
        in_specs=[pl.BlockSpec((n_ctx, 1024), lambda b: (b, 0)), pl.BlockSpec((n_ctx, 256), lambda b: (b, 0)),
                  pl.BlockSpec((n_ctx, 256), lambda b: (b, 0)), _full((1, SWA_HEADS))],
        out_specs=pl.BlockSpec((n_ctx, 1024), lambda b: (b, 0)),
        compiler_params=_params("parallel"),
        name="swa_ctx_attention",
    )(qc, kc, vc, sink.reshape(1, SWA_HEADS))


def _lru_coef_kernel(x_ref, xp_ref, xn_ref, cw_ref, cb_ref, wh_ref, wl_ref, bg_ref, lam_ref,
                     af_ref, bf_ref, ab_ref, bb_ref, *, tm, seq_tiles):
    i = pl.program_id(0) % seq_tiles
    first = i == 0
    last = i == seq_tiles - 1
    x = x_ref[...]
    row = lax.broadcasted_iota(jnp.int32, x.shape, 0)
    p6 = jnp.where(first, 0.0, xp_ref[6:7, :])
    p7 = jnp.where(first, 0.0, xp_ref[7:8, :])
    n0 = jnp.where(last, 0.0, xn_ref[0:1, :])
    xm2, xm1, xp1 = _shift_rows(x, row, tm, (p6, p7), n0)
    u = (cw_ref[0:1, :] * xm2 + cw_ref[1:2, :] * xm1 + cw_ref[2:3, :] * x + cw_ref[3:4, :] * xp1) + cb_ref[...]
    sp = _softplus(-lam_ref[...])
    outs = ((af_ref, bf_ref), (ab_ref, bb_ref))
    for n in range(LRU_BLOCKS):
        sl = slice(128 * n, 128 * (n + 1))
        un = u[:, sl]
        z = _mm3w(un, wh_ref[n], wl_ref[n]) + bg_ref[n]
        for d in range(2):
            r = jax.nn.sigmoid(z[:, 256 * d:256 * d + 128])
            gi = jax.nn.sigmoid(z[:, 256 * d + 128:256 * d + 256])
            log_a = (-LRU_C) * r * sp[d:d + 1, sl]
            a_ref, b_ref = outs[d]
            a_ref[:, sl] = jnp.exp(log_a)
            b_ref[:, sl] = jnp.sqrt(1.0 - jnp.exp(2.0 * log_a)) * (gi * un)


def _lru_coef(p, t_len, conv_w, conv_b, wg_hi, wg_lo, bg, lam, tm=256):
    m = p.shape[0]
    tm = min(tm, t_len)
    seq_tiles = t_len // tm
    out = jax.ShapeDtypeStruct((m, 1024), F32)
    ospec = pl.BlockSpec((tm, 1024), lambda i: (i, 0))
    return pl.pallas_call(
        functools.partial(_lru_coef_kernel, tm=tm, seq_tiles=seq_tiles),
        out_shape=(out, out, out, out),
        grid=(m // tm,),
        in_specs=_halo_specs(OFF_BX // 1024, 1024, tm, m) + [
            _full((4, 1024)), _full((1, 1024)), _full((8, 128, 512)), _full((8, 128, 512)),
            _full((8, 1, 512)), _full((2, 1024))],
        out_specs=(ospec, ospec, ospec, ospec),
        compiler_params=_params("parallel"),
        name="lru_coefficients",
    )(p, p, p, conv_w, conv_b, wg_hi, wg_lo, bg, lam)


def _lru_scan_kernel(*refs, reverse, final, nc, tc):
    if final:
        a_ref, b_ref, h0_ref, hf_ref, gate_ref, o_ref, hl_ref, hs_ref, hrows_ref = refs
    else:
        a_ref, b_ref, h0_ref, o_ref, hl_ref, hs_ref = refs
        hrows_ref = o_ref
    c = pl.program_id(0)

    @pl.when(c == 0)
    def _():
        hs_ref[...] = h0_ref[...]

    def body(t, h):
        tt = (tc - 1 - t) if reverse else t
        h = a_ref[:, pl.ds(tt, 1), :] * h + b_ref[:, pl.ds(tt, 1), :]
        hrows_ref[:, pl.ds(tt, 1), :] = h
        return h

    h = lax.fori_loop(0, tc, body, hs_ref[...], unroll=8)
    hs_ref[...] = h

    @pl.when(c == (0 if reverse else nc - 1))
    def _():
        hl_ref[...] = hrows_ref[:, tc - 1:tc, :]

    if final:
        o_ref[...] = ((hf_ref[...] + hrows_ref[...]) * _gelu_tanh(gate_ref[...])).astype(BF16)


def _lru_scan(a, b, h0, bsz, t_len, *, reverse, h_other=None, p=None, tc=256):
    tc = min(tc, t_len)
    nc = t_len // tc
    final = h_other is not None

    def chunk(c):
        return (nc - 1 - c) if reverse else c

    def seq(t):
        return t.reshape(bsz, t_len, t.shape[-1])

    rows = pl.BlockSpec((bsz, tc, 1024), lambda c: (0, chunk(c), 0))
    state = _full((bsz, 1, 1024))
    in_specs = [rows, rows, state]
    args = [seq(a), seq(b), h0]
    scratch = [pltpu.VMEM((bsz, 1, 1024), F32)]
    if final:
        in_specs += [rows, pl.BlockSpec((bsz, tc, 1024), lambda c: (0, chunk(c), OFF_BG // 1024))]
        args += [seq(h_other), seq(p)]
        scratch.append(pltpu.VMEM((bsz, tc, 1024), F32))
    out, h_last = pl.pallas_call(
        functools.partial(_lru_scan_kernel, reverse=reverse, final=final, nc=nc, tc=tc),
        out_shape=(jax.ShapeDtypeStruct((bsz, t_len, 1024), BF16 if final else F32),
                   jax.ShapeDtypeStruct((bsz, 1, 1024), F32)),
        grid=(nc,),
        in_specs=in_specs,
        out_specs=(rows, state),
        scratch_shapes=scratch,
        compiler_params=_params("arbitrary"),
        name="lru_scan_" + ("bwd_out" if final else ("bwd" if reverse else "fwd")),
    )(*args)
    return out.reshape(bsz * t_len, 1024), h_last


def _mla_proj_kernel(*refs, rotate):
    if rotate:
        (qa_ref, kva_ref, kr_ref, qn_ref, kvn_ref, wq_ref, wkv_ref, cos_ref, sa_ref, sb_ref,
         q_ref, k_ref, v_ref) = refs
        cos, sa, sb = cos_ref[...], sa_ref[...], sb_ref[...]
    else:
        qa_ref, kva_ref, kr_ref, qn_ref, kvn_ref, wq_ref, wkv_ref, q_ref, k_ref, v_ref = refs

    def rms(x, g):
        return (x * lax.rsqrt(jnp.mean(x * x, -1, keepdims=True) + RMS_EPS) * g).astype(BF16)

    scale = np.float32((MLA_NOPE + MLA_ROPE) ** -0.5 * np.log2(np.e))
    q = _mm(rms(qa_ref[...], qn_ref[...]), wq_ref[...]) * scale
    kv = _mm(rms(kva_ref[...], kvn_ref[...]), wkv_ref[...])
    kr = kr_ref[...]
    if rotate:
        kr = _rope(kr, cos, sa, sb, 16)
    kr = kr.astype(BF16)
    lane = lax.broadcasted_iota(jnp.int32, kr.shape, 1)
    ones_col = jnp.where(lane == 0, 1.0, 0.0).astype(BF16)
    for h in range(MLA_HEADS):
        lo = MLA_HEAD_PAD * h
        q_ref[:, lo:lo + 128] = q[:, lo:lo + 128].astype(BF16)
        qr = q[:, lo + 128:lo + 256]
        if rotate:
            qr = _rope(qr, cos, sa, sb, 16)
        q_ref[:, lo + 128:lo + 256] = qr.astype(BF16)
        k_ref[:, lo:lo + 128] = kv[:, 128 * h:128 * (h + 1)].astype(BF16)
        k_ref[:, lo + 128:lo + 256] = kr
        v_ref[:, lo:lo + 128] = kv[:, 1024 + 128 * h:1024 + 128 * (h + 1)].astype(BF16)
        v_ref[:, lo + 128:lo + 256] = ones_col


def _mla_proj(p, t_len, q_norm, kv_norm, wq, wkv, tables, tm=256):
    m = p.shape[0]
    tm = min(tm, t_len)
    seq_tiles = t_len // tm
    rotate = tables is not None
    in_specs = [pl.BlockSpec((tm, 1024), lambda i: (i, OFF_CQ // 1024)),
                pl.BlockSpec((tm, 512), lambda i: (i, OFF_CKV // 512)),
                pl.BlockSpec((tm, 128), lambda i: (i, OFF_CR // 128)),
                _full((1, 1024)), _full((1, 512)), _full((1024, 2048)), _full((512, 2048))]
    args = [p, p, p, q_norm.reshape(1, -1), kv_norm.reshape(1, -1), wq, wkv]
    if rotate:
        in_specs += [pl.BlockSpec((tm, 128), lambda i: (i % seq_tiles, 0))] * 3
        args += list(tables)
    wide = pl.BlockSpec((tm, 2048), lambda i: (i, 0))
    return pl.pallas_call(
        functools.partial(_mla_proj_kernel, rotate=rotate),
        out_shape=(jax.ShapeDtypeStruct((m, 2048), BF16),) * 3,
        grid=(m // tm,),
        in_specs=in_specs,
        out_specs=(wide, wide, wide),
        compiler_params=_params("parallel"),
        name="mla_project_rope" if rotate else "mla_project",
    )(*args)


def _mla_flash_kernel(*refs, with_lat, n_chunks, ck):
    if with_lat:
        q_ref, kc_ref, vc_ref, k_ref, v_ref, o_ref = refs
    else:
        q_ref, kc_ref, vc_ref, o_ref = refs
    q = q_ref[...]
    chunks = [(kc_ref, vc_ref, slice(None))]
    if with_lat:
        chunks += [(k_ref, v_ref, slice(j * ck, (j + 1) * ck)) for j in range(n_chunks)]
    k0, _, sl0 = chunks[0]
    s_cur = _mm_nt(q, k0[sl0, :])
    m = acc = pending = s_next = None
    for j, (_, vj_ref, slj) in enumerate(chunks):
        if j + 1 < len(chunks):
            kn_ref, _, sln = chunks[j + 1]
            s_next = _mm_nt(q, kn_ref[sln, :])
        if pending is not None:
            p_prev, v_prev, alpha_prev = pending
            pv = _mm(p_prev, v_prev)
            acc = pv if acc is None else alpha_prev * acc + pv
        mx = jnp.max(s_cur, -1, keepdims=True)
        m_new = mx if m is None else jnp.maximum(m, mx)
        alpha = None if m is None else jnp.exp2(m - m_new)
        pending = (jnp.exp2((s_cur - m_new).astype(BF16)), vj_ref[slj, :], alpha)
        m = m_new
        s_cur = s_next
    p_prev, v_prev, alpha_prev = pending
    pv = _mm(p_prev, v_prev)
    acc = pv if acc is None else alpha_prev * acc + pv
    o_ref[...] = (acc[:, 0:MLA_V] / acc[:, MLA_V:MLA_V + 1]).astype(BF16)


def _mla_flash(q, kc, vc, bsz, tq_len, n_ctx, k=None, v=None, tq=512, ck=512):
    with_lat = k is not None
    tq = min(tq, tq_len)
    nq = tq_len // tq
    in_specs = [pl.BlockSpec((tq, MLA_HEAD_PAD), lambda b, h, i: (b * nq + i, h)),
                pl.BlockSpec((n_ctx, MLA_HEAD_PAD), lambda b, h, i: (b, h)),
                pl.BlockSpec((n_ctx, MLA_HEAD_PAD), lambda b, h, i: (b, h))]
    args = [q, kc, vc]
    n_chunks = 0
    if with_lat:
        t_len = k.shape[0] // bsz
        ck = min(ck, t_len)
        n_chunks = t_len // ck
        in_specs += [pl.BlockSpec((t_len, MLA_HEAD_PAD), lambda b, h, i: (b, h)),
                     pl.BlockSpec((t_len, MLA_HEAD_PAD), lambda b, h, i: (b, h))]
        args += [k, v]
    return pl.pallas_call(
        functools.partial(_mla_flash_kernel, with_lat=with_lat, n_chunks=n_chunks, ck=ck),
        out_shape=jax.ShapeDtypeStruct((bsz * tq_len, 1024), BF16),
        grid=(bsz, MLA_HEADS, nq),
        in_specs=in_specs,
        out_specs=pl.BlockSpec((tq, MLA_V), lambda b, h, i: (b * nq + i, h)),
        compiler_params=_params("parallel", "parallel", "parallel"),
        name="mla_flash" if with_lat else "mla_ctx_attention",
    )(*args)


def _rwkv_feat_kernel(xr_ref, xrp_ref, xrn_ref, xk_ref, xkp_ref, xkn_ref, xv_ref, xvp_ref, xvn_ref,
                      xl_ref, xlp_ref, xln_ref, mup_ref, mun_ref, w0_ref, a0_ref,
                      w2h_ref, w2l_ref, a2h_ref, a2l_ref, g2h_ref, g2l_ref, kk_w_ref, ka_ref, rk_ref, bd_ref,
                      r_o, v_o, kk_o, g_o, bonus_o, lwf_o, kf_o, bf_o, lwb_o, kb_o, bb_o, *, tm, seq_tiles):
    i = pl.program_id(0) % seq_tiles
    first = i == 0
    last = i == seq_tiles - 1

    def shifted(x_ref, xp_ref, xn_ref, lo, hi):
        x = x_ref[...]
        row = lax.broadcasted_iota(jnp.int32, x.shape, 0)
        p7 = jnp.where(first, 0.0, xp_ref[7:8, :])
        n0 = jnp.where(last, 0.0, xn_ref[0:1, :])
        _, xm1, xp1 = _shift_rows(x, row, tm, (None, p7), n0)
        return x + mup_ref[:, lo:hi] * (xm1 - x) + mun_ref[:, lo:hi] * (xp1 - x)

    r = shifted(xr_ref, xrp_ref, xrn_ref, 0, 1024)
    k = shifted(xk_ref, xkp_ref, xkn_ref, 1024, 2048)
    v = shifted(xv_ref, xvp_ref, xvn_ref, 2048, 3072)
    lo = shifted(xl_ref, xlp_ref, xln_ref, 3072, 3584)
    bd = bd_ref[...]

    kkr = k * kk_w_ref[...]
    kk = kkr / jnp.maximum(jnp.sqrt(_seg_sum(kkr * kkr, bd)), 1e-12)
    g = _mm3w(jax.nn.sigmoid(lo[:, 256:512]), g2h_ref[...], g2l_ref[...])
    wl = w0_ref[...] + _mm3w(jnp.tanh(lo[:, 0:128]), w2h_ref[...], w2l_ref[...])
    lw = np.float32(-np.exp(-0.5)) * jax.nn.sigmoid(wl)
    a = jax.nn.sigmoid(a0_ref[...] + _mm3w(lo[:, 128:256], a2h_ref[...], a2l_ref[...]))
    r_o[...] = r
    v_o[...] = v
    kk_o[...] = kk
    g_o[...] = g
    bonus = None
    for d, (lw_o, k_o, b_o) in enumerate(((lwf_o, kf_o, bf_o), (lwb_o, kb_o, bb_o))):
        a_d = a[:, 1024 * d:1024 * (d + 1)]
        k_d = k * (1.0 + (a_d - 1.0) * ka_ref[...])
        bo = _seg_sum(r * k_d * rk_ref[...], bd) * v
        bonus = bo if bonus is None else bonus + bo
        lw_o[...] = lw[:, 1024 * d:1024 * (d + 1)]
        k_o[...] = k_d
        b_o[...] = kk * a_d
    bonus_o[...] = bonus


def _rwkv_feat(p, t_len, wts, tm=256):
    m = p.shape[0]
    tm = min(tm, t_len)
    seq_tiles = t_len // tm
    in_specs = (_halo_specs(OFF_DR // 1024, 1024, tm, m) + _halo_specs(OFF_DK // 1024, 1024, tm, m)
                + _halo_specs(OFF_DV // 1024, 1024, tm, m) + _halo_specs(OFF_DL // 512, 512, tm, m)
                + [_full(w.shape) for w in wts])
    out = jax.ShapeDtypeStruct((m, 1024), F32)
    ospec = pl.BlockSpec((tm, 1024), lambda i: (i, 0))
    return pl.pallas_call(
        functools.partial(_rwkv_feat_kernel, tm=tm, seq_tiles=seq_tiles),
        out_shape=(out,) * 11,
        grid=(m // tm,),
        in_specs=in_specs,
        out_specs=(ospec,) * 11,
        compiler_params=_params("parallel"),
        name="rwkv_features",
    )(*([p] * 12), *wts)


def _wkv_operands(r_ref, v_ref, kk_ref, lw_ref, k_ref, b_ref, incl, reverse):
    cl = WKV_CHUNK
    tri = jnp.where(incl, 1.0, 0.0).astype(BF16)
    lw = lw_ref[...]
    l1 = lw.astype(BF16)
    rem = lw - l1.astype(F32)
    l2 = rem.astype(BF16)
    l3 = (rem - l2.astype(F32)).astype(BF16)
    cum = _mm(tri, l1) + (_mm(tri, l2) + _mm(tri, l3))
    total = cum[0:1, :] if reverse else cum[cl - 1:cl, :]
    e_k = jnp.exp(-cum)
    e_t = jnp.exp(total - cum)
    k_raw = k_ref[...]
    b_raw = b_ref[...]
    return dict(d_c=jnp.exp(total),
                r_t=(r_ref[...] * jnp.exp(cum)).astype(BF16),
                kk_t=(kk_ref[...] * jnp.exp(cum - lw)).astype(BF16),
                k_t=(k_raw * e_k).astype(BF16), b_t=(b_raw * e_k).astype(BF16),
                k_d=(k_raw * e_t).astype(BF16), b_d=(b_raw * e_t).astype(BF16),
                v=v_ref[...].astype(BF16))


def _wkv_kernel(rf_ref, vf_ref, kkf_ref, lwf_ref, kf_ref, bf_ref, rb_ref, vb_ref, kkb_ref, lwb_ref, kb_ref, bb_ref,
                s0f_ref, s0b_ref, yf_ref, yb_ref, soutf_ref, soutb_ref, sf_ref, sb_ref, *, nc):
    c = pl.program_id(1)
    cl = WKV_CHUNK

    @pl.when(c == 0)
    def _():
        sf_ref[...] = s0f_ref[...]
        sb_ref[...] = s0b_ref[...]

    ti = lax.broadcasted_iota(jnp.int32, (cl, cl), 0)
    si = lax.broadcasted_iota(jnp.int32, (cl, cl), 1)
    blk = (ti // WKV_INV_BLOCK) == (si // WKV_INV_BLOCK)
    masks = ((si <= ti, si < ti), (si >= ti, si > ti))
    ops = (_wkv_operands(rf_ref, vf_ref, kkf_ref, lwf_ref, kf_ref, bf_ref, masks[0][0], False),
           _wkv_operands(rb_ref, vb_ref, kkb_ref, lwb_ref, kb_ref, bb_ref, masks[1][0], True))
    s_all = (sf_ref[...], sb_ref[...])

    chains = [(d, slice(RWKV_HEAD * h, RWKV_HEAD * (h + 1))) for d in range(2) for h in range(RWKV_HEADS)]
    heads = range(len(chains))
    incl = [masks[d][0] for d, _ in chains]
    strict = [masks[d][1] for d, _ in chains]

    def lanes(name):
        return [ops[d][name][:, sl] for d, sl in chains]

    v_h, k_dh, b_dh, d_ch = lanes("v"), lanes("k_d"), lanes("b_d"), lanes("d_c")
    s0 = [s_all[d][sl, :] for d, sl in chains]
    left = [jnp.concatenate([kk, r], axis=0) for kk, r in zip(lanes("kk_t"), lanes("r_t"))]
    right = [jnp.concatenate([k, b], axis=0) for k, b in zip(lanes("k_t"), lanes("b_t"))]
    a = [_mm_nt(left[h], right[h]) for h in heads]
    ls = [_mm_nt(left[h], s0[h].astype(BF16)) for h in heads]
    a_l = [jnp.where(strict[h], a[h][0:cl, cl:2 * cl], 0.0) for h in heads]
    a_v = [jnp.concatenate([jnp.where(strict[h], a[h][0:cl, 0:cl], 0.0),
                            jnp.where(incl[h], a[h][cl:2 * cl, 0:cl], 0.0)], axis=0).astype(BF16) for h in heads]
    a_rb = [jnp.where(incl[h], a[h][cl:2 * cl, cl:2 * cl], 0.0).astype(BF16) for h in heads]
    av = [_mm(a_v[h], v_h[h]) for h in heads]
    dg = [jnp.where(blk, a_l[h], 0.0) for h in heads]
    off = [a_l[h] - dg[h] for h in heads]
    d2 = [_mm1(dg[h], dg[h]) for h in heads]
    d4 = [_mm1(d2[h], d2[h]) for h in heads]
    x = [d2[h] - dg[h] - _mm1(dg[h], d2[h]) for h in heads]
    d8 = [_mm1(d4[h], d4[h]) for h in heads]
    x = [x[h] + d4[h] + _mm1(x[h], d4[h]) for h in heads]
    t16 = [x[h] + d8[h] + _mm1(x[h], d8[h]) for h in heads]
    n1 = [off[h] + _mm1(t16[h], off[h]) for h in heads]
    n2 = [_mm1(n1[h], n1[h]) for h in heads]
    y1 = [t16[h] - n1[h] - _mm1(n1[h], t16[h]) for h in heads]
    t_m = [y1[h] + n2[h] + _mm1(n2[h], y1[h]) for h in heads]
    rhs = [ls[h][0:cl] + av[h][0:cl] for h in heads]
    u = [rhs[h] + _mm1(t_m[h], rhs[h]) for h in heads]
    vu = [jnp.concatenate([v_h[h], (-u[h]).astype(BF16)], axis=0) for h in heads]
    kb = [jnp.concatenate([k_dh[h], b_dh[h]], axis=0) for h in heads]
    states = [s0[h] * d_ch[h] + _mm_tn(vu[h], kb[h]) for h in heads]
    ys = [ls[h][cl:2 * cl] + av[h][cl:2 * cl] - _mm(a_rb[h], u[h].astype(BF16)) for h in heads]
    yf_ref[...] = jnp.concatenate(ys[:RWKV_HEADS], axis=1)
    yb_ref[...] = jnp.concatenate(ys[RWKV_HEADS:], axis=1)
    sf_ref[...] = jnp.concatenate(states[:RWKV_HEADS], axis=0)
    sb_ref[...] = jnp.concatenate(states[RWKV_HEADS:], axis=0)

    @pl.when(c == nc - 1)
    def _():
        soutf_ref[...] = sf_ref[...]
        soutb_ref[...] = sb_ref[...]


def _wkv_scan(feats, s0_f, s0_b, bsz, t_len):
    r, v, kk, _, _, lw_f, k_f, b_f, lw_b, k_b, b_b = feats
    nc = t_len // WKV_CHUNK
    rows_f = pl.BlockSpec((WKV_CHUNK, GROUP_W), lambda bi, c: (bi * nc + c, 0))
    rows_b = pl.BlockSpec((WKV_CHUNK, GROUP_W), lambda bi, c: (bi * nc + nc - 1 - c, 0))
    state = pl.BlockSpec((None, GROUP_W, RWKV_HEAD), lambda bi, c: (bi, 0, 0))
    y_shape = jax.ShapeDtypeStruct((bsz * t_len, GROUP_W), F32)
    s_shape = jax.ShapeDtypeStruct((bsz, GROUP_W, RWKV_HEAD), F32)
    return pl.pallas_call(
        functools.partial(_wkv_kernel, nc=nc),
        out_shape=(y_shape, y_shape, s_shape, s_shape),
        grid=(bsz, nc),
        in_specs=[rows_f] * 6 + [rows_b] * 6 + [state, state],
        out_specs=(rows_f, rows_b, state, state),
        scratch_shapes=[pltpu.VMEM((GROUP_W, RWKV_HEAD), F32)] * 2,
        compiler_params=_params("parallel", "arbitrary"),
        name="wkv7_chunked",
    )(r, v, kk, lw_f, k_f, b_f, r, v, kk, lw_b, k_b, b_b, s0_f, s0_b)


def _rwkv_out_kernel(yf_ref, yb_ref, bonus_ref, g_ref, lng_ref, lnb_ref, bd_ref, o_ref):
    bd = bd_ref[...]
    y = yf_ref[...] + yb_ref[...]
    inv_n = np.float32(1.0 / RWKV_HEAD)
    yc = y - _seg_sum(y, bd) * inv_n
    var = _seg_sum(yc * yc, bd) * inv_n
    yn = yc * lax.rsqrt(var + RWKV_GN_EPS) * lng_ref[...] + lnb_ref[...]
    o_ref[...] = ((yn + bonus_ref[...]) * g_ref[...]).astype(BF16)


def _rwkv_out(y_f, y_b, bonus, g, ln_g, ln_b, bd, t_len, tm=256):
    m = y_f.shape[0]
    tm = min(tm, t_len)
    rows = pl.BlockSpec((tm, 1024), lambda i: (i, 0))
    return pl.pallas_call(
        _rwkv_out_kernel,
        out_shape=jax.ShapeDtypeStruct((m, 1024), BF16),
        grid=(m // tm,),
        in_specs=[rows] * 4 + [_full((1, 1024)), _full((1, 1024)), _full((128, 128))],
        out_specs=rows,
        compiler_params=_params("parallel"),
        name="rwkv_groupnorm_gate",
    )(y_f, y_b, bonus, g, ln_g.reshape(1, -1), ln_b.reshape(1, -1), bd)


def _w_in_layout_kernel(wt_ref, o_ref):
    tr = o_ref.shape[0]
    dst = 0
    for src, width, padded in _PROJ_LAYOUT:
        step = min(padded, 256)
        for off in range(0, padded, step):
            piece = min(step, padded - off)
            real = 0 if src is None else max(0, min(piece, width - off))
            if real == 0:
                block = jnp.zeros((tr, piece), BF16)
            else:
                rows = wt_ref[src + off:src + off + real, :]
                if real < piece:
                    rows = jnp.concatenate([rows, jnp.zeros((piece - real, tr), F32)], axis=0)
                block = rows.T.astype(BF16)
            o_ref[:, dst + off:dst + off + piece] = block
        dst += padded


def _permute_w_in(w, tr=256):
    depth, rows, cols = w.shape
    return pl.pallas_call(
        _w_in_layout_kernel,
        out_shape=jax.ShapeDtypeStruct((depth, rows, N_PROJ), BF16),
        grid=(depth, rows // tr),
        in_specs=[pl.BlockSpec((None, cols, tr), lambda l, i: (l, 0, i))],
        out_specs=pl.BlockSpec((None, tr, N_PROJ), lambda l, i: (l, i, 0)),
        compiler_params=_params("parallel", "parallel"),
        name="w_in_layout",
    )(jnp.swapaxes(w, 1, 2))


def _permute_w_qb(w):
    w = w.reshape(w.shape[0], MLA_HEADS, MLA_NOPE + MLA_ROPE)
    w = jnp.pad(w, ((0, 0), (0, 0), (0, MLA_HEAD_PAD - MLA_NOPE - MLA_ROPE)))
    return w.reshape(w.shape[0], MLA_HEADS * MLA_HEAD_PAD).astype(BF16)


def _permute_w_kvb(w):
    w = w.reshape(w.shape[0], MLA_HEADS, MLA_NOPE + MLA_V)
    return jnp.concatenate([w[:, :, :MLA_NOPE].reshape(w.shape[0], -1),
                            w[:, :, MLA_NOPE:].reshape(w.shape[0], -1)], axis=1).astype(BF16)


def _rope_tables(pos_row, pos_col, half):
    inv = ROPE_BASE ** (-jnp.arange(half, dtype=F32) / half)
    zeros = jnp.zeros((pos_row.shape[0], half), F32)
    cos, sin_a, sin_b = [], [], []
    for pos in (pos_row, pos_col):
        ang = pos.astype(F32)[:, None] * inv[None, :]
        c, s = jnp.cos(ang), jnp.sin(ang)
        cos += [c, c]
        sin_a += [-s, zeros]
        sin_b += [zeros, s]
    pad = jnp.zeros((pos_row.shape[0], 128 - 4 * half), F32)
    return tuple(jnp.concatenate(t + [pad], axis=1) for t in (cos, sin_a, sin_b))


def _block_diag_ones():
    i = np.arange(128)
    return jnp.asarray((i[:, None] // RWKV_HEAD) == (i[None, :] // RWKV_HEAD), BF16)


def _split_w(w):
    hi = w.astype(BF16)
    return hi, (w - hi.astype(F32)).astype(BF16)


def _lru_gate_weights(wa, ba, wi, bi):
    w = jnp.concatenate([wa[0], wi[0], wa[1], wi[1]], axis=-1)
    b = jnp.concatenate([t.reshape(LRU_BLOCKS, 1, LRU_BLOCK_W) for t in (ba[0], bi[0], ba[1], bi[1])], axis=-1)
    return _split_w(w) + (b,)


def _rwkv_weights(mu_prev, mu_next, w0, w2, a0, a2, g2, k_k, k_a, r_k, bd):
    def pad_mu(mu):
        return jnp.pad(mu, (0, 3584 - mu.shape[0])).reshape(1, 3584)

    def two_dir(w):
        z = jnp.zeros_like(w[0])
        return jnp.concatenate([jnp.concatenate([w[0], z], axis=1), jnp.concatenate([z, w[1]], axis=1)], axis=0)

    g2p = jnp.pad(g2, ((0, 256 - RWKV_GATE_LORA), (0, 0)))
    return (pad_mu(mu_prev), pad_mu(mu_next), w0.reshape(1, 2048), a0.reshape(1, 2048),
            *_split_w(two_dir(w2)), *_split_w(two_dir(a2)), *_split_w(g2p),
            k_k.reshape(1, 1024), k_a.reshape(1, 1024), r_k.reshape(1, 1024), bd)


def kernel(x, c, ctx, c_ctx, w_mod, b_mod, w_in, w_out, ln1_g, ln1_b, w_ff1, w_ff2, ln2_g, ln2_b, swa_sink, lru_conv_w, lru_conv_b, lru_wa, lru_ba, lru_wi, lru_bi, lru_lam, mla_q_norm, mla_kv_norm, mla_w_qb, mla_w_kvb, rwkv_mu_prev, rwkv_mu_next, rwkv_w0, rwkv_w2, rwkv_a0, rwkv_a2, rwkv_g2, rwkv_k_k, rwkv_k_a, rwkv_r_k, rwkv_ln_g, rwkv_ln_b):
    bsz, t_len, d = x.shape
    n_ctx = ctx.shape[1]
    depth = w_mod.shape[0]

    cc = jnp.zeros((8, d), F32).at[:bsz].set(c).at[bsz].set(c_ctx)
    mod = _modulation(cc, w_mod, b_mod)

    pos = jnp.arange(t_len, dtype=jnp.int32)
    row, col = pos // GRID_W, pos % GRID_W
    swa_tables = _rope_tables(row, col, SWA_HEAD_DIM // 4)
    mla_tables = _rope_tables(row, col, MLA_ROPE // 4)
    bd = _block_diag_ones()
    zeros_h = jnp.zeros((bsz, 1, GROUP_W), F32)
    zeros_s = jnp.zeros((bsz, RWKV_HEADS * RWKV_HEAD, RWKV_HEAD), F32)

    xl = x.reshape(bsz * t_len, d)
    xc = ctx.reshape(bsz * n_ctx, d)

    w_in_p = _permute_w_in(w_in)
    w_out_b = w_out.astype(BF16)
    w_ff1_b = w_ff1.astype(BF16)
    w_ff2_b = w_ff2.astype(BF16)

    for l in range(depth):
        with_ctx = l < depth - 1
        chunks = [mod[l, :, k * d:(k + 1) * d] for k in range(6)]
        lat = [m[:bsz][:, None, :] for m in chunks]
        cxm = [m[bsz:bsz + 1][:, None, :] for m in chunks]

        p = _ln_mod_matmul(xl, lat[0], lat[1], w_in_p, l, act=None, out_dtype=F32, tm=1024, tn=512)
        pc = _ln_mod_matmul(xc, cxm[0], cxm[1], w_in_p, l, act=None, out_dtype=F32)

        q_a, k_a, v_a = _swa_prep(p, t_len, swa_tables)
        qc_a, kc_a, vc_a = _swa_prep(pc, n_ctx, None)
        o_swa = _swa_attn(q_a, k_a, v_a, kc_a, vc_a, swa_sink[l], bsz, t_len, n_ctx)

        gate_w = _lru_gate_weights(lru_wa[l], lru_ba[l], lru_wi[l], lru_bi[l])
        lru_args = (lru_conv_w[l], lru_conv_b[l].reshape(1, -1), *gate_w, lru_lam[l])
        caf, cbf, cab, cbb = _lru_coef(pc, n_ctx, *lru_args)
        hc_f, hlast_f = _lru_scan(caf, cbf, zeros_h, bsz, n_ctx, reverse=False)
        oc_lru, hlast_b = _lru_scan(cab, cbb, zeros_h, bsz, n_ctx, reverse=True, h_other=hc_f, p=pc)
        laf, lbf, lab, lbb = _lru_coef(p, t_len, *lru_args)
        h_f, _ = _lru_scan(laf, lbf, hlast_f, bsz, t_len, reverse=False)
        o_lru, _ = _lru_scan(lab, lbb, hlast_b, bsz, t_len, reverse=True, h_other=h_f, p=p)

        wq = _permute_w_qb(mla_w_qb[l])
        wkv = _permute_w_kvb(mla_w_kvb[l])
        q_c, k_c, v_c = _mla_proj(p, t_len, mla_q_norm[l], mla_kv_norm[l], wq, wkv, mla_tables)
        qc_c, kc_c, vc_c = _mla_proj(pc, n_ctx, mla_q_norm[l], mla_kv_norm[l], wq, wkv, None)
        o_mla = _mla_flash(q_c, kc_c, vc_c, bsz, t_len, n_ctx, k=k_c, v=v_c)

        rw = _rwkv_weights(rwkv_mu_prev[l], rwkv_mu_next[l], rwkv_w0[l], rwkv_w2[l], rwkv_a0[l], rwkv_a2[l],
                           rwkv_g2[l], rwkv_k_k[l], rwkv_k_a[l], rwkv_r_k[l], bd)
        fc = _rwkv_feat(pc, n_ctx, rw)
        yc_f, yc_b, s_f, s_b = _wkv_scan(fc, zeros_s, zeros_s, bsz, n_ctx)
        fl = _rwkv_feat(p, t_len, rw)
        y_f, y_b, _, _ = _wkv_scan(fl, s_f, s_b, bsz, t_len)
        o_rwkv = _rwkv_out(y_f, y_b, fl[4], fl[3], rwkv_ln_g[l], rwkv_ln_b[l], bd, t_len)

        def tail(xin, mix, m):
            x1 = _matmul_res_ln(mix, w_out_b, l, xin, m[2], ln1_g[l], ln1_b[l], tm=512)
            hid = _ln_mod_matmul(x1, m[3], m[4], w_ff1_b, l, act="relu2", out_dtype=BF16, tm=1024)
            return _matmul_res_ln(hid, w_ff2_b, l, x1, m[5], ln2_g[l], ln2_b[l])

        if with_ctx:
            oc_swa = _swa_ctx_attn(qc_a, kc_a, vc_a, swa_sink[l], bsz, n_ctx)
            oc_mla = _mla_flash(qc_c, kc_c, vc_c, bsz, n_ctx, n_ctx)
            oc_rwkv = _rwkv_out(yc_f, yc_b, fc[4], fc[3], rwkv_ln_g[l], rwkv_ln_b[l], bd, n_ctx)
            xc = tail(xc, jnp.concatenate([oc_swa, oc_lru, oc_mla, oc_rwkv], axis=-1), cxm)

        xl = tail(xl, jnp.concatenate([o_swa, o_lru, o_mla, o_rwkv], axis=-1), lat)

    return xl.reshape(bsz, t_len, d)
```

```python
import functools

import numpy as np
import jax
import jax.numpy as jnp
from jax import lax
from jax.experimental import pallas as pl
from jax.experimental.pallas import tpu as pltpu

F32 = jnp.float32
BF16 = jnp.bfloat16

D_MODEL = 4096
GRID_W = 64
GROUP_W = 1024
D_FF = 4 * D_MODEL
BLOCK = 128

SWA_HEAD_DIM = 128
SWA_HEADS = 8
SWA_KV_HEADS = 2
SWA_GROUP = 4

LRU_BLOCKS = 8
LRU_BLOCK_W = 128
LRU_C = 8.0

MLA_HEADS = 8
MLA_NOPE = 128
MLA_ROPE = 64
MLA_V = 128
MLA_HEAD_PAD = 256

RWKV_HEAD = 64
RWKV_HEADS = 16
RWKV_GATE_LORA = 160
RWKV_GN_EPS = 64e-5
WKV_CHUNK = 64
WKV_INV_BLOCK = 16

ROPE_BASE = 10000.0
LN_EPS = 1e-5
RMS_EPS = 1e-6
DEPTH = 2
ALPHA = (2 * DEPTH) ** 0.25

OFF_AQ, OFF_BX, OFF_BG, OFF_CQ, OFF_DR, OFF_DK, OFF_DV = 0, 1024, 2048, 3072, 4096, 5120, 6144
OFF_CKV, OFF_AK, OFF_AV, OFF_DL, OFF_CR = 7168, 7680, 7936, 8192, 8704
N_PROJ = 9216
_PROJ_LAYOUT = ((0, 1024, 1024), (1536, 1024, 1024), (2560, 1024, 1024), (3584, 1024, 1024), (5184, 1024, 1024),
                (6208, 1024, 1024), (7232, 1024, 1024), (4608, 512, 512), (1024, 256, 256), (1280, 256, 256),
                (8256, 416, 512), (5120, 64, 128), (None, 0, 384))

VMEM_LIMIT_V7X = 56 * 1024 * 1024


def _params(*sem):
    return pltpu.CompilerParams(dimension_semantics=sem, vmem_limit_bytes=VMEM_LIMIT_V7X)


def _split2(a):
    hi = a.astype(BF16)
    lo = (a - hi.astype(F32)).astype(BF16)
    return hi, lo


def _mm(a, b):
    return jnp.dot(a, b, preferred_element_type=F32)


def _mm_nt(a, b):
    return lax.dot_general(a, b, (((1,), (1,)), ((), ())), preferred_element_type=F32)


def _mm_tn(a, b):
    return lax.dot_general(a, b, (((0,), (0,)), ((), ())), preferred_element_type=F32)


def _mm3(a, b, mm=_mm):
    ah, al = _split2(a)
    bh, bl = _split2(b)
    return mm(ah, bh) + (mm(al, bh) + mm(ah, bl))


def _mm3w(a, bh, bl):
    ah, al = _split2(a)
    return _mm(ah, bh) + (_mm(al, bh) + _mm(ah, bl))


def _mm1(a, b, mm=_mm):
    return mm(a.astype(BF16), b.astype(BF16))


def _seg_sum(x, bd):
    parts = []
    for j in range(x.shape[1] // 128):
        hi, lo = _split2(x[:, 128 * j:128 * (j + 1)])
        parts.append(_mm(hi, bd) + _mm(lo, bd))
    return jnp.concatenate(parts, axis=1)


def _layer_norm_rows(x):
    mu = jnp.mean(x, -1, keepdims=True)
    xc = x - mu
    var = jnp.mean(xc * xc, -1, keepdims=True)
    return xc * lax.rsqrt(var + LN_EPS)


def _softplus(z):
    return jnp.maximum(z, 0.0) + jnp.log1p(jnp.exp(-jnp.abs(z)))


def _gelu_tanh(x):
    return 0.5 * x * (1.0 + jnp.tanh(np.sqrt(2.0 / np.pi).astype(np.float32) * (x + 0.044715 * (x * x * x))))


def _rope(x, cos, sin_a, sin_b, half):
    return x * cos + pltpu.roll(x, 128 - half, 1) * sin_a + pltpu.roll(x, half, 1) * sin_b


def _shift_rows(x, row, tm, prev_rows, next_row):
    p6, p7 = prev_rows
    xm1 = jnp.where(row == 0, p7, pltpu.roll(x, 1, 0))
    xp1 = jnp.where(row == tm - 1, next_row, pltpu.roll(x, tm - 1, 0))
    xm2 = None
    if p6 is not None:
        xm2 = jnp.where(row == 0, p6, jnp.where(row == 1, p7, pltpu.roll(x, 2, 0)))
    return xm2, xm1, xp1


def _halo_specs(col, width, tm, m_rows):
    r8 = tm // 8
    last8 = m_rows // 8 - 1
    return [pl.BlockSpec((tm, width), lambda i: (i, col)),
            pl.BlockSpec((8, width), lambda i: (jnp.maximum(i * r8 - 1, 0), col)),
            pl.BlockSpec((8, width), lambda i: (jnp.minimum((i + 1) * r8, last8), col))]


def _full(shape):
    nd = len(shape)
    return pl.BlockSpec(shape, lambda *_: (0,) * nd)


def _mod_kernel(c_ref, w_ref, b_ref, o_ref):
    cc = c_ref[...]
    a = cc * jax.nn.sigmoid(cc)
    o_ref[...] = _mm3(a, w_ref[...]) + b_ref[...]


def _modulation(cc, w_mod, b_mod, tn=512):
    depth, d, n = w_mod.shape
    return pl.pallas_call(
        _mod_kernel,
        out_shape=jax.ShapeDtypeStruct((depth, 8, n), F32),
        grid=(depth, n // tn),
        in_specs=[pl.BlockSpec((8, d), lambda l, j: (0, 0)),
                  pl.BlockSpec((None, d, tn), lambda l, j: (l, 0, j)),
                  pl.BlockSpec((None, 1, tn), lambda l, j: (l, 0, j))],
        out_specs=pl.BlockSpec((None, 8, tn), lambda l, j: (l, 0, j)),
        compiler_params=_params("parallel", "parallel"),
        name="adaln_modulation",
    )(cc, w_mod, b_mod.reshape(depth, 1, n))


def _lnmm_kernel(x_ref, sh_ref, sc_ref, w_ref, o_ref, xn_ref, *, act, tm):
    @pl.when(pl.program_id(1) == 0)
    def _():
        scale = 1.0 + sc_ref[...]
        shift = sh_ref[...]
        rows = min(tm, 128)

        def body(r, carry):
            sl = pl.ds(pl.multiple_of(r * rows, rows), rows)
            xn_ref[sl, :] = (_layer_norm_rows(x_ref[sl, :]) * scale + shift).astype(BF16)
            return carry

        lax.fori_loop(0, tm // rows, body, 0)

    acc = _mm(xn_ref[...], w_ref[...])
    if act == "relu2":
        acc = jnp.maximum(acc, 0.0)
        acc = acc * acc
    o_ref[...] = acc.astype(o_ref.dtype)


def _ln_mod_matmul(x, shift, scale, w, layer, *, act, out_dtype, tm=512, tn=1024):
    m, k = x.shape
    n = w.shape[2]
    nbm = shift.shape[0]
    tm = min(tm, m // nbm)
    seq_tiles = m // nbm // tm
    mod_spec = pl.BlockSpec((None, 1, k), lambda i, j: (i // seq_tiles, 0, 0))
    x_spec = pl.BlockSpec((tm, k), lambda i, j: (i, 0), pipeline_mode=pl.Buffered(1))
    return pl.pallas_call(
        functools.partial(_lnmm_kernel, act=act, tm=tm),
        out_shape=jax.ShapeDtypeStruct((m, n), out_dtype),
        grid=(m // tm, n // tn),
        in_specs=[x_spec, mod_spec, mod_spec,
                  pl.BlockSpec((None, k, tn), lambda i, j: (layer, 0, j))],
        out_specs=pl.BlockSpec((tm, tn), lambda i, j: (i, j)),
        scratch_shapes=[pltpu.VMEM((tm, k), BF16)],
        compiler_params=_params("parallel", "arbitrary"),
        name="ln_mod_matmul_" + (act or "id"),
    )(x, shift, scale, w)


def _mmln_kernel(a_ref, w_ref, x_ref, ga_ref, g_ref, b_ref, o_ref, *, nk, tm):
    k = pl.program_id(1)

    @pl.when(k == 0)
    def _():
        o_ref[...] = _mm(a_ref[...], w_ref[...])

    @pl.when(k > 0)
    def _():
        o_ref[...] += _mm(a_ref[...], w_ref[...])

    @pl.when(k == nk - 1)
    def _():
        gate = ga_ref[...]
        g = g_ref[...]
        b = b_ref[...]
        rows = min(tm, 128)

        def body(r, carry):
            sl = pl.ds(pl.multiple_of(r * rows, rows), rows)
            z = ALPHA * x_ref[sl, :] + gate * o_ref[sl, :]
            o_ref[sl, :] = _layer_norm_rows(z) * g + b
            return carry

        lax.fori_loop(0, tm // rows, body, 0)


def _matmul_res_ln(a, w, layer, xres, gate, g, b, *, tm=1024, tk=512):
    m, kdim = a.shape
    n = w.shape[2]
    nbm = gate.shape[0]
    tm = min(tm, m // nbm)
    seq_tiles = m // nbm // tm
    nk = kdim // tk
    mode = dict(pipeline_mode=pl.Buffered(1)) if nk >= 16 else {}
    return pl.pallas_call(
        functools.partial(_mmln_kernel, nk=nk, tm=tm),
        out_shape=jax.ShapeDtypeStruct((m, n), F32),
        grid=(m // tm, nk),
        in_specs=[pl.BlockSpec((tm, tk), lambda i, k: (i, k)),
                  pl.BlockSpec((None, tk, n), lambda i, k: (layer, k, 0)),
                  pl.BlockSpec((tm, n), lambda i, k: (i, 0), **mode),
                  pl.BlockSpec((None, 1, n), lambda i, k: (i // seq_tiles, 0, 0)),
                  _full((1, n)), _full((1, n))],
        out_specs=pl.BlockSpec((tm, n), lambda i, k: (i, 0), **mode),
        compiler_params=_params("parallel", "arbitrary"),
        name="matmul_res_ln",
    )(a, w, xres, gate, g.reshape(1, n), b.reshape(1, n))


def _swa_prep_kernel(*refs, rotate):
    if rotate:
        q_ref, kv_ref, cos_ref, sa_ref, sb_ref, qo_ref, ko_ref, vo_ref = refs
        cos, sa, sb = cos_ref[...], sa_ref[...], sb_ref[...]
    else:
        q_ref, kv_ref, qo_ref, ko_ref, vo_ref = refs
    scale = np.float32(SWA_HEAD_DIM ** -0.5)
    for h in range(SWA_HEADS):
        sl = slice(128 * h, 128 * (h + 1))
        q = q_ref[:, sl]
        if rotate:
            q = _rope(q, cos, sa, sb, 32)
        qo_ref[:, sl] = (q * scale).astype(BF16)
    for h in range(SWA_KV_HEADS):
        sl = slice(128 * h, 128 * (h + 1))
        k = kv_ref[:, sl]
        if rotate:
            k = _rope(k, cos, sa, sb, 32)
        ko_ref[:, sl] = k.astype(BF16)
    vo_ref[...] = kv_ref[:, 256:512].astype(BF16)


def _swa_prep(p, t_len, tables, tm=256):
    m = p.shape[0]
    tm = min(tm, t_len)
    seq_tiles = t_len // tm
    rotate = tables is not None
    in_specs = [pl.BlockSpec((tm, 1024), lambda i: (i, OFF_AQ // 1024)),
                pl.BlockSpec((tm, 512), lambda i: (i, OFF_AK // 512))]
    args = [p, p]
    if rotate:
        in_specs += [pl.BlockSpec((tm, 128), lambda i: (i % seq_tiles, 0))] * 3
        args += list(tables)
    return pl.pallas_call(
        functools.partial(_swa_prep_kernel, rotate=rotate),
        out_shape=(jax.ShapeDtypeStruct((m, 1024), BF16), jax.ShapeDtypeStruct((m, 256), BF16),
                   jax.ShapeDtypeStruct((m, 256), BF16)),
        grid=(m // tm,),
        in_specs=in_specs,
        out_specs=(pl.BlockSpec((tm, 1024), lambda i: (i, 0)), pl.BlockSpec((tm, 256), lambda i: (i, 0)),
                   pl.BlockSpec((tm, 256), lambda i: (i, 0))),
        compiler_params=_params("parallel"),
        name="swa_prep_rope" if rotate else "swa_prep",
    )(*args)


def _sink_softmax_pv(s, sink, v):
    m = jnp.maximum(jnp.max(s, -1, keepdims=True), sink)
    p = jnp.exp(s - m)
    den = jnp.sum(p, -1, keepdims=True) + jnp.exp(sink - m)
    return _mm(p.astype(BF16), v) / den


def _swa_kernel(q_ref, kp_ref, ko_ref, kn_ref, vp_ref, vo_ref, vn_ref, kc_ref, vc_ref, sink_ref, o_ref, *, nb, n_ctx):
    n = pl.program_id(1)
    qi = lax.broadcasted_iota(jnp.int32, (BLOCK, BLOCK), 0)
    kj = lax.broadcasted_iota(jnp.int32, (BLOCK, BLOCK), 1)
    neg = np.float32(-np.inf)
    bias_prev = jnp.where(jnp.logical_and(kj >= qi, n > 0), 0.0, neg)
    bias_next = jnp.where(jnp.logical_and(kj <= qi, n < nb - 1), 0.0, neg)
    bias = jnp.concatenate([bias_prev, jnp.zeros((BLOCK, BLOCK), F32), bias_next,
                            jnp.zeros((BLOCK, n_ctx), F32)], axis=1)
    kv_sl = [slice(128 * hh, 128 * (hh + 1)) for hh in range(SWA_KV_HEADS)]
    k_all = [jnp.concatenate([kp_ref[:, sl], ko_ref[:, sl], kn_ref[:, sl], kc_ref[:, sl]], axis=0) for sl in kv_sl]
    v_all = [jnp.concatenate([vp_ref[:, sl], vo_ref[:, sl], vn_ref[:, sl], vc_ref[:, sl]], axis=0) for sl in kv_sl]
    heads = range(SWA_HEADS)
    s = [_mm_nt(q_ref[:, 128 * h:128 * (h + 1)], k_all[h // SWA_GROUP]) + bias for h in heads]
    sink = [sink_ref[0:1, h:h + 1] for h in heads]
    m = [jnp.maximum(jnp.max(s[h], -1, keepdims=True), sink[h]) for h in heads]
    p = [jnp.exp(s[h] - m[h]) for h in heads]
    den = [jnp.sum(p[h], -1, keepdims=True) + jnp.exp(sink[h] - m[h]) for h in heads]
    o = [_mm(p[h].astype(BF16), v_all[h // SWA_GROUP]) for h in heads]
    for h in heads:
        o_ref[:, 128 * h:128 * (h + 1)] = (o[h] / den[h]).astype(BF16)


def _swa_attn(q, k, v, kc, vc, sink, bsz, t_len, n_ctx, out_width=GROUP_W):
    nb = t_len // BLOCK

    def blk(width, off):
        def idx(b, n):
            return (b * nb + jnp.clip(n + off, 0, nb - 1), 0)
        return pl.BlockSpec((BLOCK, width), idx)

    ctx_spec = pl.BlockSpec((n_ctx, 256), lambda b, n: (b, 0))
    return pl.pallas_call(
        functools.partial(_swa_kernel, nb=nb, n_ctx=n_ctx),
        out_shape=jax.ShapeDtypeStruct((bsz * t_len, out_width), BF16),
        grid=(bsz, nb),
        in_specs=[blk(1024, 0), blk(256, -1), blk(256, 0), blk(256, 1), blk(256, -1), blk(256, 0), blk(256, 1),
                  ctx_spec, ctx_spec, _full((1, SWA_HEADS))],
        out_specs=blk(1024, 0),
        compiler_params=_params("parallel", "parallel"),
        name="swa_attention",
    )(q, k, k, k, v, v, v, kc, vc, sink.reshape(1, SWA_HEADS))


def _swa_ctx_kernel(q_ref, kc_ref, vc_ref, sink_ref, o_ref):
    for hh in range(SWA_KV_HEADS):
        sl = slice(128 * hh, 128 * (hh + 1))
        for g in range(SWA_GROUP):
            h = hh * SWA_GROUP + g
            hs = slice(128 * h, 128 * (h + 1))
            s = _mm_nt(q_ref[:, hs], kc_ref[:, sl])
            o_ref[:, hs] = _sink_softmax_pv(s, sink_ref[0:1, h:h + 1], vc_ref[:, sl]).astype(BF16)


def _swa_ctx_attn(qc, kc, vc, sink, bsz, n_ctx):
    return pl.pallas_call(
        _swa_ctx_kernel,
        out_shape=jax.ShapeDtypeStruct((bsz * n_ctx, 1024), BF16),
        grid=(bsz,),
        in_specs=[pl.BlockSpec((n_ctx, 1024), lambda b: (b, 0)), pl.BlockSpec((n_ctx, 256), lambda b: (b, 0)),
                  pl.BlockSpec((n_ctx, 256), lambda b: (b, 0)), _full((1, SWA_HEADS))],
        out_specs=pl.BlockSpec((n_ctx, 1024), lambda b: (b, 0)),
        compiler_params=_params("parallel"),
        name="swa_ctx_attention",
    )(qc, kc, vc, sink.reshape(1, SWA_HEADS))


def _lru_coef_kernel(x_ref, xp_ref, xn_ref, cw_ref, cb_ref, wh_ref, wl_ref, bg_ref, lam_ref,
                     af_ref, bf_ref, ab_ref, bb_ref, *, tm, seq_tiles):
    i = pl.program_id(0) % seq_tiles
    first = i == 0
    last = i == seq_tiles - 1
    x = x_ref[...]
    row = lax.broadcasted_iota(jnp.int32, x.shape, 0)
    p6 = jnp.where(first, 0.0, xp_ref[6:7, :])
    p7 = jnp.where(first, 0.0, xp_ref[7:8, :])
    n0 = jnp.where(last, 0.0, xn_ref[0:1, :])
    xm2, xm1, xp1 = _shift_rows(x, row, tm, (p6, p7), n0)
    u = (cw_ref[0:1, :] * xm2 + cw_ref[1:2, :] * xm1 + cw_ref[2:3, :] * x + cw_ref[3:4, :] * xp1) + cb_ref[...]
    sp = _softplus(-lam_ref[...])
    outs = ((af_ref, bf_ref), (ab_ref, bb_ref))
    for n in range(LRU_BLOCKS):
        sl = slice(128 * n, 128 * (n + 1))
        un = u[:, sl]
        z = _mm3w(un, wh_ref[n], wl_ref[n]) + bg_ref[n]
        for d in range(2):
            r = jax.nn.sigmoid(z[:, 256 * d:256 * d + 128])
            gi = jax.nn.sigmoid(z[:, 256 * d + 128:256 * d + 256])
            log_a = (-LRU_C) * r * sp[d:d + 1, sl]
            a_ref, b_ref = outs[d]
            a_ref[:, sl] = jnp.exp(log_a)
            b_ref[:, sl] = jnp.sqrt(1.0 - jnp.exp(2.0 * log_a)) * (gi * un)


def _lru_coef(p, t_len, conv_w, conv_b, wg_hi, wg_lo, bg, lam, tm=256):
    m = p.shape[0]
    tm = min(tm, t_len)
    seq_tiles = t_len // tm
    out = jax.ShapeDtypeStruct((m, 1024), F32)
    ospec = pl.BlockSpec((tm, 1024), lambda i: (i, 0))
    return pl.pallas_call(
        functools.partial(_lru_coef_kernel, tm=tm, seq_tiles=seq_tiles),
        out_shape=(out, out, out, out),
        grid=(m // tm,),
        in_specs=_halo_specs(OFF_BX // 1024, 1024, tm, m) + [
            _full((4, 1024)), _full((1, 1024)), _full((8, 128, 512)), _full((8, 128, 512)),
            _full((8, 1, 512)), _full((2, 1024))],
        out_specs=(ospec, ospec, ospec, ospec),
        compiler_params=_params("parallel"),
        name="lru_coefficients",
    )(p, p, p, conv_w, conv_b, wg_hi, wg_lo, bg, lam)


def _lru_scan_kernel(*refs, reverse, final, nc, tc, aliased):
    if aliased:
        refs = refs[:5] + refs[6:]
    if final:
        a_ref, b_ref, h0_ref, hf_ref, gate_ref, o_ref, hl_ref, hs_ref, hrows_ref = refs
    else:
        a_ref, b_ref, h0_ref, o_ref, hl_ref, hs_ref = refs
        hrows_ref = o_ref
    c = pl.program_id(0)

    @pl.when(c == 0)
    def _():
        hs_ref[...] = h0_ref[...]

    def body(t, h):
        tt = (tc - 1 - t) if reverse else t
        h = a_ref[:, pl.ds(tt, 1), :] * h + b_ref[:, pl.ds(tt, 1), :]
        hrows_ref[:, pl.ds(tt, 1), :] = h
        return h

    h = lax.fori_loop(0, tc, body, hs_ref[...], unroll=8)
    hs_ref[...] = h

    @pl.when(c == (0 if reverse else nc - 1))
    def _():
        hl_ref[...] = hrows_ref[:, tc - 1:tc, :]

    if final:
        o_ref[...] = ((hf_ref[...] + hrows_ref[...]) * _gelu_tanh(gate_ref[...])).astype(BF16)


def _lru_scan(a, b, h0, bsz, t_len, *, reverse, h_other=None, p=None, mix=None, tc=256):
    tc = min(tc, t_len)
    nc = t_len // tc
    final = h_other is not None
    aliased = mix is not None

    def chunk(c):
        return (nc - 1 - c) if reverse else c

    def seq(t):
        return t.reshape(bsz, t_len, t.shape[-1])

    rows = pl.BlockSpec((bsz, tc, 1024), lambda c: (0, chunk(c), 0))
    state = _full((bsz, 1, 1024))
    in_specs = [rows, rows, state]
    args = [seq(a), seq(b), h0]
    scratch = [pltpu.VMEM((bsz, 1, 1024), F32)]
    if final:
        in_specs += [rows, pl.BlockSpec((bsz, tc, 1024), lambda c: (0, chunk(c), OFF_BG // 1024))]
        args += [seq(h_other), seq(p)]
        scratch.append(pltpu.VMEM((bsz, tc, 1024), F32))
    out_rows, out_width, aliases = rows, 1024, {}
    if aliased:
        in_specs.append(pl.BlockSpec(memory_space=pl.ANY))
        args.append(seq(mix))
        out_rows = pl.BlockSpec((bsz, tc, 1024), lambda c: (0, chunk(c), 1))
        out_width, aliases = mix.shape[-1], {len(args) - 1: 0}
    out, h_last = pl.pallas_call(
        functools.partial(_lru_scan_kernel, reverse=reverse, final=final, nc=nc, tc=tc, aliased=aliased),
        out_shape=(jax.ShapeDtypeStruct((bsz, t_len, out_width), BF16 if final else F32),
                   jax.ShapeDtypeStruct((bsz, 1, 1024), F32)),
        grid=(nc,),
        in_specs=in_specs,
        out_specs=(out_rows, state),
        scratch_shapes=scratch,
        input_output_aliases=aliases,
        compiler_params=_params("arbitrary"),
        name="lru_scan_" + ("bwd_out" if final else ("bwd" if reverse else "fwd")),
    )(*args)
    return out.reshape(bsz * t_len, out_width), h_last


def _mla_proj_kernel(*refs, rotate):
    if rotate:
        (qa_ref, kva_ref, kr_ref, qn_ref, kvn_ref, wq_ref, wkv_ref, cos_ref, sa_ref, sb_ref,
         q_ref, k_ref, v_ref) = refs
        cos, sa, sb = cos_ref[...], sa_ref[...], sb_ref[...]
    else:
        qa_ref, kva_ref, kr_ref, qn_ref, kvn_ref, wq_ref, wkv_ref, q_ref, k_ref, v_ref = refs

    def rms(x, g):
        return (x * lax.rsqrt(jnp.mean(x * x, -1, keepdims=True) + RMS_EPS) * g).astype(BF16)

    scale = np.float32((MLA_NOPE + MLA_ROPE) ** -0.5 * np.log2(np.e))
    q = _mm(rms(qa_ref[...], qn_ref[...]), wq_ref[...]) * scale
    kv = _mm(rms(kva_ref[...], kvn_ref[...]), wkv_ref[...])
    kr = kr_ref[...]
    if rotate:
        kr = _rope(kr, cos, sa, sb, 16)
    kr = kr.astype(BF16)
    lane = lax.broadcasted_iota(jnp.int32, kr.shape, 1)
    ones_col = jnp.where(lane == 0, 1.0, 0.0).astype(BF16)
    for h in range(MLA_HEADS):
        lo = MLA_HEAD_PAD * h
        q_ref[:, lo:lo + 128] = q[:, lo:lo + 128].astype(BF16)
        qr = q[:, lo + 128:lo + 256]
        if rotate:
            qr = _rope(qr, cos, sa, sb, 16)
        q_ref[:, lo + 128:lo + 256] = qr.astype(BF16)
        k_ref[:, lo:lo + 128] = kv[:, 128 * h:128 * (h + 1)].astype(BF16)
        k_ref[:, lo + 128:lo + 256] = kr
        v_ref[:, lo:lo + 128] = kv[:, 1024 + 128 * h:1024 + 128 * (h + 1)].astype(BF16)
        v_ref[:, lo + 128:lo + 256] = ones_col


def _mla_proj(p, t_len, q_norm, kv_norm, wq, wkv, tables, tm=256):
    m = p.shape[0]
    tm = min(tm, t_len)
    seq_tiles = t_len // tm
    rotate = tables is not None
    in_specs = [pl.BlockSpec((tm, 1024), lambda i: (i, OFF_CQ // 1024)),
                pl.BlockSpec((tm, 512), lambda i: (i, OFF_CKV // 512)),
                pl.BlockSpec((tm, 128), lambda i: (i, OFF_CR // 128)),
                _full((1, 1024)), _full((1, 512)), _full((1024, 2048)), _full((512, 2048))]
    args = [p, p, p, q_norm.reshape(1, -1), kv_norm.reshape(1, -1), wq, wkv]
    if rotate:
        in_specs += [pl.BlockSpec((tm, 128), lambda i: (i % seq_tiles, 0))] * 3
        args += list(tables)
    wide = pl.BlockSpec((tm, 2048), lambda i: (i, 0))
    return pl.pallas_call(
        functools.partial(_mla_proj_kernel, rotate=rotate),
        out_shape=(jax.ShapeDtypeStruct((m, 2048), BF16),) * 3,
        grid=(m // tm,),
        in_specs=in_specs,
        out_specs=(wide, wide, wide),
        compiler_params=_params("parallel"),
        name="mla_project_rope" if rotate else "mla_project",
    )(*args)


def _mla_flash_kernel(*refs, with_lat, n_chunks, ck, aliased):
    if aliased:
        refs = refs[:-2] + refs[-1:]
    if with_lat:
        q_ref, kc_ref, vc_ref, k_ref, v_ref, o_ref = refs
    else:
        q_ref, kc_ref, vc_ref, o_ref = refs
    q = q_ref[...]
    chunks = [(kc_ref, vc_ref, slice(None))]
    if with_lat:
        chunks += [(k_ref, v_ref, slice(j * ck, (j + 1) * ck)) for j in range(n_chunks)]
    k0, _, sl0 = chunks[0]
    s_cur = _mm_nt(q, k0[sl0, :])
    m = acc = pending = s_next = None
    for j, (_, vj_ref, slj) in enumerate(chunks):
        if j + 1 < len(chunks):
            kn_ref, _, sln = chunks[j + 1]
            s_next = _mm_nt(q, kn_ref[sln, :])
        if pending is not None:
            p_prev, v_prev, alpha_prev = pending
            pv = _mm(p_prev, v_prev)
            acc = pv if acc is None else alpha_prev * acc + pv
        mx = jnp.max(s_cur, -1, keepdims=True)
        m_new = mx if m is None else jnp.maximum(m, mx)
        alpha = None if m is None else jnp.exp2(m - m_new)
        pending = (jnp.exp2((s_cur - m_new).astype(BF16)), vj_ref[slj, :], alpha)
        m = m_new
        s_cur = s_next
    p_prev, v_prev, alpha_prev = pending
    pv = _mm(p_prev, v_prev)
    acc = pv if acc is None else alpha_prev * acc + pv
    o_ref[...] = (acc[:, 0:MLA_V] / acc[:, MLA_V:MLA_V + 1]).astype(BF16)


def _mla_flash(q, kc, vc, bsz, tq_len, n_ctx, k=None, v=None, mix=None, tq=1024, ck=512):
    with_lat = k is not None
    aliased = mix is not None
    tq = min(tq, tq_len)
    nq = tq_len // tq
    in_specs = [pl.BlockSpec((tq, MLA_HEAD_PAD), lambda b, h, i: (b * nq + i, h)),
                pl.BlockSpec((n_ctx, MLA_HEAD_PAD), lambda b, h, i: (b, h)),
                pl.BlockSpec((n_ctx, MLA_HEAD_PAD), lambda b, h, i: (b, h))]
    args = [q, kc, vc]
    n_chunks = 0
    if with_lat:
        t_len = k.shape[0] // bsz
        ck = min(ck, t_len)
        n_chunks = t_len // ck
        in_specs += [pl.BlockSpec((t_len, MLA_HEAD_PAD), lambda b, h, i: (b, h)),
                     pl.BlockSpec((t_len, MLA_HEAD_PAD), lambda b, h, i: (b, h))]
        args += [k, v]
    out_width, col0, aliases = GROUP_W, 0, {}
    if aliased:
        in_specs.append(pl.BlockSpec(memory_space=pl.ANY))
        args.append(mix)
        out_width, col0, aliases = mix.shape[-1], 2 * GROUP_W // MLA_V, {len(args) - 1: 0}
    return pl.pallas_call(
        functools.partial(_mla_flash_kernel, with_lat=with_lat, n_chunks=n_chunks, ck=ck, aliased=aliased),
        out_shape=jax.ShapeDtypeStruct((bsz * tq_len, out_width), BF16),
        grid=(bsz, MLA_HEADS, nq),
        in_specs=in_specs,
        out_specs=pl.BlockSpec((tq, MLA_V), lambda b, h, i: (b * nq + i, col0 + h)),
        input_output_aliases=aliases,
        compiler_params=_params("parallel", "parallel", "parallel"),
        name="mla_flash" if with_lat else "mla_ctx_attention",
    )(*args)


def _rwkv_feat_kernel(xr_ref, xrp_ref, xrn_ref, xk_ref, xkp_ref, xkn_ref, xv_ref, xvp_ref, xvn_ref,
                      xl_ref, xlp_ref, xln_ref, mup_ref, mun_ref, w0_ref, a0_ref,
                      w2h_ref, w2l_ref, a2h_ref, a2l_ref, g2h_ref, g2l_ref, kk_w_ref, ka_ref, rk_ref, bd_ref,
                      r_o, v_o, kk_o, g_o, bonus_o, lwf_o, kf_o, bf_o, lwb_o, kb_o, bb_o, *, tm, seq_tiles):
    i = pl.program_id(0) % seq_tiles
    first = i == 0
    last = i == seq_tiles - 1

    def shifted(x_ref, xp_ref, xn_ref, lo, hi):
        x = x_ref[...]
        row = lax.broadcasted_iota(jnp.int32, x.shape, 0)
        p7 = jnp.where(first, 0.0, xp_ref[7:8, :])
        n0 = jnp.where(last, 0.0, xn_ref[0:1, :])
        _, xm1, xp1 = _shift_rows(x, row, tm, (None, p7), n0)
        return x + mup_ref[:, lo:hi] * (xm1 - x) + mun_ref[:, lo:hi] * (xp1 - x)

    r = shifted(xr_ref, xrp_ref, xrn_ref, 0, 1024)
    k = shifted(xk_ref, xkp_ref, xkn_ref, 1024, 2048)
    v = shifted(xv_ref, xvp_ref, xvn_ref, 2048, 3072)
    lo = shifted(xl_ref, xlp_ref, xln_ref, 3072, 3584)
    bd = bd_ref[...]

    kkr = k * kk_w_ref[...]
    kk = kkr / jnp.maximum(jnp.sqrt(_seg_sum(kkr * kkr, bd)), 1e-12)
    g = _mm3w(jax.nn.sigmoid(lo[:, 256:512]), g2h_ref[...], g2l_ref[...])
    wl = w0_ref[...] + _mm3w(jnp.tanh(lo[:, 0:128]), w2h_ref[...], w2l_ref[...])
    lw = np.float32(-np.exp(-0.5)) * jax.nn.sigmoid(wl)
    a = jax.nn.sigmoid(a0_ref[...] + _mm3w(lo[:, 128:256], a2h_ref[...], a2l_ref[...]))
    r_o[...] = r
    v_o[...] = v
    kk_o[...] = kk
    g_o[...] = g
    bonus = None
    for d, (lw_o, k_o, b_o) in enumerate(((lwf_o, kf_o, bf_o), (lwb_o, kb_o, bb_o))):
        a_d = a[:, 1024 * d:1024 * (d + 1)]
        k_d = k * (1.0 + (a_d - 1.0) * ka_ref[...])
        bo = _seg_sum(r * k_d * rk_ref[...], bd) * v
        bonus = bo if bonus is None else bonus + bo
        lw_o[...] = lw[:, 1024 * d:1024 * (d + 1)]
        k_o[...] = k_d
        b_o[...] = kk * a_d
    bonus_o[...] = bonus


def _rwkv_feat(p, t_len, wts, tm=256):
    m = p.shape[0]
    tm = min(tm, t_len)
    seq_tiles = t_len // tm
    in_specs = (_halo_specs(OFF_DR // 1024, 1024, tm, m) + _halo_specs(OFF_DK // 1024, 1024, tm, m)
                + _halo_specs(OFF_DV // 1024, 1024, tm, m) + _halo_specs(OFF_DL // 512, 512, tm, m)
                + [_full(w.shape) for w in wts])
    out = jax.ShapeDtypeStruct((m, 1024), F32)
    ospec = pl.BlockSpec((tm, 1024), lambda i: (i, 0))
    return pl.pallas_call(
        functools.partial(_rwkv_feat_kernel, tm=tm, seq_tiles=seq_tiles),
        out_shape=(out,) * 11,
        grid=(m // tm,),
        in_specs=in_specs,
        out_specs=(ospec,) * 11,
        compiler_params=_params("parallel"),
        name="rwkv_features",
    )(*([p] * 12), *wts)


def _wkv_operands(r_ref, v_ref, kk_ref, lw_ref, k_ref, b_ref, incl, reverse):
    cl = WKV_CHUNK
    tri = jnp.where(incl, 1.0, 0.0).astype(BF16)
    lw = lw_ref[...]
    l1 = lw.astype(BF16)
    rem = lw - l1.astype(F32)
    l2 = rem.astype(BF16)
    l3 = (rem - l2.astype(F32)).astype(BF16)
    cum = _mm(tri, l1) + (_mm(tri, l2) + _mm(tri, l3))
    total = cum[0:1, :] if reverse else cum[cl - 1:cl, :]
    e_k = jnp.exp(-cum)
    e_t = jnp.exp(total - cum)
    k_raw = k_ref[...]
    b_raw = b_ref[...]
    return dict(d_c=jnp.exp(total),
                r_t=(r_ref[...] * jnp.exp(cum)).astype(BF16),
                kk_t=(kk_ref[...] * jnp.exp(cum - lw)).astype(BF16),
                k_t=(k_raw * e_k).astype(BF16), b_t=(b_raw * e_k).astype(BF16),
                k_d=(k_raw * e_t).astype(BF16), b_d=(b_raw * e_t).astype(BF16),
                v=v_ref[...].astype(BF16))


def _wkv_kernel(rf_ref, vf_ref, kkf_ref, lwf_ref, kf_ref, bf_ref, rb_ref, vb_ref, kkb_ref, lwb_ref, kb_ref, bb_ref,
                s0f_ref, s0b_ref, yf_ref, yb_ref, soutf_ref, soutb_ref, sf_ref, sb_ref, *, nc):
    c = pl.program_id(1)
    cl = WKV_CHUNK

    @pl.when(c == 0)
    def _():
        sf_ref[...] = s0f_ref[...]
        sb_ref[...] = s0b_ref[...]

    ti = lax.broadcasted_iota(jnp.int32, (cl, cl), 0)
    si = lax.broadcasted_iota(jnp.int32, (cl, cl), 1)
    blk = (ti // WKV_INV_BLOCK) == (si // WKV_INV_BLOCK)
    masks = ((si <= ti, si < ti), (si >= ti, si > ti))
    ops = (_wkv_operands(rf_ref, vf_ref, kkf_ref, lwf_ref, kf_ref, bf_ref, masks[0][0], False),
           _wkv_operands(rb_ref, vb_ref, kkb_ref, lwb_ref, kb_ref, bb_ref, masks[1][0], True))
    s_all = (sf_ref[...], sb_ref[...])

    chains = [(d, slice(RWKV_HEAD * h, RWKV_HEAD * (h + 1))) for d in range(2) for h in range(RWKV_HEADS)]
    heads = range(len(chains))
    incl = [masks[d][0] for d, _ in chains]
    strict = [masks[d][1] for d, _ in chains]

    def lanes(name):
        return [ops[d][name][:, sl] for d, sl in chains]

    v_h, k_dh, b_dh, d_ch = lanes("v"), lanes("k_d"), lanes("b_d"), lanes("d_c")
    s0 = [s_all[d][sl, :] for d, sl in chains]
    left = [jnp.concatenate([kk, r], axis=0) for kk, r in zip(lanes("kk_t"), lanes("r_t"))]
    right = [jnp.concatenate([k, b], axis=0) for k, b in zip(lanes("k_t"), lanes("b_t"))]
    a = [_mm_nt(left[h], right[h]) for h in heads]
    ls = [_mm_nt(left[h], s0[h].astype(BF16)) for h in heads]
    a_l = [jnp.where(strict[h], a[h][0:cl, cl:2 * cl], 0.0) for h in heads]
    a_v = [jnp.concatenate([jnp.where(strict[h], a[h][0:cl, 0:cl], 0.0),
                            jnp.where(incl[h], a[h][cl:2 * cl, 0:cl], 0.0)], axis=0).astype(BF16) for h in heads]
    a_rb = [jnp.where(incl[h], a[h][cl:2 * cl, cl:2 * cl], 0.0).astype(BF16) for h in heads]
    av = [_mm(a_v[h], v_h[h]) for h in heads]
    dg = [jnp.where(blk, a_l[h], 0.0) for h in heads]
    off = [a_l[h] - dg[h] for h in heads]
    d2 = [_mm1(dg[h], dg[h]) for h in heads]
    d4 = [_mm1(d2[h], d2[h]) for h in heads]
    x = [d2[h] - dg[h] - _mm1(dg[h], d2[h]) for h in heads]
    d8 = [_mm1(d4[h], d4[h]) for h in heads]
    x = [x[h] + d4[h] + _mm1(x[h], d4[h]) for h in heads]
    t16 = [x[h] + d8[h] + _mm1(x[h], d8[h]) for h in heads]
    n1 = [off[h] + _mm1(t16[h], off[h]) for h in heads]
    n2 = [_mm1(n1[h], n1[h]) for h in heads]
    y1 = [t16[h] - n1[h] - _mm1(n1[h], t16[h]) for h in heads]
    t_m = [y1[h] + n2[h] + _mm1(n2[h], y1[h]) for h in heads]
    rhs = [ls[h][0:cl] + av[h][0:cl] for h in heads]
    u = [rhs[h] + _mm1(t_m[h], rhs[h]) for h in heads]
    vu = [jnp.concatenate([v_h[h], (-u[h]).astype(BF16)], axis=0) for h in heads]
    kb = [jnp.concatenate([k_dh[h], b_dh[h]], axis=0) for h in heads]
    states = [s0[h] * d_ch[h] + _mm_tn(vu[h], kb[h]) for h in heads]
    ys = [ls[h][cl:2 * cl] + av[h][cl:2 * cl] - _mm(a_rb[h], u[h].astype(BF16)) for h in heads]
    yf_ref[...] = jnp.concatenate(ys[:RWKV_HEADS], axis=1)
    yb_ref[...] = jnp.concatenate(ys[RWKV_HEADS:], axis=1)
    sf_ref[...] = jnp.concatenate(states[:RWKV_HEADS], axis=0)
    sb_ref[...] = jnp.concatenate(states[RWKV_HEADS:], axis=0)

    @pl.when(c == nc - 1)
    def _():
        soutf_ref[...] = sf_ref[...]
        soutb_ref[...] = sb_ref[...]


def _wkv_scan(feats, s0_f, s0_b, bsz, t_len):
    r, v, kk, _, _, lw_f, k_f, b_f, lw_b, k_b, b_b = feats
    nc = t_len // WKV_CHUNK
    rows_f = pl.BlockSpec((WKV_CHUNK, GROUP_W), lambda bi, c: (bi * nc + c, 0))
    rows_b = pl.BlockSpec((WKV_CHUNK, GROUP_W), lambda bi, c: (bi * nc + nc - 1 - c, 0))
    state = pl.BlockSpec((None, GROUP_W, RWKV_HEAD), lambda bi, c: (bi, 0, 0))
    y_shape = jax.ShapeDtypeStruct((bsz * t_len, GROUP_W), F32)
    s_shape = jax.ShapeDtypeStruct((bsz, GROUP_W, RWKV_HEAD), F32)
    return pl.pallas_call(
        functools.partial(_wkv_kernel, nc=nc),
        out_shape=(y_shape, y_shape, s_shape, s_shape),
        grid=(bsz, nc),
        in_specs=[rows_f] * 6 + [rows_b] * 6 + [state, state],
        out_specs=(rows_f, rows_b, state, state),
        scratch_shapes=[pltpu.VMEM((GROUP_W, RWKV_HEAD), F32)] * 2,
        compiler_params=_params("parallel", "arbitrary"),
        name="wkv7_chunked",
    )(r, v, kk, lw_f, k_f, b_f, r, v, kk, lw_b, k_b, b_b, s0_f, s0_b)


def _rwkv_out_kernel(yf_ref, yb_ref, bonus_ref, g_ref, lng_ref, lnb_ref, bd_ref, o_ref):
    bd = bd_ref[...]
    y = yf_ref[...] + yb_ref[...]
    inv_n = np.float32(1.0 / RWKV_HEAD)
    yc = y - _seg_sum(y, bd) * inv_n
    var = _seg_sum(yc * yc, bd) * inv_n
    yn = yc * lax.rsqrt(var + RWKV_GN_EPS) * lng_ref[...] + lnb_ref[...]
    o_ref[...] = ((yn + bonus_ref[...]) * g_ref[...]).astype(BF16)


def _rwkv_out(y_f, y_b, bonus, g, ln_g, ln_b, bd, t_len, mix=None, tm=256):
    m = y_f.shape[0]
    tm = min(tm, t_len)
    rows = pl.BlockSpec((tm, 1024), lambda i: (i, 0))
    in_specs = [rows] * 4 + [_full((1, 1024)), _full((1, 1024)), _full((128, 128))]
    args = [y_f, y_b, bonus, g, ln_g.reshape(1, -1), ln_b.reshape(1, -1), bd]
    out_rows, out_width, aliases, body = rows, GROUP_W, {}, _rwkv_out_kernel
    if mix is not None:
        in_specs.append(pl.BlockSpec(memory_space=pl.ANY))
        args.append(mix)
        out_rows = pl.BlockSpec((tm, 1024), lambda i: (i, 3))
        out_width, aliases = mix.shape[-1], {len(args) - 1: 0}

        def body(*refs):
            _rwkv_out_kernel(*refs[:-2], refs[-1])

    return pl.pallas_call(
        body,
        out_shape=jax.ShapeDtypeStruct((m, out_width), BF16),
        grid=(m // tm,),
        in_specs=in_specs,
        out_specs=out_rows,
        input_output_aliases=aliases,
        compiler_params=_params("parallel"),
        name="rwkv_groupnorm_gate",
    )(*args)


def _w_in_layout_kernel(wt_ref, o_ref):
    tr = o_ref.shape[0]
    dst = 0
    for src, width, padded in _PROJ_LAYOUT:
        step = min(padded, 256)
        for off in range(0, padded, step):
            piece = min(step, padded - off)
            real = 0 if src is None else max(0, min(piece, width - off))
            if real == 0:
                block = jnp.zeros((tr, piece), BF16)
            else:
                rows = wt_ref[src + off:src + off + real, :]
                if real < piece:
                    rows = jnp.concatenate([rows, jnp.zeros((piece - real, tr), F32)], axis=0)
                block = rows.T.astype(BF16)
            o_ref[:, dst + off:dst + off + piece] = block
        dst += padded


def _permute_w_in(w, tr=256):
    depth, rows, cols = w.shape
    return pl.pallas_call(
        _w_in_layout_kernel,
        out_shape=jax.ShapeDtypeStruct((depth, rows, N_PROJ), BF16),
        grid=(depth, rows // tr),
        in_specs=[pl.BlockSpec((None, cols, tr), lambda l, i: (l, 0, i))],
        out_specs=pl.BlockSpec((None, tr, N_PROJ), lambda l, i: (l, i, 0)),
        compiler_params=_params("parallel", "parallel"),
        name="w_in_layout",
    )(jnp.swapaxes(w, 1, 2))


def _permute_w_qb(w):
    w = w.reshape(w.shape[0], MLA_HEADS, MLA_NOPE + MLA_ROPE)
    w = jnp.pad(w, ((0, 0), (0, 0), (0, MLA_HEAD_PAD - MLA_NOPE - MLA_ROPE)))
    return w.reshape(w.shape[0], MLA_HEADS * MLA_HEAD_PAD).astype(BF16)


def _permute_w_kvb(w):
    w = w.reshape(w.shape[0], MLA_HEADS, MLA_NOPE + MLA_V)
    return jnp.concatenate([w[:, :, :MLA_NOPE].reshape(w.shape[0], -1),
                            w[:, :, MLA_NOPE:].reshape(w.shape[0], -1)], axis=1).astype(BF16)


def _rope_tables(pos_row, pos_col, half):
    inv = ROPE_BASE ** (-jnp.arange(half, dtype=F32) / half)
    zeros = jnp.zeros((pos_row.shape[0], half), F32)
    cos, sin_a, sin_b = [], [], []
    for pos in (pos_row, pos_col):
        ang = pos.astype(F32)[:, None] * inv[None, :]
        c, s = jnp.cos(ang), jnp.sin(ang)
        cos += [c, c]
        sin_a += [-s, zeros]
        sin_b += [zeros, s]
    pad = jnp.zeros((pos_row.shape[0], 128 - 4 * half), F32)
    return tuple(jnp.concatenate(t + [pad], axis=1) for t in (cos, sin_a, sin_b))


def _block_diag_ones():
    i = np.arange(128)
    return jnp.asarray((i[:, None] // RWKV_HEAD) == (i[None, :] // RWKV_HEAD), BF16)


def _split_w(w):
    hi = w.astype(BF16)
    return hi, (w - hi.astype(F32)).astype(BF16)


def _lru_gate_weights(wa, ba, wi, bi):
    w = jnp.concatenate([wa[0], wi[0], wa[1], wi[1]], axis=-1)
    b = jnp.concatenate([t.reshape(LRU_BLOCKS, 1, LRU_BLOCK_W) for t in (ba[0], bi[0], ba[1], bi[1])], axis=-1)
    return _split_w(w) + (b,)


def _rwkv_weights(mu_prev, mu_next, w0, w2, a0, a2, g2, k_k, k_a, r_k, bd):
    def pad_mu(mu):
        return jnp.pad(mu, (0, 3584 - mu.shape[0])).reshape(1, 3584)

    def two_dir(w):
        z = jnp.zeros_like(w[0])
        return jnp.concatenate([jnp.concatenate([w[0], z], axis=1), jnp.concatenate([z, w[1]], axis=1)], axis=0)

    g2p = jnp.pad(g2, ((0, 256 - RWKV_GATE_LORA), (0, 0)))
    return (pad_mu(mu_prev), pad_mu(mu_next), w0.reshape(1, 2048), a0.reshape(1, 2048),
            *_split_w(two_dir(w2)), *_split_w(two_dir(a2)), *_split_w(g2p),
            k_k.reshape(1, 1024), k_a.reshape(1, 1024), r_k.reshape(1, 1024), bd)


def kernel(x, c, ctx, c_ctx, w_mod, b_mod, w_in, w_out, ln1_g, ln1_b, w_ff1, w_ff2, ln2_g, ln2_b, swa_sink, lru_conv_w, lru_conv_b, lru_wa, lru_ba, lru_wi, lru_bi, lru_lam, mla_q_norm, mla_kv_norm, mla_w_qb, mla_w_kvb, rwkv_mu_prev, rwkv_mu_next, rwkv_w0, rwkv_w2, rwkv_a0, rwkv_a2, rwkv_g2, rwkv_k_k, rwkv_k_a, rwkv_r_k, rwkv_ln_g, rwkv_ln_b):
    bsz, t_len, d = x.shape
    n_ctx = ctx.shape[1]
    depth = w_mod.shape[0]

    cc = jnp.zeros((8, d), F32).at[:bsz].set(c).at[bsz].set(c_ctx)
    mod = _modulation(cc, w_mod, b_mod)

    pos = jnp.arange(t_len, dtype=jnp.int32)
    row, col = pos // GRID_W, pos % GRID_W
    swa_tables = _rope_tables(row, col, SWA_HEAD_DIM // 4)
    mla_tables = _rope_tables(row, col, MLA_ROPE // 4)
    bd = _block_diag_ones()
    zeros_h = jnp.zeros((bsz, 1, GROUP_W), F32)
    zeros_s = jnp.zeros((bsz, RWKV_HEADS * RWKV_HEAD, RWKV_HEAD), F32)

    xl = x.reshape(bsz * t_len, d)
    xc = ctx.reshape(bsz * n_ctx, d)

    w_in_p = _permute_w_in(w_in)
    w_out_b = w_out.astype(BF16)
    w_ff1_b = w_ff1.astype(BF16)
    w_ff2_b = w_ff2.astype(BF16)

    for l in range(depth):
        with_ctx = l < depth - 1
        chunks = [mod[l, :, k * d:(k + 1) * d] for k in range(6)]
        lat = [m[:bsz][:, None, :] for m in chunks]
        cxm = [m[bsz:bsz + 1][:, None, :] for m in chunks]

        p = _ln_mod_matmul(xl, lat[0], lat[1], w_in_p, l, act=None, out_dtype=F32, tm=1024, tn=512)
        pc = _ln_mod_matmul(xc, cxm[0], cxm[1], w_in_p, l, act=None, out_dtype=F32)

        q_a, k_a, v_a = _swa_prep(p, t_len, swa_tables)
        qc_a, kc_a, vc_a = _swa_prep(pc, n_ctx, None)
        mix = _swa_attn(q_a, k_a, v_a, kc_a, vc_a, swa_sink[l], bsz, t_len, n_ctx, out_width=4 * GROUP_W)

        gate_w = _lru_gate_weights(lru_wa[l], lru_ba[l], lru_wi[l], lru_bi[l])
        lru_args = (lru_conv_w[l], lru_conv_b[l].reshape(1, -1), *gate_w, lru_lam[l])
        caf, cbf, cab, cbb = _lru_coef(pc, n_ctx, *lru_args)
        hc_f, hlast_f = _lru_scan(caf, cbf, zeros_h, bsz, n_ctx, reverse=False)
        oc_lru, hlast_b = _lru_scan(cab, cbb, zeros_h, bsz, n_ctx, reverse=True, h_other=hc_f, p=pc)
        laf, lbf, lab, lbb = _lru_coef(p, t_len, *lru_args)
        h_f, _ = _lru_scan(laf, lbf, hlast_f, bsz, t_len, reverse=False)
        mix, _ = _lru_scan(lab, lbb, hlast_b, bsz, t_len, reverse=True, h_other=h_f, p=p, mix=mix)

        wq = _permute_w_qb(mla_w_qb[l])
        wkv = _permute_w_kvb(mla_w_kvb[l])
        q_c, k_c, v_c = _mla_proj(p, t_len, mla_q_norm[l], mla_kv_norm[l], wq, wkv, mla_tables)
        qc_c, kc_c, vc_c = _mla_proj(pc, n_ctx, mla_q_norm[l], mla_kv_norm[l], wq, wkv, None)
        mix = _mla_flash(q_c, kc_c, vc_c, bsz, t_len, n_ctx, k=k_c, v=v_c, mix=mix)

        rw = _rwkv_weights(rwkv_mu_prev[l], rwkv_mu_next[l], rwkv_w0[l], rwkv_w2[l], rwkv_a0[l], rwkv_a2[l],
                           rwkv_g2[l], rwkv_k_k[l], rwkv_k_a[l], rwkv_r_k[l], bd)
        fc = _rwkv_feat(pc, n_ctx, rw)
        yc_f, yc_b, s_f, s_b = _wkv_scan(fc, zeros_s, zeros_s, bsz, n_ctx)
        fl = _rwkv_feat(p, t_len, rw)
        y_f, y_b, _, _ = _wkv_scan(fl, s_f, s_b, bsz, t_len)
        mix = _rwkv_out(y_f, y_b, fl[4], fl[3], rwkv_ln_g[l], rwkv_ln_b[l], bd, t_len, mix=mix)

        def tail(xin, mix, m):
            x1 = _matmul_res_ln(mix, w_out_b, l, xin, m[2], ln1_g[l], ln1_b[l], tm=512)
            hid = _ln_mod_matmul(x1, m[3], m[4], w_ff1_b, l, act="relu2", out_dtype=BF16, tm=1024)
            return _matmul_res_ln(hid, w_ff2_b, l, x1, m[5], ln2_g[l], ln2_b[l])

        if with_ctx:
            oc_swa = _swa_ctx_attn(qc_a, kc_a, vc_a, swa_sink[l], bsz, n_ctx)
            oc_mla = _mla_flash(qc_c, kc_c, vc_c, bsz, n_ctx, n_ctx)
            oc_rwkv = _rwkv_out(yc_f, yc_b, fc[4], fc[3], rwkv_ln_g[l], rwkv_ln_b[l], bd, n_ctx)
            xc = tail(xc, jnp.concatenate([oc_swa, oc_lru, oc_mla, oc_rwkv], axis=-1), cxm)

        xl = tail(xl, mix, lat)

    return xl.reshape(bsz, t_len, d)
```

```python
import functools

import numpy as np
import jax
import jax.numpy as jnp
from jax import lax
from jax.experimental import pallas as pl
from jax.experimental.pallas import tpu as pltpu

F32 = jnp.float32
BF16 = jnp.bfloat16

D_MODEL = 4096
GRID_W = 64
GROUP_W = 1024
D_FF = 4 * D_MODEL
BLOCK = 128

SWA_HEAD_DIM = 128
SWA_HEADS = 8
SWA_KV_HEADS = 2
SWA_GROUP = 4

LRU_BLOCKS = 8
LRU_BLOCK_W = 128
LRU_C = 8.0

MLA_HEADS = 8
MLA_NOPE = 128
MLA_ROPE = 64
MLA_V = 128
MLA_HEAD_PAD = 256

RWKV_HEAD = 64
RWKV_HEADS = 16
RWKV_GATE_LORA = 160
RWKV_GN_EPS = 64e-5
WKV_CHUNK = 64
WKV_INV_BLOCK = 16

ROPE_BASE = 10000.0
LN_EPS = 1e-5
RMS_EPS = 1e-6
DEPTH = 2
ALPHA = (2 * DEPTH) ** 0.25

OFF_AQ, OFF_BX, OFF_BG, OFF_CQ, OFF_DR, OFF_DK, OFF_DV = 0, 1024, 2048, 3072, 4096, 5120, 6144
OFF_CKV, OFF_AK, OFF_AV, OFF_DL, OFF_CR = 7168, 7680, 7936, 8192, 8704
N_PROJ = 9216
_PROJ_LAYOUT = ((0, 1024, 1024), (1536, 1024, 1024), (2560, 1024, 1024), (3584, 1024, 1024), (5184, 1024, 1024),
                (6208, 1024, 1024), (7232, 1024, 1024), (4608, 512, 512), (1024, 256, 256), (1280, 256, 256),
                (8256, 416, 512), (5120, 64, 128), (None, 0, 384))

VMEM_LIMIT_V7X = 56 * 1024 * 1024


def _params(*sem):
    return pltpu.CompilerParams(dimension_semantics=sem, vmem_limit_bytes=VMEM_LIMIT_V7X)


def _split2(a):
    hi = a.astype(BF16)
    lo = (a - hi.astype(F32)).astype(BF16)
    return hi, lo


def _mm(a, b):
    return jnp.dot(a, b, preferred_element_type=F32)


def _mm_nt(a, b):
    return lax.dot_general(a, b, (((1,), (1,)), ((), ())), preferred_element_type=F32)


def _mm_tn(a, b):
    return lax.dot_general(a, b, (((0,), (0,)), ((), ())), preferred_element_type=F32)


def _mm3(a, b, mm=_mm):
    ah, al = _split2(a)
    bh, bl = _split2(b)
    return mm(ah, bh) + (mm(al, bh) + mm(ah, bl))


def _mm3w(a, bh, bl):
    ah, al = _split2(a)
    return _mm(ah, bh) + (_mm(al, bh) + _mm(ah, bl))


def _mm1(a, b, mm=_mm):
    return mm(a.astype(BF16), b.astype(BF16))


def _seg_sum(x, bd):
    parts = []
    for j in range(x.shape[1] // 128):
        hi, lo = _split2(x[:, 128 * j:128 * (j + 1)])
        parts.append(_mm(hi, bd) + _mm(lo, bd))
    return jnp.concatenate(parts, axis=1)


def _layer_norm_rows(x):
    mu = jnp.mean(x, -1, keepdims=True)
    xc = x - mu
    var = jnp.mean(xc * xc, -1, keepdims=True)
    return xc * lax.rsqrt(var + LN_EPS)


def _softplus(z):
    return jnp.maximum(z, 0.0) + jnp.log1p(jnp.exp(-jnp.abs(z)))


def _gelu_tanh(x):
    return 0.5 * x * (1.0 + jnp.tanh(np.sqrt(2.0 / np.pi).astype(np.float32) * (x + 0.044715 * (x * x * x))))


def _rope(x, cos, sin_a, sin_b, half):
    return x * cos + pltpu.roll(x, 128 - half, 1) * sin_a + pltpu.roll(x, half, 1) * sin_b


def _shift_rows(x, row, tm, prev_rows, next_row):
    p6, p7 = prev_rows
    xm1 = jnp.where(row == 0, p7, pltpu.roll(x, 1, 0))
    xp1 = jnp.where(row == tm - 1, next_row, pltpu.roll(x, tm - 1, 0))
    xm2 = None
    if p6 is not None:
        xm2 = jnp.where(row == 0, p6, jnp.where(row == 1, p7, pltpu.roll(x, 2, 0)))
    return xm2, xm1, xp1


def _halo_specs(col, width, tm, m_rows):
    r8 = tm // 8
    last8 = m_rows // 8 - 1
    return [pl.BlockSpec((tm, width), lambda i: (i, col)),
            pl.BlockSpec((8, width), lambda i: (jnp.maximum(i * r8 - 1, 0), col)),
            pl.BlockSpec((8, width), lambda i: (jnp.minimum((i + 1) * r8, last8), col))]


def _full(shape):
    nd = len(shape)
    return pl.BlockSpec(shape, lambda *_: (0,) * nd)


def _mod_kernel(c_ref, w_ref, b_ref, o_ref):
    cc = c_ref[...]
    a = cc * jax.nn.sigmoid(cc)
    o_ref[...] = _mm3(a, w_ref[...]) + b_ref[...]


def _modulation(cc, w_mod, b_mod, tn=512):
    depth, d, n = w_mod.shape
    return pl.pallas_call(
        _mod_kernel,
        out_shape=jax.ShapeDtypeStruct((depth, 8, n), F32),
        grid=(depth, n // tn),
        in_specs=[pl.BlockSpec((8, d), lambda l, j: (0, 0)),
                  pl.BlockSpec((None, d, tn), lambda l, j: (l, 0, j)),
                  pl.BlockSpec((None, 1, tn), lambda l, j: (l, 0, j))],
        out_specs=pl.BlockSpec((None, 8, tn), lambda l, j: (l, 0, j)),
        compiler_params=_params("parallel", "parallel"),
        name="adaln_modulation",
    )(cc, w_mod, b_mod.reshape(depth, 1, n))


def _lnmm_kernel(x_ref, sh_ref, sc_ref, w_ref, o_ref, xn_ref, *, act, tm):
    @pl.when(pl.program_id(1) == 0)
    def _():
        scale = 1.0 + sc_ref[...]
        shift = sh_ref[...]
        rows = min(tm, 128)

        def body(r, carry):
            sl = pl.ds(pl.multiple_of(r * rows, rows), rows)
            xn_ref[sl, :] = (_layer_norm_rows(x_ref[sl, :]) * scale + shift).astype(BF16)
            return carry

        lax.fori_loop(0, tm // rows, body, 0)

    acc = _mm(xn_ref[...], w_ref[...])
    if act == "relu2":
        acc = jnp.maximum(acc, 0.0)
        acc = acc * acc
    o_ref[...] = acc.astype(o_ref.dtype)


def _ln_mod_matmul(x, shift, scale, w, layer, *, act, out_dtype, tm=512, tn=1024):
    m, k = x.shape
    n = w.shape[2]
    nbm = shift.shape[0]
    tm = min(tm, m // nbm)
    seq_tiles = m // nbm // tm
    mod_spec = pl.BlockSpec((None, 1, k), lambda i, j: (i // seq_tiles, 0, 0))
    x_spec = pl.BlockSpec((tm, k), lambda i, j: (i, 0), pipeline_mode=pl.Buffered(1))
    return pl.pallas_call(
        functools.partial(_lnmm_kernel, act=act, tm=tm),
        out_shape=jax.ShapeDtypeStruct((m, n), out_dtype),
        grid=(m // tm, n // tn),
        in_specs=[x_spec, mod_spec, mod_spec,
                  pl.BlockSpec((None, k, tn), lambda i, j: (layer, 0, j))],
        out_specs=pl.BlockSpec((tm, tn), lambda i, j: (i, j)),
        scratch_shapes=[pltpu.VMEM((tm, k), BF16)],
        compiler_params=_params("parallel", "arbitrary"),
        name="ln_mod_matmul_" + (act or "id"),
    )(x, shift, scale, w)


def _mmln_kernel(a_ref, w_ref, x_ref, ga_ref, g_ref, b_ref, o_ref, *, nk, tm):
    k = pl.program_id(1)

    @pl.when(k == 0)
    def _():
        o_ref[...] = _mm(a_ref[...], w_ref[...])

    @pl.when(k > 0)
    def _():
        o_ref[...] += _mm(a_ref[...], w_ref[...])

    @pl.when(k == nk - 1)
    def _():
        gate = ga_ref[...]
        g = g_ref[...]
        b = b_ref[...]
        rows = min(tm, 128)

        def body(r, carry):
            sl = pl.ds(pl.multiple_of(r * rows, rows), rows)
            z = ALPHA * x_ref[sl, :] + gate * o_ref[sl, :]
            o_ref[sl, :] = _layer_norm_rows(z) * g + b
            return carry

        lax.fori_loop(0, tm // rows, body, 0)


def _matmul_res_ln(a, w, layer, xres, gate, g, b, *, tm=1024, tk=512):
    m, kdim = a.shape
    n = w.shape[2]
    nbm = gate.shape[0]
    tm = min(tm, m // nbm)
    seq_tiles = m // nbm // tm
    nk = kdim // tk
    mode = dict(pipeline_mode=pl.Buffered(1)) if nk >= 16 else {}
    return pl.pallas_call(
        functools.partial(_mmln_kernel, nk=nk, tm=tm),
        out_shape=jax.ShapeDtypeStruct((m, n), F32),
        grid=(m // tm, nk),
        in_specs=[pl.BlockSpec((tm, tk), lambda i, k: (i, k)),
                  pl.BlockSpec((None, tk, n), lambda i, k: (layer, k, 0)),
                  pl.BlockSpec((tm, n), lambda i, k: (i, 0), **mode),
                  pl.BlockSpec((None, 1, n), lambda i, k: (i // seq_tiles, 0, 0)),
                  _full((1, n)), _full((1, n))],
        out_specs=pl.BlockSpec((tm, n), lambda i, k: (i, 0), **mode),
        compiler_params=_params("parallel", "arbitrary"),
        name="matmul_res_ln",
    )(a, w, xres, gate, g.reshape(1, n), b.reshape(1, n))


def _swa_prep_kernel(*refs, rotate):
    if rotate:
        q_ref, kv_ref, cos_ref, sa_ref, sb_ref, qo_ref, ko_ref, vo_ref = refs
        cos, sa, sb = cos_ref[...], sa_ref[...], sb_ref[...]
    else:
        q_ref, kv_ref, qo_ref, ko_ref, vo_ref = refs
    scale = np.float32(SWA_HEAD_DIM ** -0.5)
    for h in range(SWA_HEADS):
        sl = slice(128 * h, 128 * (h + 1))
        q = q_ref[:, sl]
        if rotate:
            q = _rope(q, cos, sa, sb, 32)
        qo_ref[:, sl] = (q * scale).astype(BF16)
    for h in range(SWA_KV_HEADS):
        sl = slice(128 * h, 128 * (h + 1))
        k = kv_ref[:, sl]
        if rotate:
            k = _rope(k, cos, sa, sb, 32)
        ko_ref[:, sl] = k.astype(BF16)
    vo_ref[...] = kv_ref[:, 256:512].astype(BF16)


def _swa_prep(p, t_len, tables, tm=256):
    m = p.shape[0]
    tm = min(tm, t_len)
    seq_tiles = t_len // tm
    rotate = tables is not None
    in_specs = [pl.BlockSpec((tm, 1024), lambda i: (i, OFF_AQ // 1024)),
                pl.BlockSpec((tm, 512), lambda i: (i, OFF_AK // 512))]
    args = [p, p]
    if rotate:
        in_specs += [pl.BlockSpec((tm, 128), lambda i: (i % seq_tiles, 0))] * 3
        args += list(tables)
    return pl.pallas_call(
        functools.partial(_swa_prep_kernel, rotate=rotate),
        out_shape=(jax.ShapeDtypeStruct((m, 1024), BF16), jax.ShapeDtypeStruct((m, 256), BF16),
                   jax.ShapeDtypeStruct((m, 256), BF16)),
        grid=(m // tm,),
        in_specs=in_specs,
        out_specs=(pl.BlockSpec((tm, 1024), lambda i: (i, 0)), pl.BlockSpec((tm, 256), lambda i: (i, 0)),
                   pl.BlockSpec((tm, 256), lambda i: (i, 0))),
        compiler_params=_params("parallel"),
        name="swa_prep_rope" if rotate else "swa_prep",
    )(*args)


def _sink_softmax_pv(s, sink, v):
    m = jnp.maximum(jnp.max(s, -1, keepdims=True), sink)
    p = jnp.exp(s - m)
    den = jnp.sum(p, -1, keepdims=True) + jnp.exp(sink - m)
    return _mm(p.astype(BF16), v) / den


def _swa_kernel(q_ref, kp_ref, ko_ref, kn_ref, vp_ref, vo_ref, vn_ref, kc_ref, vc_ref, sink_ref, o_ref, *, nb, n_ctx):
    n = pl.program_id(1)
    qi = lax.broadcasted_iota(jnp.int32, (BLOCK, BLOCK), 0)
    kj = lax.broadcasted_iota(jnp.int32, (BLOCK, BLOCK), 1)
    neg = np.float32(-np.inf)
    bias_prev = jnp.where(jnp.logical_and(kj >= qi, n > 0), 0.0, neg)
    bias_next = jnp.where(jnp.logical_and(kj <= qi, n < nb - 1), 0.0, neg)
    bias = jnp.concatenate([bias_prev, jnp.zeros((BLOCK, BLOCK), F32), bias_next,
                            jnp.zeros((BLOCK, n_ctx), F32)], axis=1)
    kv_sl = [slice(128 * hh, 128 * (hh + 1)) for hh in range(SWA_KV_HEADS)]
    k_all = [jnp.concatenate([kp_ref[:, sl], ko_ref[:, sl], kn_ref[:, sl], kc_ref[:, sl]], axis=0) for sl in kv_sl]
    v_all = [jnp.concatenate([vp_ref[:, sl], vo_ref[:, sl], vn_ref[:, sl], vc_ref[:, sl]], axis=0) for sl in kv_sl]
    heads = range(SWA_HEADS)
    s = [_mm_nt(q_ref[:, 128 * h:128 * (h + 1)], k_all[h // SWA_GROUP]) + bias for h in heads]
    sink = [sink_ref[0:1, h:h + 1] for h in heads]
    m = [jnp.maximum(jnp.max(s[h], -1, keepdims=True), sink[h]) for h in heads]
    p = [jnp.exp(s[h] - m[h]) for h in heads]
    den = [jnp.sum(p[h], -1, keepdims=True) + jnp.exp(sink[h] - m[h]) for h in heads]
    o = [_mm(p[h].astype(BF16), v_all[h // SWA_GROUP]) for h in heads]
    for h in heads:
        o_ref[:, 128 * h:128 * (h + 1)] = (o[h] / den[h]).astype(BF16)


def _swa_attn(q, k, v, kc, vc, sink, bsz, t_len, n_ctx, mix=None):
    nb = t_len // BLOCK

    def blk(width, off):
        def idx(b, n):
            return (b * nb + jnp.clip(n + off, 0, nb - 1), 0)
        return pl.BlockSpec((BLOCK, width), idx)

    ctx_spec = pl.BlockSpec((n_ctx, 256), lambda b, n: (b, 0))
    in_specs = [blk(1024, 0), blk(256, -1), blk(256, 0), blk(256, 1), blk(256, -1), blk(256, 0), blk(256, 1),
                ctx_spec, ctx_spec, _full((1, SWA_HEADS))]
    args = [q, k, k, k, v, v, v, kc, vc, sink.reshape(1, SWA_HEADS)]
    out_width, aliases, body = GROUP_W, {}, functools.partial(_swa_kernel, nb=nb, n_ctx=n_ctx)
    if mix is not None:
        in_specs.append(pl.BlockSpec(memory_space=pl.ANY))
        args.append(mix)
        out_width, aliases = mix.shape[-1], {len(args) - 1: 0}

        def body(*refs):
            _swa_kernel(*refs[:-2], refs[-1], nb=nb, n_ctx=n_ctx)

    return pl.pallas_call(
        body,
        out_shape=jax.ShapeDtypeStruct((bsz * t_len, out_width), BF16),
        grid=(bsz, nb),
        in_specs=in_specs,
        out_specs=blk(1024, 0),
        input_output_aliases=aliases,
        compiler_params=_params("parallel", "parallel"),
        name="swa_attention",
    )(*args)


def _swa_ctx_kernel(q_ref, kc_ref, vc_ref, sink_ref, o_ref):
    for hh in range(SWA_KV_HEADS):
        sl = slice(128 * hh, 128 * (hh + 1))
        for g in range(SWA_GROUP):
            h = hh * SWA_GROUP + g
            hs = slice(128 * h, 128 * (h + 1))
            s = _mm_nt(q_ref[:, hs], kc_ref[:, sl])
            o_ref[:, hs] = _sink_softmax_pv(s, sink_ref[0:1, h:h + 1], vc_ref[:, sl]).astype(BF16)


def _swa_ctx_attn(qc, kc, vc, sink, bsz, n_ctx):
    return pl.pallas_call(
        _swa_ctx_kernel,
        out_shape=jax.ShapeDtypeStruct((bsz * n_ctx, 1024), BF16),
        grid=(bsz,),
        in_specs=[pl.BlockSpec((n_ctx, 1024), lambda b: (b, 0)), pl.BlockSpec((n_ctx, 256), lambda b: (b, 0)),
                  pl.BlockSpec((n_ctx, 256), lambda b: (b, 0)), _full((1, SWA_HEADS))],
        out_specs=pl.BlockSpec((n_ctx, 1024), lambda b: (b, 0)),
        compiler_params=_params("parallel"),
        name="swa_ctx_attention",
    )(qc, kc, vc, sink.reshape(1, SWA_HEADS))


def _lru_coef_kernel(x_ref, xp_ref, xn_ref, cw_ref, cb_ref, wh_ref, wl_ref, bg_ref, lam_ref,
                     af_ref, bf_ref, ab_ref, bb_ref, *, tm, seq_tiles):
    i = pl.program_id(0) % seq_tiles
    first = i == 0
    last = i == seq_tiles - 1
    x = x_ref[...]
    row = lax.broadcasted_iota(jnp.int32, x.shape, 0)
    p6 = jnp.where(first, 0.0, xp_ref[6:7, :])
    p7 = jnp.where(first, 0.0, xp_ref[7:8, :])
    n0 = jnp.where(last, 0.0, xn_ref[0:1, :])
    xm2, xm1, xp1 = _shift_rows(x, row, tm, (p6, p7), n0)
    u = (cw_ref[0:1, :] * xm2 + cw_ref[1:2, :] * xm1 + cw_ref[2:3, :] * x + cw_ref[3:4, :] * xp1) + cb_ref[...]
    sp = _softplus(-lam_ref[...])
    outs = ((af_ref, bf_ref), (ab_ref, bb_ref))
    for n in range(LRU_BLOCKS):
        sl = slice(128 * n, 128 * (n + 1))
        un = u[:, sl]
        z = _mm3w(un, wh_ref[n], wl_ref[n]) + bg_ref[n]
        for d in range(2):
            r = jax.nn.sigmoid(z[:, 256 * d:256 * d + 128])
            gi = jax.nn.sigmoid(z[:, 256 * d + 128:256 * d + 256])
            log_a = (-LRU_C) * r * sp[d:d + 1, sl]
            a_ref, b_ref = outs[d]
            a_ref[:, sl] = jnp.exp(log_a)
            b_ref[:, sl] = jnp.sqrt(1.0 - jnp.exp(2.0 * log_a)) * (gi * un)


def _lru_coef(p, t_len, conv_w, conv_b, wg_hi, wg_lo, bg, lam, tm=256):
    m = p.shape[0]
    tm = min(tm, t_len)
    seq_tiles = t_len // tm
    out = jax.ShapeDtypeStruct((m, 1024), F32)
    ospec = pl.BlockSpec((tm, 1024), lambda i: (i, 0))
    return pl.pallas_call(
        functools.partial(_lru_coef_kernel, tm=tm, seq_tiles=seq_tiles),
        out_shape=(out, out, out, out),
        grid=(m // tm,),
        in_specs=_halo_specs(OFF_BX // 1024, 1024, tm, m) + [
            _full((4, 1024)), _full((1, 1024)), _full((8, 128, 512)), _full((8, 128, 512)),
            _full((8, 1, 512)), _full((2, 1024))],
        out_specs=(ospec, ospec, ospec, ospec),
        compiler_params=_params("parallel"),
        name="lru_coefficients",
    )(p, p, p, conv_w, conv_b, wg_hi, wg_lo, bg, lam)


def _lru_scan_kernel(*refs, reverse, final, nc, tc, aliased):
    if aliased:
        refs = refs[:5] + refs[6:]
    if final:
        a_ref, b_ref, h0_ref, hf_ref, gate_ref, o_ref, hl_ref, hs_ref, hrows_ref = refs
    else:
        a_ref, b_ref, h0_ref, o_ref, hl_ref, hs_ref = refs
        hrows_ref = o_ref
    c = pl.program_id(0)

    @pl.when(c == 0)
    def _():
        hs_ref[...] = h0_ref[...]

    def body(t, h):
        tt = (tc - 1 - t) if reverse else t
        h = a_ref[:, pl.ds(tt, 1), :] * h + b_ref[:, pl.ds(tt, 1), :]
        hrows_ref[:, pl.ds(tt, 1), :] = h
        return h

    h = lax.fori_loop(0, tc, body, hs_ref[...], unroll=8)
    hs_ref[...] = h

    @pl.when(c == (0 if reverse else nc - 1))
    def _():
        hl_ref[...] = hrows_ref[:, tc - 1:tc, :]

    if final:
        o_ref[...] = ((hf_ref[...] + hrows_ref[...]) * _gelu_tanh(gate_ref[...])).astype(BF16)


def _lru_scan(a, b, h0, bsz, t_len, *, reverse, h_other=None, p=None, mix=None, tc=256):
    tc = min(tc, t_len)
    nc = t_len // tc
    final = h_other is not None
    aliased = mix is not None

    def chunk(c):
        return (nc - 1 - c) if reverse else c

    def seq(t):
        return t.reshape(bsz, t_len, t.shape[-1])

    rows = pl.BlockSpec((bsz, tc, 1024), lambda c: (0, chunk(c), 0))
    state = _full((bsz, 1, 1024))
    in_specs = [rows, rows, state]
    args = [seq(a), seq(b), h0]
    scratch = [pltpu.VMEM((bsz, 1, 1024), F32)]
    if final:
        in_specs += [rows, pl.BlockSpec((bsz, tc, 1024), lambda c: (0, chunk(c), OFF_BG // 1024))]
        args += [seq(h_other), seq(p)]
        scratch.append(pltpu.VMEM((bsz, tc, 1024), F32))
    out_rows, out_width, aliases = rows, 1024, {}
    if aliased:
        in_specs.append(pl.BlockSpec(memory_space=pl.ANY))
        args.append(seq(mix))
        out_rows = pl.BlockSpec((bsz, tc, 1024), lambda c: (0, chunk(c), 1))
        out_width, aliases = mix.shape[-1], {len(args) - 1: 0}
    out, h_last = pl.pallas_call(
        functools.partial(_lru_scan_kernel, reverse=reverse, final=final, nc=nc, tc=tc, aliased=aliased),
        out_shape=(jax.ShapeDtypeStruct((bsz, t_len, out_width), BF16 if final else F32),
                   jax.ShapeDtypeStruct((bsz, 1, 1024), F32)),
        grid=(nc,),
        in_specs=in_specs,
        out_specs=(out_rows, state),
        scratch_shapes=scratch,
        input_output_aliases=aliases,
        compiler_params=_params("arbitrary"),
        name="lru_scan_" + ("bwd_out" if final else ("bwd" if reverse else "fwd")),
    )(*args)
    return out.reshape(bsz * t_len, out_width), h_last


def _mla_proj_kernel(*refs, rotate):
    if rotate:
        (qa_ref, kva_ref, kr_ref, qn_ref, kvn_ref, wq_ref, wkv_ref, cos_ref, sa_ref, sb_ref,
         q_ref, k_ref, v_ref) = refs
        cos, sa, sb = cos_ref[...], sa_ref[...], sb_ref[...]
    else:
        qa_ref, kva_ref, kr_ref, qn_ref, kvn_ref, wq_ref, wkv_ref, q_ref, k_ref, v_ref = refs

    def rms(x, g):
        return (x * lax.rsqrt(jnp.mean(x * x, -1, keepdims=True) + RMS_EPS) * g).astype(BF16)

    scale = np.float32((MLA_NOPE + MLA_ROPE) ** -0.5 * np.log2(np.e))
    q = _mm(rms(qa_ref[...], qn_ref[...]), wq_ref[...]) * scale
    kv = _mm(rms(kva_ref[...], kvn_ref[...]), wkv_ref[...])
    kr = kr_ref[...]
    if rotate:
        kr = _rope(kr, cos, sa, sb, 16)
    kr = kr.astype(BF16)
    lane = lax.broadcasted_iota(jnp.int32, kr.shape, 1)
    ones_col = jnp.where(lane == 0, 1.0, 0.0).astype(BF16)
    for h in range(MLA_HEADS):
        lo = MLA_HEAD_PAD * h
        q_ref[:, lo:lo + 128] = q[:, lo:lo + 128].astype(BF16)
        qr = q[:, lo + 128:lo + 256]
        if rotate:
            qr = _rope(qr, cos, sa, sb, 16)
        q_ref[:, lo + 128:lo + 256] = qr.astype(BF16)
        k_ref[:, lo:lo + 128] = kv[:, 128 * h:128 * (h + 1)].astype(BF16)
        k_ref[:, lo + 128:lo + 256] = kr
        v_ref[:, lo:lo + 128] = kv[:, 1024 + 128 * h:1024 + 128 * (h + 1)].astype(BF16)
        v_ref[:, lo + 128:lo + 256] = ones_col


def _mla_proj(p, t_len, q_norm, kv_norm, wq, wkv, tables, tm=256):
    m = p.shape[0]
    tm = min(tm, t_len)
    seq_tiles = t_len // tm
    rotate = tables is not None
    in_specs = [pl.BlockSpec((tm, 1024), lambda i: (i, OFF_CQ // 1024)),
                pl.BlockSpec((tm, 512), lambda i: (i, OFF_CKV // 512)),
                pl.BlockSpec((tm, 128), lambda i: (i, OFF_CR // 128)),
                _full((1, 1024)), _full((1, 512)), _full((1024, 2048)), _full((512, 2048))]
    args = [p, p, p, q_norm.reshape(1, -1), kv_norm.reshape(1, -1), wq, wkv]
    if rotate:
        in_specs += [pl.BlockSpec((tm, 128), lambda i: (i % seq_tiles, 0))] * 3
        args += list(tables)
    wide = pl.BlockSpec((tm, 2048), lambda i: (i, 0))
    return pl.pallas_call(
        functools.partial(_mla_proj_kernel, rotate=rotate),
        out_shape=(jax.ShapeDtypeStruct((m, 2048), BF16),) * 3,
        grid=(m // tm,),
        in_specs=in_specs,
        out_specs=(wide, wide, wide),
        compiler_params=_params("parallel"),
        name="mla_project_rope" if rotate else "mla_project",
    )(*args)


def _mla_flash_kernel(*refs, with_lat, n_chunks, ck, aliased):
    if aliased:
        refs = refs[:-2] + refs[-1:]
    if with_lat:
        q_ref, kc_ref, vc_ref, k_ref, v_ref, o_ref = refs
    else:
        q_ref, kc_ref, vc_ref, o_ref = refs
    q = q_ref[...]
    chunks = [(kc_ref, vc_ref, slice(None))]
    if with_lat:
        chunks += [(k_ref, v_ref, slice(j * ck, (j + 1) * ck)) for j in range(n_chunks)]
    k0, _, sl0 = chunks[0]
    s_cur = _mm_nt(q, k0[sl0, :])
    m = acc = pending = s_next = None
    for j, (_, vj_ref, slj) in enumerate(chunks):
        if j + 1 < len(chunks):
            kn_ref, _, sln = chunks[j + 1]
            s_next = _mm_nt(q, kn_ref[sln, :])
        if pending is not None:
            p_prev, v_prev, alpha_prev = pending
            pv = _mm(p_prev, v_prev)
            acc = pv if acc is None else alpha_prev * acc + pv
        mx = jnp.max(s_cur, -1, keepdims=True)
        m_new = mx if m is None else jnp.maximum(m, mx)
        alpha = None if m is None else jnp.exp2(m - m_new)
        pending = (jnp.exp2((s_cur - m_new).astype(BF16)), vj_ref[slj, :], alpha)
        m = m_new
        s_cur = s_next
    p_prev, v_prev, alpha_prev = pending
    pv = _mm(p_prev, v_prev)
    acc = pv if acc is None else alpha_prev * acc + pv
    o_ref[...] = (acc[:, 0:MLA_V] / acc[:, MLA_V:MLA_V + 1]).astype(BF16)


def _mla_flash(q, kc, vc, bsz, tq_len, n_ctx, k=None, v=None, mix=None, tq=1024, ck=512):
    with_lat = k is not None
    aliased = mix is not None
    tq = min(tq, tq_len)
    nq = tq_len // tq
    in_specs = [pl.BlockSpec((tq, MLA_HEAD_PAD), lambda b, h, i: (b * nq + i, h)),
                pl.BlockSpec((n_ctx, MLA_HEAD_PAD), lambda b, h, i: (b, h)),
                pl.BlockSpec((n_ctx, MLA_HEAD_PAD), lambda b, h, i: (b, h))]
    args = [q, kc, vc]
    n_chunks = 0
    if with_lat:
        t_len = k.shape[0] // bsz
        ck = min(ck, t_len)
        n_chunks = t_len // ck
        in_specs += [pl.BlockSpec((t_len, MLA_HEAD_PAD), lambda b, h, i: (b, h)),
                     pl.BlockSpec((t_len, MLA_HEAD_PAD), lambda b, h, i: (b, h))]
        args += [k, v]
    out_width, col0, aliases = GROUP_W, 0, {}
    if aliased:
        in_specs.append(pl.BlockSpec(memory_space=pl.ANY))
        args.append(mix)
        out_width, col0, aliases = mix.shape[-1], 2 * GROUP_W // MLA_V, {len(args) - 1: 0}
    return pl.pallas_call(
        functools.partial(_mla_flash_kernel, with_lat=with_lat, n_chunks=n_chunks, ck=ck, aliased=aliased),
        out_shape=jax.ShapeDtypeStruct((bsz * tq_len, out_width), BF16),
        grid=(bsz, MLA_HEADS, nq),
        in_specs=in_specs,
        out_specs=pl.BlockSpec((tq, MLA_V), lambda b, h, i: (b * nq + i, col0 + h)),
        input_output_aliases=aliases,
        compiler_params=_params("parallel", "parallel", "parallel"),
        name="mla_flash" if with_lat else "mla_ctx_attention",
    )(*args)


def _rwkv_feat_kernel(xr_ref, xrp_ref, xrn_ref, xk_ref, xkp_ref, xkn_ref, xv_ref, xvp_ref, xvn_ref,
                      xl_ref, xlp_ref, xln_ref, mup_ref, mun_ref, w0_ref, a0_ref,
                      w2h_ref, w2l_ref, a2h_ref, a2l_ref, g2h_ref, g2l_ref, kk_w_ref, ka_ref, rk_ref, bd_ref,
                      r_o, v_o, kk_o, g_o, bonus_o, lwf_o, kf_o, bf_o, lwb_o, kb_o, bb_o, *, tm, seq_tiles):
    i = pl.program_id(0) % seq_tiles
    first = i == 0
    last = i == seq_tiles - 1

    def shifted(x_ref, xp_ref, xn_ref, lo, hi):
        x = x_ref[...]
        row = lax.broadcasted_iota(jnp.int32, x.shape, 0)
        p7 = jnp.where(first, 0.0, xp_ref[7:8, :])
        n0 = jnp.where(last, 0.0, xn_ref[0:1, :])
        _, xm1, xp1 = _shift_rows(x, row, tm, (None, p7), n0)
        return x + mup_ref[:, lo:hi] * (xm1 - x) + mun_ref[:, lo:hi] * (xp1 - x)

    r = shifted(xr_ref, xrp_ref, xrn_ref, 0, 1024)
    k = shifted(xk_ref, xkp_ref, xkn_ref, 1024, 2048)
    v = shifted(xv_ref, xvp_ref, xvn_ref, 2048, 3072)
    lo = shifted(xl_ref, xlp_ref, xln_ref, 3072, 3584)
    bd = bd_ref[...]

    kkr = k * kk_w_ref[...]
    kk = kkr / jnp.maximum(jnp.sqrt(_seg_sum(kkr * kkr, bd)), 1e-12)
    g = _mm3w(jax.nn.sigmoid(lo[:, 256:512]), g2h_ref[...], g2l_ref[...])
    wl = w0_ref[...] + _mm3w(jnp.tanh(lo[:, 0:128]), w2h_ref[...], w2l_ref[...])
    lw = np.float32(-np.exp(-0.5)) * jax.nn.sigmoid(wl)
    a = jax.nn.sigmoid(a0_ref[...] + _mm3w(lo[:, 128:256], a2h_ref[...], a2l_ref[...]))
    r_o[...] = r
    v_o[...] = v
    kk_o[...] = kk
    g_o[...] = g
    bonus = None
    for d, (lw_o, k_o, b_o) in enumerate(((lwf_o, kf_o, bf_o), (lwb_o, kb_o, bb_o))):
        a_d = a[:, 1024 * d:1024 * (d + 1)]
        k_d = k * (1.0 + (a_d - 1.0) * ka_ref[...])
        bo = _seg_sum(r * k_d * rk_ref[...], bd) * v
        bonus = bo if bonus is None else bonus + bo
        lw_o[...] = lw[:, 1024 * d:1024 * (d + 1)]
        k_o[...] = k_d
        b_o[...] = kk * a_d
    bonus_o[...] = bonus


def _rwkv_feat(p, t_len, wts, tm=256):
    m = p.shape[0]
    tm = min(tm, t_len)
    seq_tiles = t_len // tm
    in_specs = (_halo_specs(OFF_DR // 1024, 1024, tm, m) + _halo_specs(OFF_DK // 1024, 1024, tm, m)
                + _halo_specs(OFF_DV // 1024, 1024, tm, m) + _halo_specs(OFF_DL // 512, 512, tm, m)
                + [_full(w.shape) for w in wts])
    out = jax.ShapeDtypeStruct((m, 1024), F32)
    ospec = pl.BlockSpec((tm, 1024), lambda i: (i, 0))
    return pl.pallas_call(
        functools.partial(_rwkv_feat_kernel, tm=tm, seq_tiles=seq_tiles),
        out_shape=(out,) * 11,
        grid=(m // tm,),
        in_specs=in_specs,
        out_specs=(ospec,) * 11,
        compiler_params=_params("parallel"),
        name="rwkv_features",
    )(*([p] * 12), *wts)


def _wkv_operands(r_ref, v_ref, kk_ref, lw_ref, k_ref, b_ref, incl, reverse):
    cl = WKV_CHUNK
    tri = jnp.where(incl, 1.0, 0.0).astype(BF16)
    lw = lw_ref[...]
    l1 = lw.astype(BF16)
    rem = lw - l1.astype(F32)
    l2 = rem.astype(BF16)
    l3 = (rem - l2.astype(F32)).astype(BF16)
    cum = _mm(tri, l1) + (_mm(tri, l2) + _mm(tri, l3))
    total = cum[0:1, :] if reverse else cum[cl - 1:cl, :]
    e_k = jnp.exp(-cum)
    e_t = jnp.exp(total - cum)
    k_raw = k_ref[...]
    b_raw = b_ref[...]
    return dict(d_c=jnp.exp(total),
                r_t=(r_ref[...] * jnp.exp(cum)).astype(BF16),
                kk_t=(kk_ref[...] * jnp.exp(cum - lw)).astype(BF16),
                k_t=(k_raw * e_k).astype(BF16), b_t=(b_raw * e_k).astype(BF16),
                k_d=(k_raw * e_t).astype(BF16), b_d=(b_raw * e_t).astype(BF16),
                v=v_ref[...].astype(BF16))


def _wkv_kernel(rf_ref, vf_ref, kkf_ref, lwf_ref, kf_ref, bf_ref, rb_ref, vb_ref, kkb_ref, lwb_ref, kb_ref, bb_ref,
                s0f_ref, s0b_ref, yf_ref, yb_ref, soutf_ref, soutb_ref, sf_ref, sb_ref, *, nc):
    c = pl.program_id(1)
    cl = WKV_CHUNK

    @pl.when(c == 0)
    def _():
        sf_ref[...] = s0f_ref[...]
        sb_ref[...] = s0b_ref[...]

    ti = lax.broadcasted_iota(jnp.int32, (cl, cl), 0)
    si = lax.broadcasted_iota(jnp.int32, (cl, cl), 1)
    blk = (ti // WKV_INV_BLOCK) == (si // WKV_INV_BLOCK)
    masks = ((si <= ti, si < ti), (si >= ti, si > ti))
    ops = (_wkv_operands(rf_ref, vf_ref, kkf_ref, lwf_ref, kf_ref, bf_ref, masks[0][0], False),
           _wkv_operands(rb_ref, vb_ref, kkb_ref, lwb_ref, kb_ref, bb_ref, masks[1][0], True))
    s_all = (sf_ref[...], sb_ref[...])

    chains = [(d, slice(RWKV_HEAD * h, RWKV_HEAD * (h + 1))) for d in range(2) for h in range(RWKV_HEADS)]
    heads = range(len(chains))
    incl = [masks[d][0] for d, _ in chains]
    strict = [masks[d][1] for d, _ in chains]

    def lanes(name):
        return [ops[d][name][:, sl] for d, sl in chains]

    v_h, k_dh, b_dh, d_ch = lanes("v"), lanes("k_d"), lanes("b_d"), lanes("d_c")
    s0 = [s_all[d][sl, :] for d, sl in chains]
    left = [jnp.concatenate([kk, r], axis=0) for kk, r in zip(lanes("kk_t"), lanes("r_t"))]
    right = [jnp.concatenate([k, b], axis=0) for k, b in zip(lanes("k_t"), lanes("b_t"))]
    a = [_mm_nt(left[h], right[h]) for h in heads]
    ls = [_mm_nt(left[h], s0[h].astype(BF16)) for h in heads]
    a_l = [jnp.where(strict[h], a[h][0:cl, cl:2 * cl], 0.0) for h in heads]
    a_v = [jnp.concatenate([jnp.where(strict[h], a[h][0:cl, 0:cl], 0.0),
                            jnp.where(incl[h], a[h][cl:2 * cl, 0:cl], 0.0)], axis=0).astype(BF16) for h in heads]
    a_rb = [jnp.where(incl[h], a[h][cl:2 * cl, cl:2 * cl], 0.0).astype(BF16) for h in heads]
    av = [_mm(a_v[h], v_h[h]) for h in heads]
    dg = [jnp.where(blk, a_l[h], 0.0) for h in heads]
    off = [a_l[h] - dg[h] for h in heads]
    d2 = [_mm1(dg[h], dg[h]) for h in heads]
    d4 = [_mm1(d2[h], d2[h]) for h in heads]
    x = [d2[h] - dg[h] - _mm1(dg[h], d2[h]) for h in heads]
    d8 = [_mm1(d4[h], d4[h]) for h in heads]
    x = [x[h] + d4[h] + _mm1(x[h], d4[h]) for h in heads]
    t16 = [x[h] + d8[h] + _mm1(x[h], d8[h]) for h in heads]
    n1 = [off[h] + _mm1(t16[h], off[h]) for h in heads]
    n2 = [_mm1(n1[h], n1[h]) for h in heads]
    y1 = [t16[h] - n1[h] - _mm1(n1[h], t16[h]) for h in heads]
    t_m = [y1[h] + n2[h] + _mm1(n2[h], y1[h]) for h in heads]
    rhs = [ls[h][0:cl] + av[h][0:cl] for h in heads]
    u = [rhs[h] + _mm1(t_m[h], rhs[h]) for h in heads]
    vu = [jnp.concatenate([v_h[h], (-u[h]).astype(BF16)], axis=0) for h in heads]
    kb = [jnp.concatenate([k_dh[h], b_dh[h]], axis=0) for h in heads]
    states = [s0[h] * d_ch[h] + _mm_tn(vu[h], kb[h]) for h in heads]
    ys = [ls[h][cl:2 * cl] + av[h][cl:2 * cl] - _mm(a_rb[h], u[h].astype(BF16)) for h in heads]
    yf_ref[...] = jnp.concatenate(ys[:RWKV_HEADS], axis=1)
    yb_ref[...] = jnp.concatenate(ys[RWKV_HEADS:], axis=1)
    sf_ref[...] = jnp.concatenate(states[:RWKV_HEADS], axis=0)
    sb_ref[...] = jnp.concatenate(states[RWKV_HEADS:], axis=0)

    @pl.when(c == nc - 1)
    def _():
        soutf_ref[...] = sf_ref[...]
        soutb_ref[...] = sb_ref[...]


def _wkv_scan(feats, s0_f, s0_b, bsz, t_len):
    r, v, kk, _, _, lw_f, k_f, b_f, lw_b, k_b, b_b = feats
    nc = t_len // WKV_CHUNK
    rows_f = pl.BlockSpec((WKV_CHUNK, GROUP_W), lambda bi, c: (bi * nc + c, 0))
    rows_b = pl.BlockSpec((WKV_CHUNK, GROUP_W), lambda bi, c: (bi * nc + nc - 1 - c, 0))
    state = pl.BlockSpec((None, GROUP_W, RWKV_HEAD), lambda bi, c: (bi, 0, 0))
    y_shape = jax.ShapeDtypeStruct((bsz * t_len, GROUP_W), F32)
    s_shape = jax.ShapeDtypeStruct((bsz, GROUP_W, RWKV_HEAD), F32)
    return pl.pallas_call(
        functools.partial(_wkv_kernel, nc=nc),
        out_shape=(y_shape, y_shape, s_shape, s_shape),
        grid=(bsz, nc),
        in_specs=[rows_f] * 6 + [rows_b] * 6 + [state, state],
        out_specs=(rows_f, rows_b, state, state),
        scratch_shapes=[pltpu.VMEM((GROUP_W, RWKV_HEAD), F32)] * 2,
        compiler_params=_params("parallel", "arbitrary"),
        name="wkv7_chunked",
    )(r, v, kk, lw_f, k_f, b_f, r, v, kk, lw_b, k_b, b_b, s0_f, s0_b)


def _rwkv_out_kernel(yf_ref, yb_ref, bonus_ref, g_ref, lng_ref, lnb_ref, bd_ref, o_ref):
    bd = bd_ref[...]
    y = yf_ref[...] + yb_ref[...]
    inv_n = np.float32(1.0 / RWKV_HEAD)
    yc = y - _seg_sum(y, bd) * inv_n
    var = _seg_sum(yc * yc, bd) * inv_n
    yn = yc * lax.rsqrt(var + RWKV_GN_EPS) * lng_ref[...] + lnb_ref[...]
    o_ref[...] = ((yn + bonus_ref[...]) * g_ref[...]).astype(BF16)


def _rwkv_out(y_f, y_b, bonus, g, ln_g, ln_b, bd, t_len, mix=None, tm=256):
    m = y_f.shape[0]
    tm = min(tm, t_len)
    rows = pl.BlockSpec((tm, 1024), lambda i: (i, 0))
    in_specs = [rows] * 4 + [_full((1, 1024)), _full((1, 1024)), _full((128, 128))]
    args = [y_f, y_b, bonus, g, ln_g.reshape(1, -1), ln_b.reshape(1, -1), bd]
    out_rows, out_width, aliases, body = rows, GROUP_W, {}, _rwkv_out_kernel
    if mix is not None:
        in_specs.append(pl.BlockSpec(memory_space=pl.ANY))
        args.append(mix)
        out_rows = pl.BlockSpec((tm, 1024), lambda i: (i, 3))
        out_width, aliases = mix.shape[-1], {len(args) - 1: 0}

        def body(*refs):
            _rwkv_out_kernel(*refs[:-2], refs[-1])

    return pl.pallas_call(
        body,
        out_shape=jax.ShapeDtypeStruct((m, out_width), BF16),
        grid=(m // tm,),
        in_specs=in_specs,
        out_specs=out_rows,
        input_output_aliases=aliases,
        compiler_params=_params("parallel"),
        name="rwkv_groupnorm_gate",
    )(*args)


def _w_in_layout_kernel(wt_ref, o_ref):
    tr = o_ref.shape[0]
    dst = 0
    for src, width, padded in _PROJ_LAYOUT:
        step = min(padded, 256)
        for off in range(0, padded, step):
            piece = min(step, padded - off)
            real = 0 if src is None else max(0, min(piece, width - off))
            if real == 0:
                block = jnp.zeros((tr, piece), BF16)
            else:
                rows = wt_ref[src + off:src + off + real, :]
                if real < piece:
                    rows = jnp.concatenate([rows, jnp.zeros((piece - real, tr), F32)], axis=0)
                block = rows.T.astype(BF16)
            o_ref[:, dst + off:dst + off + piece] = block
        dst += padded


def _permute_w_in(w, tr=256):
    depth, rows, cols = w.shape
    return pl.pallas_call(
        _w_in_layout_kernel,
        out_shape=jax.ShapeDtypeStruct((depth, rows, N_PROJ), BF16),
        grid=(depth, rows // tr),
        in_specs=[pl.BlockSpec((None, cols, tr), lambda l, i: (l, 0, i))],
        out_specs=pl.BlockSpec((None, tr, N_PROJ), lambda l, i: (l, i, 0)),
        compiler_params=_params("parallel", "parallel"),
        name="w_in_layout",
    )(jnp.swapaxes(w, 1, 2))


def _permute_w_qb(w):
    w = w.reshape(w.shape[0], MLA_HEADS, MLA_NOPE + MLA_ROPE)
    w = jnp.pad(w, ((0, 0), (0, 0), (0, MLA_HEAD_PAD - MLA_NOPE - MLA_ROPE)))
    return w.reshape(w.shape[0], MLA_HEADS * MLA_HEAD_PAD).astype(BF16)


def _permute_w_kvb(w):
    w = w.reshape(w.shape[0], MLA_HEADS, MLA_NOPE + MLA_V)
    return jnp.concatenate([w[:, :, :MLA_NOPE].reshape(w.shape[0], -1),
                            w[:, :, MLA_NOPE:].reshape(w.shape[0], -1)], axis=1).astype(BF16)


def _rope_tables(pos_row, pos_col, half):
    inv = ROPE_BASE ** (-jnp.arange(half, dtype=F32) / half)
    zeros = jnp.zeros((pos_row.shape[0], half), F32)
    cos, sin_a, sin_b = [], [], []
    for pos in (pos_row, pos_col):
        ang = pos.astype(F32)[:, None] * inv[None, :]
        c, s = jnp.cos(ang), jnp.sin(ang)
        cos += [c, c]
        sin_a += [-s, zeros]
        sin_b += [zeros, s]
    pad = jnp.zeros((pos_row.shape[0], 128 - 4 * half), F32)
    return tuple(jnp.concatenate(t + [pad], axis=1) for t in (cos, sin_a, sin_b))


def _block_diag_ones():
    i = np.arange(128)
    return jnp.asarray((i[:, None] // RWKV_HEAD) == (i[None, :] // RWKV_HEAD), BF16)


def _split_w(w):
    hi = w.astype(BF16)
    return hi, (w - hi.astype(F32)).astype(BF16)


def _lru_gate_weights(wa, ba, wi, bi):
    w = jnp.concatenate([wa[0], wi[0], wa[1], wi[1]], axis=-1)
    b = jnp.concatenate([t.reshape(LRU_BLOCKS, 1, LRU_BLOCK_W) for t in (ba[0], bi[0], ba[1], bi[1])], axis=-1)
    return _split_w(w) + (b,)


def _rwkv_weights(mu_prev, mu_next, w0, w2, a0, a2, g2, k_k, k_a, r_k, bd):
    def pad_mu(mu):
        return jnp.pad(mu, (0, 3584 - mu.shape[0])).reshape(1, 3584)

    def two_dir(w):
        z = jnp.zeros_like(w[0])
        return jnp.concatenate([jnp.concatenate([w[0], z], axis=1), jnp.concatenate([z, w[1]], axis=1)], axis=0)

    g2p = jnp.pad(g2, ((0, 256 - RWKV_GATE_LORA), (0, 0)))
    return (pad_mu(mu_prev), pad_mu(mu_next), w0.reshape(1, 2048), a0.reshape(1, 2048),
            *_split_w(two_dir(w2)), *_split_w(two_dir(a2)), *_split_w(g2p),
            k_k.reshape(1, 1024), k_a.reshape(1, 1024), r_k.reshape(1, 1024), bd)


def kernel(x, c, ctx, c_ctx, w_mod, b_mod, w_in, w_out, ln1_g, ln1_b, w_ff1, w_ff2, ln2_g, ln2_b, swa_sink, lru_conv_w, lru_conv_b, lru_wa, lru_ba, lru_wi, lru_bi, lru_lam, mla_q_norm, mla_kv_norm, mla_w_qb, mla_w_kvb, rwkv_mu_prev, rwkv_mu_next, rwkv_w0, rwkv_w2, rwkv_a0, rwkv_a2, rwkv_g2, rwkv_k_k, rwkv_k_a, rwkv_r_k, rwkv_ln_g, rwkv_ln_b):
    bsz, t_len, d = x.shape
    n_ctx = ctx.shape[1]
    depth = w_mod.shape[0]

    cc = jnp.zeros((8, d), F32).at[:bsz].set(c).at[bsz].set(c_ctx)
    mod = _modulation(cc, w_mod, b_mod)

    pos = jnp.arange(t_len, dtype=jnp.int32)
    row, col = pos // GRID_W, pos % GRID_W
    swa_tables = _rope_tables(row, col, SWA_HEAD_DIM // 4)
    mla_tables = _rope_tables(row, col, MLA_ROPE // 4)
    bd = _block_diag_ones()
    zeros_h = jnp.zeros((bsz, 1, GROUP_W), F32)
    zeros_s = jnp.zeros((bsz, RWKV_HEADS * RWKV_HEAD, RWKV_HEAD), F32)

    xl = x.reshape(bsz * t_len, d)
    xc = ctx.reshape(bsz * n_ctx, d)

    w_in_p = _permute_w_in(w_in)
    w_out_b = w_out.astype(BF16)
    w_ff1_b = w_ff1.astype(BF16)
    w_ff2_b = w_ff2.astype(BF16)

    for l in range(depth):
        with_ctx = l < depth - 1
        chunks = [mod[l, :, k * d:(k + 1) * d] for k in range(6)]
        lat = [m[:bsz][:, None, :] for m in chunks]
        cxm = [m[bsz:bsz + 1][:, None, :] for m in chunks]

        p = _ln_mod_matmul(xl, lat[0], lat[1], w_in_p, l, act=None, out_dtype=F32, tm=1024, tn=512)
        pc = _ln_mod_matmul(xc, cxm[0], cxm[1], w_in_p, l, act=None, out_dtype=F32)

        q_a, k_a, v_a = _swa_prep(p, t_len, swa_tables)
        qc_a, kc_a, vc_a = _swa_prep(pc, n_ctx, None)
        mix = jnp.zeros((bsz * t_len, 4 * GROUP_W), BF16)
        mix = _swa_attn(q_a, k_a, v_a, kc_a, vc_a, swa_sink[l], bsz, t_len, n_ctx, mix=mix)

        gate_w = _lru_gate_weights(lru_wa[l], lru_ba[l], lru_wi[l], lru_bi[l])
        lru_args = (lru_conv_w[l], lru_conv_b[l].reshape(1, -1), *gate_w, lru_lam[l])
        caf, cbf, cab, cbb = _lru_coef(pc, n_ctx, *lru_args)
        hc_f, hlast_f = _lru_scan(caf, cbf, zeros_h, bsz, n_ctx, reverse=False)
        oc_lru, hlast_b = _lru_scan(cab, cbb, zeros_h, bsz, n_ctx, reverse=True, h_other=hc_f, p=pc)
        laf, lbf, lab, lbb = _lru_coef(p, t_len, *lru_args)
        h_f, _ = _lru_scan(laf, lbf, hlast_f, bsz, t_len, reverse=False)
        mix, _ = _lru_scan(lab, lbb, hlast_b, bsz, t_len, reverse=True, h_other=h_f, p=p, mix=mix)

        wq = _permute_w_qb(mla_w_qb[l])
        wkv = _permute_w_kvb(mla_w_kvb[l])
        q_c, k_c, v_c = _mla_proj(p, t_len, mla_q_norm[l], mla_kv_norm[l], wq, wkv, mla_tables)
        qc_c, kc_c, vc_c = _mla_proj(pc, n_ctx, mla_q_norm[l], mla_kv_norm[l], wq, wkv, None)
        mix = _mla_flash(q_c, kc_c, vc_c, bsz, t_len, n_ctx, k=k_c, v=v_c, mix=mix)

        rw = _rwkv_weights(rwkv_mu_prev[l], rwkv_mu_next[l], rwkv_w0[l], rwkv_w2[l], rwkv_a0[l], rwkv_a2[l],
                           rwkv_g2[l], rwkv_k_k[l], rwkv_k_a[l], rwkv_r_k[l], bd)
        fc = _rwkv_feat(pc, n_ctx, rw)
        yc_f, yc_b, s_f, s_b = _wkv_scan(fc, zeros_s, zeros_s, bsz, n_ctx)
        fl = _rwkv_feat(p, t_len, rw)
        y_f, y_b, _, _ = _wkv_scan(fl, s_f, s_b, bsz, t_len)
        mix = _rwkv_out(y_f, y_b, fl[4], fl[3], rwkv_ln_g[l], rwkv_ln_b[l], bd, t_len, mix=mix)

        def tail(xin, mix, m):
            x1 = _matmul_res_ln(mix, w_out_b, l, xin, m[2], ln1_g[l], ln1_b[l], tm=512)
            hid = _ln_mod_matmul(x1, m[3], m[4], w_ff1_b, l, act="relu2", out_dtype=BF16, tm=1024)
            return _matmul_res_ln(hid, w_ff2_b, l, x1, m[5], ln2_g[l], ln2_b[l])

        if with_ctx:
            oc_swa = _swa_ctx_attn(qc_a, kc_a, vc_a, swa_sink[l], bsz, n_ctx)
            oc_mla = _mla_flash(qc_c, kc_c, vc_c, bsz, n_ctx, n_ctx)
            oc_rwkv = _rwkv_out(yc_f, yc_b, fc[4], fc[3], rwkv_ln_g[l], rwkv_ln_b[l], bd, n_ctx)
            xc = tail(xc, jnp.concatenate([oc_swa, oc_lru, oc_mla, oc_rwkv], axis=-1), cxm)

        xl = tail(xl, mix, lat)

    return xl.reshape(bsz, t_len, d)
```
